```python
import jax, jax.numpy as jnp
from jax import lax
import numpy as np

D_MODEL = 1024
BATCH = 2
SEQ = 8192
DEPTH = 2
DEC_BATCH = 8
DEC_SEQ = 16
PAST_LEN = 1024

CHUNK = 64
N_META = 16
QBLOCK = 128
H_A = 4
DK_A = 32
DV_A = 64
GATE_RANK = 16
GATE_TAU = 16.0
H_B = 4
DH_B = 64
CONV_W = 4
H_C = 8
HKV_C = 2
DH_C = 64
H_IDX = 4
D_IDX = 32
TOPK_KEYS = 256
N_EXPERTS = 16
N_GROUPS = 4
TOPK_GROUPS = 1
TOP_K = 2
D_EXPERT = 256

W_A = H_A * DV_A
W_B = H_B * DH_B
W_C = H_C * DH_C
D_MIX = W_A + W_B + W_C
LN_EPS = 1e-5
ALPHA = (2 * DEPTH) ** 0.25
BETA = (8 * DEPTH) ** -0.25

_SPLITS = (
    ('a_q', H_A * DK_A), ('a_k', H_A * DK_A), ('a_v', W_A), ('a_g', GATE_RANK), ('a_r', W_A),
    ('b_qk', 2 * W_B), ('b_v', W_B), ('b_i', H_B), ('b_f', H_B), ('b_o', W_B),
    ('c_q', W_C), ('c_k', HKV_C * DH_C), ('c_v', HKV_C * DH_C),
    ('c_qi', H_IDX * D_IDX), ('c_w', H_IDX), ('c_ki', D_IDX),
)
D_IN = sum(n for _, n in _SPLITS)

kernel_name = 'hybrid_gla_mlstm_dsa_moe_stream_step'


def split_cols(u):
    out = {}
    off = 0
    for name, n in _SPLITS:
        out[name] = u[..., off:off + n]
        off += n
    return out


def layer_norm(x, g, b):
    xf = x.astype(jnp.float32)
    mu = xf.mean(-1, keepdims=True)
    var = jnp.square(xf - mu).mean(-1, keepdims=True)
    return ((xf - mu) * lax.rsqrt(var + LN_EPS) * g + b).astype(x.dtype)


def head_norm(h, g):
    mu = h.mean(-1, keepdims=True)
    var = jnp.square(h - mu).mean(-1, keepdims=True)
    y = (h - mu) * lax.rsqrt(var + LN_EPS)
    return y.reshape(*h.shape[:-2], -1) * g.astype(jnp.float32)


def gla_block(S0, q, k, v, la):
    L = q.shape[1]
    b = jnp.cumsum(la, axis=1)
    causal = jnp.tril(jnp.ones((L, L), bool))
    diff = jnp.where(causal[None, :, :, None, None], b[:, :, None] - b[:, None, :], -jnp.inf)
    att = jnp.einsum('bthd,bshd,btshd->bhts', q, k, jnp.exp(diff))
    o = jnp.einsum('bhts,bshv->bthv', att, v) + jnp.einsum('bthd,bhdv->bthv', q * jnp.exp(b), S0)
    bL = b[:, -1]
    S = jnp.exp(bL)[..., None] * S0 + jnp.einsum('bshd,bshv->bhdv', k * jnp.exp(bL[:, None] - b), v)
    return S, o


def mlstm_block(carry, q, k, v, ig, lf):
    C0, n0, m0 = carry
    L = q.shape[1]
    b = jnp.cumsum(lf, axis=1)
    causal = jnp.tril(jnp.ones((L, L), bool))
    logw = jnp.where(causal[None, :, :, None], b[:, :, None] - b[:, None, :] + ig[:, None, :], -jnp.inf)
    log_prev = b + m0[:, None]
    m = jnp.maximum(log_prev, logw.max(axis=2))
    w = jnp.exp(logw - m[:, :, None])
    sp = jnp.exp(log_prev - m)
    qk = jnp.einsum('bthd,bshd->btsh', q, k) * w
    num = jnp.einsum('btsh,bshv->bthv', qk, v) + sp[..., None] * jnp.einsum('bhvd,bthd->bthv', C0, q)
    den = qk.sum(axis=2) + sp * jnp.einsum('bhd,bthd->bth', n0, q)
    h = num / jnp.maximum(jnp.abs(den), jnp.exp(-m))[..., None]
    mL = m[:, -1]
    wL = jnp.exp(b[:, -1:] - b + ig - mL[:, None])
    sL = jnp.exp(b[:, -1] + m0 - mL)
    C = sL[..., None, None] * C0 + jnp.einsum('bsh,bshv,bshd->bhvd', wL, v, k)
    n = sL[..., None] * n0 + jnp.einsum('bsh,bshd->bhd', wL, k)
    return (C, n, mL), h


def scan_stream(block_fn, carry, seqs):
    carry, out_meta = block_fn(carry, *[s[:, :N_META] for s in seqs])
    rest = [s[:, N_META:] for s in seqs]
    B, T = rest[0].shape[:2]
    nc = T // CHUNK
    blocks = tuple(jnp.swapaxes(r.reshape(B, nc, CHUNK, *r.shape[2:]), 0, 1) for r in rest)
    carry, outs = lax.scan(lambda c, xs: block_fn(c, *xs), carry, blocks)
    outs = jnp.swapaxes(outs, 0, 1).reshape(B, T, *outs.shape[3:])
    return carry, jnp.concatenate([out_meta, outs], axis=1)


def causal_conv(u, buf, w, b):
    T = u.shape[1]
    up = jnp.concatenate([buf, u], axis=1)
    y = sum(up[:, j:j + T] * w[j] for j in range(CONV_W)) + b
    return y, up[:, -(CONV_W - 1):]


def dsa_attend(q, qi, wi, qcid, k_all, v_all, ki_all, kcid, n_sel):
    f32 = jnp.float32
    raw = jnp.einsum('bthi,bsi->bths', qi.astype(f32), ki_all.astype(f32))
    iscore = jnp.einsum('bth,bths->bts', wi.astype(f32), jax.nn.relu(raw))
    admissible = kcid[None, :] <= qcid[:, None]
    iscore = jnp.where(admissible[None], iscore, -jnp.inf)
    _, sel = lax.top_k(iscore, n_sel)
    valid = jnp.take(kcid, sel) <= qcid[None, :, None]
    gather = jax.vmap(lambda a, i: a[i])
    k_sel = gather(k_all, sel).astype(f32)
    v_sel = gather(v_all, sel).astype(f32)
    logits = jnp.einsum('btkgd,btnkd->btkgn', q.astype(f32), k_sel) * DH_C ** -0.5
    logits = jnp.where(valid[:, :, None, None, :], logits, -jnp.inf)
    p = jax.nn.softmax(logits, axis=-1)
    o = jnp.einsum('btkgn,btnkd->btkgd', p, v_sel)
    return o.reshape(*o.shape[:2], -1)


def dsa_prompt(q, qi, wi, qcid, k_all, v_all, ki_all, kcid, n_sel):
    B, T = q.shape[:2]
    nb = -(-T // QBLOCK)
    pad = nb * QBLOCK - T

    def blk(a):
        a = jnp.pad(a, [(0, 0), (0, pad)] + [(0, 0)] * (a.ndim - 2))
        return jnp.swapaxes(a.reshape(B, nb, QBLOCK, *a.shape[2:]), 0, 1)

    qcb = jnp.pad(qcid, (0, pad), mode='edge').reshape(nb, QBLOCK)
    out = lax.map(lambda xs: dsa_attend(xs[0], xs[1], xs[2], xs[3], k_all, v_all, ki_all, kcid, n_sel),
                  (blk(q), blk(qi), blk(wi), qcb))
    return jnp.swapaxes(out, 0, 1).reshape(B, nb * QBLOCK, -1)[:, :T]


def token_mixer(h, st, qcid, kcid, n_sel, is_prompt, w_in, b_in, w_gla_gate, b_gla_gate, g_gla_norm,
                conv_w, conv_b, b_forget, g_mlstm_norm, w_out):
    f32 = jnp.float32
    ck, cv, cki, s_gla, s_c, s_n, s_m, s_conv = st
    B, T, _ = h.shape
    c = split_cols(h @ w_in + b_in)
    qa = c['a_q'].reshape(B, T, H_A, DK_A).astype(f32) * DK_A ** -0.5
    ka = c['a_k'].reshape(B, T, H_A, DK_A).astype(f32)
    va = c['a_v'].reshape(B, T, H_A, DV_A).astype(f32)
    la = jax.nn.log_sigmoid((c['a_g'] @ w_gla_gate + b_gla_gate).astype(f32)).reshape(B, T, H_A, DK_A) / GATE_TAU
    seq_a = (qa, ka, va, la)
    if is_prompt:
        S_a, oa = scan_stream(gla_block, s_gla.astype(f32), seq_a)
    else:
        S_a, oa = gla_block(s_gla.astype(f32), *seq_a)
    ya = head_norm(oa, g_gla_norm) * jax.nn.silu(c['a_r'].astype(f32))
    qk_pre = c['b_qk']
    conv_out, conv_buf = causal_conv(qk_pre, s_conv.astype(qk_pre.dtype), conv_w, conv_b)
    conv_out = jax.nn.silu(conv_out.astype(f32))
    qb = conv_out[..., :W_B].reshape(B, T, H_B, DH_B)
    kb = conv_out[..., W_B:].reshape(B, T, H_B, DH_B) * DH_B ** -0.5
    vb = c['b_v'].reshape(B, T, H_B, DH_B).astype(f32)
    ig = c['b_i'].astype(f32)
    lf = jax.nn.log_sigmoid(c['b_f'].astype(f32) + b_forget.astype(f32))
    carry_b = (s_c.astype(f32), s_n.astype(f32), s_m.astype(f32))
    seq_b = (qb, kb, vb, ig, lf)
    if is_prompt:
        (C_b, n_b, m_b), hb = scan_stream(mlstm_block, carry_b, seq_b)
    else:
        (C_b, n_b, m_b), hb = mlstm_block(carry_b, *seq_b)
    yb = head_norm(hb, g_mlstm_norm) * jax.nn.sigmoid(c['b_o'].astype(f32))
    q_c = c['c_q'].reshape(B, T, HKV_C, H_C // HKV_C, DH_C)
    k_new = c['c_k'].reshape(B, T, HKV_C, DH_C)
    v_new = c['c_v'].reshape(B, T, HKV_C, DH_C)
    ki_new = c['c_ki']
    qi = c['c_qi'].reshape(B, T, H_IDX, D_IDX) * D_IDX ** -0.5
    wi = c['c_w'] * H_IDX ** -0.5
    k_all = jnp.concatenate([ck.astype(h.dtype), k_new], axis=1)
    v_all = jnp.concatenate([cv.astype(h.dtype), v_new], axis=1)
    ki_all = jnp.concatenate([cki.astype(h.dtype), ki_new], axis=1)
    if is_prompt:
        yc = dsa_prompt(q_c, qi, wi, qcid, k_all, v_all, ki_all, kcid, n_sel)
    else:
        yc = dsa_attend(q_c, qi, wi, qcid, k_all, v_all, ki_all, kcid, n_sel)
    y = jnp.concatenate([ya, yb, yc], axis=-1).astype(h.dtype) @ w_out
    return y, (k_new, v_new, ki_new, S_a, C_b, n_b, m_b, conv_buf)


def moe(h, w_router, b_router, w_gate, w_up, w_down):
    f32 = jnp.float32
    shp = h.shape
    x = h.reshape(-1, shp[-1])
    s = jax.nn.sigmoid((x @ w_router).astype(f32))
    sb = s + b_router.astype(f32)
    eg = N_EXPERTS // N_GROUPS
    grp_score = lax.top_k(sb.reshape(-1, N_GROUPS, eg), TOP_K)[0].sum(-1)
    _, gsel = lax.top_k(grp_score, TOPK_GROUPS)
    gmask = jax.nn.one_hot(gsel, N_GROUPS, dtype=f32).sum(-2) > 0
    emask = jnp.repeat(gmask, eg, axis=-1)
    _, esel = lax.top_k(jnp.where(emask, sb, -jnp.inf), TOP_K)
    wsel = jnp.take_along_axis(s, esel, axis=-1)
    wsel = wsel / wsel.sum(-1, keepdims=True)
    gates = jnp.einsum('nk,nke->ne', wsel, jax.nn.one_hot(esel, N_EXPERTS, dtype=f32))
    y = jnp.zeros(x.shape, f32)
    for e in range(N_EXPERTS):
        he = jax.nn.silu(x @ w_gate[e]) * (x @ w_up[e])
        y = y + gates[:, e:e + 1] * (he @ w_down[e]).astype(f32)
    return y.astype(h.dtype).reshape(shp)


def trunk(h, caches, qcid, kcid, n_sel, is_prompt, w_in, b_in, w_gla_gate, b_gla_gate, g_gla_norm,
          conv_w, conv_b, b_forget, g_mlstm_norm, w_out, ln1_g, ln1_b, w_router, b_router,
          w_gate, w_up, w_down, ln2_g, ln2_b):
    new = [[] for _ in caches]
    for l in range(DEPTH):
        mix, st = token_mixer(h, tuple(cc[l] for cc in caches), qcid, kcid, n_sel, is_prompt,
                              w_in[l], b_in[l], w_gla_gate[l], b_gla_gate[l], g_gla_norm[l],
                              conv_w[l], conv_b[l], b_forget[l], g_mlstm_norm[l], w_out[l])
        h = layer_norm(ALPHA * h + mix, ln1_g[l], ln1_b[l])
        h = layer_norm(ALPHA * h + moe(h, w_router, b_router, w_gate[l], w_up[l], w_down[l]), ln2_g[l], ln2_b[l])
        for acc, a in zip(new, st):
            acc.append(a.astype(h.dtype))
    return h, [jnp.stack(acc) for acc in new]


def setup_inputs(seed: int = 0) -> dict:
    key = jax.random.key(seed)
    ks = iter(jax.random.split(key, 48))

    def nrm(shape, scale=1.0):
        return jax.random.normal(next(ks), shape, jnp.float32) * scale

    d = {}
    d['x_prompt'] = nrm((BATCH, SEQ, D_MODEL))
    d['x_sample'] = nrm((DEC_BATCH, DEC_SEQ, D_MODEL))
    d['cache_k'] = nrm((DEPTH, DEC_BATCH, PAST_LEN, HKV_C, DH_C))
    d['cache_v'] = nrm((DEPTH, DEC_BATCH, PAST_LEN, HKV_C, DH_C))
    d['cache_kidx'] = nrm((DEPTH, DEC_BATCH, PAST_LEN, D_IDX))
    d['state_gla'] = nrm((DEPTH, DEC_BATCH, H_A, DK_A, DV_A), 0.5)
    d['state_mlstm_c'] = nrm((DEPTH, DEC_BATCH, H_B, DH_B, DH_B), 0.5)
    d['state_mlstm_n'] = nrm((DEPTH, DEC_BATCH, H_B, DH_B), 0.5)
    d['state_mlstm_m'] = nrm((DEPTH, DEC_BATCH, H_B))
    d['state_conv'] = nrm((DEPTH, DEC_BATCH, CONV_W - 1, 2 * W_B))
    d['meta_tokens'] = nrm((N_META, D_MODEL))
    d['ln_in_g'] = 1.0 + nrm((D_MODEL,), 0.02)
    d['ln_in_b'] = nrm((D_MODEL,), 0.02)
    d['w_in'] = nrm((DEPTH, D_MODEL, D_IN), D_MODEL ** -0.5)
    d['b_in'] = nrm((DEPTH, D_IN), 0.02)
    d['w_gla_gate'] = nrm((DEPTH, GATE_RANK, H_A * DK_A), GATE_RANK ** -0.5)
    d['b_gla_gate'] = nrm((DEPTH, H_A * DK_A), 0.1)
    d['g_gla_norm'] = 1.0 + nrm((DEPTH, W_A), 0.02)
    d['conv_w'] = nrm((DEPTH, CONV_W, 2 * W_B), CONV_W ** -0.5)
    d['conv_b'] = nrm((DEPTH, 2 * W_B), 0.02)
    d['b_forget'] = jnp.linspace(3.0, 6.0, H_B, dtype=jnp.float32)[None] + nrm((DEPTH, H_B), 0.1)
    d['g_mlstm_norm'] = 1.0 + nrm((DEPTH, W_B), 0.02)
    d['w_out'] = nrm((DEPTH, D_MIX, D_MODEL), BETA * D_MIX ** -0.5)
    d['ln1_g'] = 1.0 + nrm((DEPTH, D_MODEL), 0.02)
    d['ln1_b'] = nrm((DEPTH, D_MODEL), 0.02)
    d['w_router'] = nrm((D_MODEL, N_EXPERTS), D_MODEL ** -0.5)
    d['b_router'] = nrm((N_EXPERTS,), 0.01)
    d['w_gate'] = nrm((DEPTH, N_EXPERTS, D_MODEL, D_EXPERT), D_MODEL ** -0.5)
    d['w_up'] = nrm((DEPTH, N_EXPERTS, D_MODEL, D_EXPERT), D_MODEL ** -0.5)
    d['w_down'] = nrm((DEPTH, N_EXPERTS, D_EXPERT, D_MODEL), BETA * D_EXPERT ** -0.5)
    d['ln2_g'] = 1.0 + nrm((DEPTH, D_MODEL), 0.02)
    d['ln2_b'] = nrm((DEPTH, D_MODEL), 0.02)
    return d


def reference(x_prompt, x_sample, cache_k, cache_v, cache_kidx, state_gla, state_mlstm_c, state_mlstm_n,
              state_mlstm_m, state_conv, meta_tokens, ln_in_g, ln_in_b, w_in, b_in, w_gla_gate, b_gla_gate,
              g_gla_norm, conv_w, conv_b, b_forget, g_mlstm_norm, w_out, ln1_g, ln1_b, w_router, b_router,
              w_gate, w_up, w_down, ln2_g, ln2_b):
    i32 = jnp.int32
    f32 = jnp.float32
    B, S_p, D = x_prompt.shape
    dt = x_prompt.dtype
    meta = jnp.broadcast_to(meta_tokens.astype(dt)[None], (B, N_META, D))
    hp = layer_norm(jnp.concatenate([meta, x_prompt], axis=1), ln_in_g, ln_in_b)
    cid_p = jnp.concatenate([jnp.full((N_META,), -1, i32), jnp.arange(S_p, dtype=i32) // CHUNK])
    n_sel_p = min(TOPK_KEYS, S_p // 4)
    zero_p = (jnp.zeros((DEPTH, B, 0, HKV_C, DH_C), dt), jnp.zeros((DEPTH, B, 0, HKV_C, DH_C), dt),
              jnp.zeros((DEPTH, B, 0, D_IDX), dt), jnp.zeros((DEPTH, B, H_A, DK_A, DV_A), f32),
              jnp.zeros((DEPTH, B, H_B, DH_B, DH_B), f32), jnp.zeros((DEPTH, B, H_B, DH_B), f32),
              jnp.zeros((DEPTH, B, H_B), f32), jnp.zeros((DEPTH, B, CONV_W - 1, 2 * W_B), dt))
    hp, p_states = trunk(hp, zero_p, cid_p, cid_p, n_sel_p, True, w_in, b_in, w_gla_gate, b_gla_gate,
                         g_gla_norm, conv_w, conv_b, b_forget, g_mlstm_norm, w_out, ln1_g, ln1_b,
                         w_router, b_router, w_gate, w_up, w_down, ln2_g, ln2_b)
    y_prompt = hp[:, N_META:]
    P = cache_k.shape[2]
    Ts = x_sample.shape[1]
    hs = layer_norm(x_sample, ln_in_g, ln_in_b)
    kcid_s = jnp.arange(P + Ts, dtype=i32) // CHUNK
    qcid_s = kcid_s[P:]
    n_sel_s = min(TOPK_KEYS, (P + Ts) // 4)
    caches_s = (cache_k, cache_v, cache_kidx, state_gla, state_mlstm_c, state_mlstm_n, state_mlstm_m, state_conv)
    hs, s_states = trunk(hs, caches_s, qcid_s, kcid_s, n_sel_s, False, w_in, b_in, w_gla_gate, b_gla_gate,
                         g_gla_norm, conv_w, conv_b, b_forget, g_mlstm_norm, w_out, ln1_g, ln1_b,
                         w_router, b_router, w_gate, w_up, w_down, ln2_g, ln2_b)
    y_sample = hs
    p_k, p_v, p_kidx, p_gla, p_c, p_n, p_m, p_conv = p_states
    s_k, s_v, s_kidx, s_gla, s_c, s_n, s_m, s_conv = s_states
    return (y_prompt, y_sample, p_k, p_v, p_kidx, p_gla, p_c, p_n, p_m, p_conv,
            s_k, s_v, s_kidx, s_gla, s_c, s_n, s_m, s_conv)
```

```python
import functools

import jax
import jax.numpy as jnp
from jax import lax
from jax.experimental import pallas as pl
from jax.experimental.pallas import tpu as pltpu

F32 = jnp.float32
BF16 = jnp.bfloat16
I32 = jnp.int32

CHUNK = 64
N_META = 16
H_A, DK_A, DV_A = 4, 32, 64
GATE_RANK, GATE_TAU = 16, 16.0
H_B, DH_B, CONV_W = 4, 64, 4
H_C, HKV_C, DH_C = 8, 2, 64
H_IDX, D_IDX = 4, 32
TOPK_KEYS = 256
N_EXPERTS, N_GROUPS, TOP_K, D_EXPERT = 16, 4, 2, 256
W_A, W_B, W_C = H_A * DV_A, H_B * DH_B, H_C * DH_C
LN_EPS = 1e-5

LANES = 128
SUBLANES = 8
VMEM_LIMIT = 56 * 1024 * 1024
KEY_BLOCK = 256
NEG_BIG = -1e30
INT_MIN = -2 ** 31

_SPLITS = (
    ('a_q', H_A * DK_A), ('a_k', H_A * DK_A), ('a_v', W_A), ('a_g', GATE_RANK), ('a_r', W_A),
    ('b_qk', 2 * W_B), ('b_v', W_B), ('b_i', H_B), ('b_f', H_B), ('b_o', W_B),
    ('c_q', W_C), ('c_k', HKV_C * DH_C), ('c_v', HKV_C * DH_C),
    ('c_qi', H_IDX * D_IDX), ('c_w', H_IDX), ('c_ki', D_IDX),
)
_GROUP_A = (('a_q', 128), ('a_k', 128), ('a_v', 256), ('a_r', 256), ('a_g', 16), (None, 112))
_GROUP_B = (('b_qk', 512), ('b_v', 256), ('b_o', 256), ('b_i', 4), ('b_f', 4), (None, 120))
_GROUP_C = (('c_q', 512), ('c_k', 128), ('c_v', 128), ('c_qi', 128), ('c_ki', 32), ('c_w', 4), (None, 92))
_GROUP_WIDTHS = tuple(sum(n for _, n in g) for g in (_GROUP_A, _GROUP_B, _GROUP_C))


def _dot(a, b):
    return jnp.dot(a.astype(BF16), b.astype(BF16), preferred_element_type=F32)


def _dot_nt(a, b):
    return lax.dot_general(a.astype(BF16), b.astype(BF16), (((1,), (1,)), ((), ())), preferred_element_type=F32)


def _dot_tn(a, b):
    return lax.dot_general(a.astype(BF16), b.astype(BF16), (((0,), (0,)), ((), ())), preferred_element_type=F32)


def _dot_exact(a, b):
    return jnp.dot(a, b, precision=lax.Precision.HIGHEST, preferred_element_type=F32)


def _log_sigmoid(x):
    return jnp.minimum(x, 0.0) - jnp.log(1.0 + jnp.exp(-jnp.abs(x)))


def _sigmoid(x):
    return 1.0 / (1.0 + jnp.exp(-x))


def _silu(x):
    return x * _sigmoid(x)


def _ln(x, g, b):
    mu = jnp.mean(x, axis=-1, keepdims=True)
    xc = x - mu
    var = jnp.mean(xc * xc, axis=-1, keepdims=True)
    return xc * lax.rsqrt(var + LN_EPS) * g + b


def _pick_tile(n, cap):
    best = None
    for t in range(SUBLANES, min(n, cap) + 1, SUBLANES):
        if n % t == 0:
            best = t
    assert best is not None, n
    return best


def _stack_heads(x, n_heads, width):
    lane_head = lax.broadcasted_iota(I32, (1, n_heads * width), 1) // width
    return jnp.concatenate([jnp.where(lane_head == h, x, 0.0) for h in range(n_heads)], axis=0)


def _head_norm_stacked(o_st, mask_st, n_heads, L, width):
    o_st = jnp.where(mask_st, o_st, 0.0)
    mu = jnp.sum(o_st, axis=-1, keepdims=True) * (1.0 / width)
    xc = jnp.where(mask_st, o_st - mu, 0.0)
    var = jnp.sum(xc * xc, axis=-1, keepdims=True) * (1.0 / width)
    yn = xc * lax.rsqrt(var + LN_EPS)
    y = yn[0:L]
    for h in range(1, n_heads):
        y = y + yn[h * L:(h + 1) * L]
    return y


def _inproj_kernel(x_ref, g_ref, b_ref, w_ref, bias_ref, *out_refs, apply_ln):
    x = x_ref[...]
    if apply_ln:
        h_ref, *u_refs = out_refs
        x = _ln(x, g_ref[...], b_ref[...])
        h_ref[...] = x
    else:
        u_refs = out_refs
    xb = x.astype(BF16)
    off = 0
    for r, n in zip(u_refs, _GROUP_WIDTHS):
        r[...] = jnp.dot(xb, w_ref[:, off:off + n], preferred_element_type=F32) + bias_ref[:, off:off + n]
        off += n


def _inproj(x, ln_g, ln_b, w_packed, b_packed, apply_ln):
    n, d = x.shape
    tm = _pick_tile(n, 384)
    dp = w_packed.shape[1]
    row = lambda i: (i, 0)
    fix = lambda i: (0, 0)
    out_shape = [jax.ShapeDtypeStruct((n, w), F32) for w in _GROUP_WIDTHS]
    out_specs = [pl.BlockSpec((tm, w), row) for w in _GROUP_WIDTHS]
    if apply_ln:
        out_shape = [jax.ShapeDtypeStruct((n, d), F32)] + out_shape
        out_specs = [pl.BlockSpec((tm, d), row)] + out_specs
    outs = pl.pallas_call(
        functools.partial(_inproj_kernel, apply_ln=apply_ln),
        grid=(n // tm,),
        in_specs=[pl.BlockSpec((tm, d), row), pl.BlockSpec((1, d), fix), pl.BlockSpec((1, d), fix),
                  pl.BlockSpec((d, dp), fix), pl.BlockSpec((1, dp), fix)],
        out_specs=out_specs, out_shape=out_shape,
        compiler_params=pltpu.CompilerParams(dimension_semantics=("parallel",), vmem_limit_bytes=VMEM_LIMIT),
        name="inproj_ln" if apply_ln else "inproj",
    )(x, ln_g.reshape(1, d), ln_b.reshape(1, d), w_packed, b_packed)
    if apply_ln:
        return outs[0], outs[1], outs[2], outs[3]
    return x, outs[0], outs[1], outs[2]


def _gla_kernel(u_ref, wg_ref, bg_ref, gn_ref, s0_ref, y_ref, sout_ref, st_sc, *, L, lo, hi):
    c = pl.program_id(1)

    @pl.when(c == 0)
    def _():
        st_sc[...] = s0_ref[0]

    u = u_ref[...]
    q = u[:, 0:128] * (DK_A ** -0.5)
    k = u[:, 128:256]
    v = u[:, 256:512]
    r = u[:, 512:768]
    g = u[:, 768:896]
    la = _log_sigmoid(_dot(g, wg_ref[...]) + bg_ref[...]) * (1.0 / GATE_TAU)
    pos = c * L + lax.broadcasted_iota(I32, (L, 1), 0)
    real = (pos >= lo) & (pos < hi)
    la = jnp.where(real, la, 0.0)
    k = jnp.where(real, k, 0.0)
    ti = lax.broadcasted_iota(I32, (L, L), 0)
    si = lax.broadcasted_iota(I32, (L, L), 1)
    b = _dot_exact((ti >= si).astype(F32), la)
    bmid = b[L // 2 - 1:L // 2, :]
    bl = b[L - 1:L, :]
    qs = _stack_heads(q * jnp.exp(b - bmid), H_A, DK_A)
    qbs = _stack_heads(q * jnp.exp(b), H_A, DK_A)
    kt = k * jnp.exp(bmid - b)
    kh = k * jnp.exp(bl - b)
    att = _dot_nt(qs, kt)
    tr = lax.broadcasted_iota(I32, (H_A * L, L), 0) % L
    sc = lax.broadcasted_iota(I32, (H_A * L, L), 1)
    att = jnp.where(sc <= tr, att, 0.0)
    row_head = lax.broadcasted_iota(I32, (H_A * L, W_A), 0) // L
    lane_head = lax.broadcasted_iota(I32, (H_A * L, W_A), 1) // DV_A
    mask_st = row_head == lane_head
    st = st_sc[...]
    o_st = jnp.where(mask_st, _dot(att, v), 0.0) + _dot_nt(qbs, st)
    y = _head_norm_stacked(o_st, mask_st, H_A, L, DV_A)
    y_ref[...] = y * gn_ref[...] * _silu(r)
    bd = (lax.broadcasted_iota(I32, (W_A, H_A * DK_A), 0) // DV_A) == (lax.broadcasted_iota(I32, (W_A, H_A * DK_A), 1) // DK_A)
    st_new = st * jnp.exp(bl) + jnp.where(bd, _dot_tn(v, kh), 0.0)
    st_sc[...] = st_new
    sout_ref[0] = st_new


def _gla(ua, wg, bg, gn, s0t, *, nb, nc, L, lo, hi):
    rows = ua.shape[0]
    return pl.pallas_call(
        functools.partial(_gla_kernel, L=L, lo=lo, hi=hi),
        grid=(nb, nc),
        in_specs=[pl.BlockSpec((L, _GROUP_WIDTHS[0]), lambda b, c: (b * nc + c, 0)),
                  pl.BlockSpec((LANES, LANES), lambda b, c: (0, 0)),
                  pl.BlockSpec((1, LANES), lambda b, c: (0, 0)),
                  pl.BlockSpec((1, W_A), lambda b, c: (0, 0)),
                  pl.BlockSpec((1, W_A, H_A * DK_A), lambda b, c: (b, 0, 0))],
        out_specs=[pl.BlockSpec((L, W_A), lambda b, c: (b * nc + c, 0)),
                   pl.BlockSpec((1, W_A, H_A * DK_A), lambda b, c: (b, 0, 0))],
        out_shape=[jax.ShapeDtypeStruct((rows, W_A), F32), jax.ShapeDtypeStruct((nb, W_A, H_A * DK_A), F32)],
        scratch_shapes=[pltpu.VMEM((W_A, H_A * DK_A), F32)],
        compiler_params=pltpu.CompilerParams(dimension_semantics=("parallel", "arbitrary"), vmem_limit_bytes=VMEM_LIMIT),
        name="gla_scan",
    )(ua, wg, bg, gn, s0t)


def _mlstm_kernel(u_ref, cw_ref, cb_ref, bf_ref, gn_ref, c0_ref, n0_ref, m0_ref, conv0_ref,
                  y_ref, cout_ref, nout_ref, mout_ref, c_sc, n_sc, m_sc, buf_sc, *, L, lo, hi):
    c = pl.program_id(1)
    H, D = H_B, DH_B

    @pl.when(c == 0)
    def _():
        c_sc[...] = c0_ref[0]
        n_sc[...] = n0_ref[0]
        m_sc[...] = m0_ref[0]
        buf_sc[0:SUBLANES, :] = conv0_ref[0]

    u = u_ref[...]
    pos = c * L + lax.broadcasted_iota(I32, (L, 1), 0)
    real = (pos >= lo) & (pos < hi)
    buf_sc[SUBLANES:SUBLANES + L, :] = jnp.where(real, u[:, 0:2 * W_B], 0.0)
    hist = SUBLANES - (CONV_W - 1)
    conv = buf_sc[hist:hist + L, :] * cw_ref[0:1, :]
    for j in range(1, CONV_W):
        conv = conv + buf_sc[hist + j:hist + j + L, :] * cw_ref[j:j + 1, :]
    conv = conv + cb_ref[...]
    buf_sc[0:SUBLANES, :] = buf_sc[L:L + SUBLANES, :]
    act = _silu(conv)
    q = act[:, 0:W_B]
    k = act[:, W_B:2 * W_B] * (D ** -0.5)
    v = u[:, 512:768]
    og = u[:, 768:1024]
    gts = u[:, 1024:1152]
    ig_col = jnp.where(real, gts, NEG_BIG)
    lf_col = jnp.where(real, _log_sigmoid(gts + bf_ref[...]), 0.0)
    ti = lax.broadcasted_iota(I32, (L, L), 0)
    si = lax.broadcasted_iota(I32, (L, L), 1)
    b_col = _dot_exact((ti >= si).astype(F32), lf_col)
    lane = lax.broadcasted_iota(I32, (1, LANES), 1)
    z = jnp.where(lane < H, ig_col, b_col)
    zt = jnp.concatenate([z, jnp.zeros((LANES - L, LANES), F32)], axis=0).T
    b_st = jnp.concatenate([b_col[:, H + h:H + h + 1] for h in range(H)], axis=0)
    m0_st = jnp.concatenate([jnp.broadcast_to(m_sc[0:1, h:h + 1], (L, 1)) for h in range(H)], axis=0)
    rowv_st = jnp.concatenate(
        [jnp.broadcast_to(zt[h:h + 1, 0:L] - zt[H + h:H + h + 1, 0:L], (L, L)) for h in range(H)], axis=0)
    tr = lax.broadcasted_iota(I32, (H * L, L), 0) % L
    sc = lax.broadcasted_iota(I32, (H * L, L), 1)
    logw = jnp.where(sc <= tr, b_st + rowv_st, NEG_BIG)
    lp = b_st + m0_st
    m_st = jnp.maximum(lp, jnp.max(logw, axis=1, keepdims=True))
    w = jnp.exp(logw - m_st)
    sp = jnp.exp(lp - m_st)
    qs = _stack_heads(q, H, D)
    qk = _dot_nt(qs, k) * w
    row_head = lax.broadcasted_iota(I32, (H * L, W_B), 0) // L
    lane_head = lax.broadcasted_iota(I32, (H * L, W_B), 1) // D
    mask_st = row_head == lane_head
    cbd = c_sc[...]
    nrow = n_sc[...]
    num = jnp.where(mask_st, _dot(qk, v), 0.0) + sp * _dot(qs, cbd)
    den = jnp.sum(qk, axis=1, keepdims=True) + sp * jnp.sum(qs * nrow, axis=1, keepdims=True)
    hh = num / jnp.maximum(jnp.abs(den), jnp.exp(-m_st))
    y = _head_norm_stacked(hh, mask_st, H, L, D)
    y_ref[...] = y * gn_ref[...] * _sigmoid(og)
    lh = lax.broadcasted_iota(I32, (1, W_B), 1) // D
    wlk = jnp.zeros((L, W_B), F32)
    sl_lanes = jnp.zeros((1, W_B), F32)
    m_new = jnp.zeros((1, LANES), F32)
    for h in range(H):
        ml = m_st[h * L + L - 1:h * L + L, :]
        bl = b_st[h * L + L - 1:h * L + L, :]
        wl = jnp.exp(bl - b_col[:, H + h:H + h + 1] + ig_col[:, h:h + 1] - ml)
        sl = jnp.exp(bl + m_sc[0:1, h:h + 1] - ml)
        wlk = jnp.where(lh == h, wl, wlk)
        sl_lanes = jnp.where(lh == h, sl, sl_lanes)
        m_new = jnp.where(lane == h, ml, m_new)
    kw = k * wlk
    bd = (lax.broadcasted_iota(I32, (W_B, W_B), 0) // D) == (lax.broadcasted_iota(I32, (W_B, W_B), 1) // D)
    c_new = sl_lanes * cbd + jnp.where(bd, _dot_tn(kw, v), 0.0)
    n_new = sl_lanes * nrow + jnp.sum(kw, axis=0, keepdims=True)
    c_sc[...] = c_new
    n_sc[...] = n_new
    m_sc[...] = m_new
    cout_ref[0] = c_new
    nout_ref[0] = n_new
    mout_ref[0] = m_new


def _mlstm(ub, cw, cb, bf, gn, c0, n0, m0, conv0, *, nb, nc, L, lo, hi):
    rows = ub.shape[0]
    fix2 = lambda b, c: (0, 0)
    per_b = lambda b, c: (b, 0, 0)
    return pl.pallas_call(
        functools.partial(_mlstm_kernel, L=L, lo=lo, hi=hi),
        grid=(nb, nc),
        in_specs=[pl.BlockSpec((L, _GROUP_WIDTHS[1]), lambda b, c: (b * nc + c, 0)),
                  pl.BlockSpec((CONV_W, 2 * W_B), fix2), pl.BlockSpec((1, 2 * W_B), fix2),
                  pl.BlockSpec((1, LANES), fix2), pl.BlockSpec((1, W_B), fix2),
                  pl.BlockSpec((1, W_B, W_B), per_b), pl.BlockSpec((1, 1, W_B), per_b),
                  pl.BlockSpec((1, 1, LANES), per_b), pl.BlockSpec((1, SUBLANES, 2 * W_B), per_b)],
        out_specs=[pl.BlockSpec((L, W_B), lambda b, c: (b * nc + c, 0)),
                   pl.BlockSpec((1, W_B, W_B), per_b), pl.BlockSpec((1, 1, W_B), per_b),
                   pl.BlockSpec((1, 1, LANES), per_b)],
        out_shape=[jax.ShapeDtypeStruct((rows, W_B), F32), jax.ShapeDtypeStruct((nb, W_B, W_B), F32),
                   jax.ShapeDtypeStruct((nb, 1, W_B), F32), jax.ShapeDtypeStruct((nb, 1, LANES), F32)],
        scratch_shapes=[pltpu.VMEM((W_B, W_B), F32), pltpu.VMEM((1, W_B), F32), pltpu.VMEM((1, LANES), F32),
                        pltpu.VMEM((L + SUBLANES, 2 * W_B), F32)],
        compiler_params=pltpu.CompilerParams(dimension_semantics=("parallel", "arbitrary"), vmem_limit_bytes=VMEM_LIMIT),
        name="mlstm_scan",
    )(ub, cw, cb, bf, gn, c0, n0, m0, conv0)


def _dsa_kernel(u_ref, k_ref, v_ref, kit_ref, y_ref, key_sc, eq_sc, acc_sc, m_sc, l_sc,
                *, TQ, KB, n_sel, adm_lo, adm_cap, end_base, end_step, idx_bits):
    c = pl.program_id(1)
    G = H_C // HKV_C
    end = jnp.minimum(end_base + end_step * c, adm_cap)
    nkb = (end + KB - 1) // KB
    u = u_ref[...]
    qi = u[:, 768:896] * (D_IDX ** -0.5)
    kw = u[:, 896:1024]
    qi_st = jnp.concatenate([qi[:, h * D_IDX:(h + 1) * D_IDX] for h in range(H_IDX)], axis=0).astype(BF16)
    wi = [kw[:, D_IDX + h:D_IDX + h + 1] * (H_IDX ** -0.5) for h in range(H_IDX)]
    lane_idx = lax.broadcasted_iota(I32, (TQ, KB), 1)

    def score_body(kb, carry):
        raw = jnp.maximum(jnp.dot(qi_st, kit_ref[0, kb], preferred_element_type=F32), 0.0)
        isc = wi[0] * raw[0:TQ]
        for h in range(1, H_IDX):
            isc = isc + wi[h] * raw[h * TQ:(h + 1) * TQ]
        bits = lax.bitcast_convert_type(isc, I32)
        key = bits ^ ((bits >> 31) & 0x7FFFFFFF)
        key = jnp.where(key == -1, 0, key)
        idx = kb * KB + lane_idx
        key_sc[kb] = jnp.where((idx >= adm_lo) & (idx < end), key, INT_MIN)
        return carry

    lax.fori_loop(0, nkb, score_body, 0)

    def count(pred_fn):
        def body(kb, acc):
            hit = jnp.where(pred_fn(kb), 1, 0)
            part = hit[:, 0:LANES]
            for j in range(1, KB // LANES):
                part = part + hit[:, j * LANES:(j + 1) * LANES]
            return acc + part
        acc = lax.fori_loop(0, nkb, body, jnp.zeros((TQ, LANES), I32))
        return jnp.sum(acc, axis=1, keepdims=True)

    def thr_body(i, p):
        cand = p + lax.shift_left(jnp.int32(1), 31 - i)
        cnt = count(lambda kb: key_sc[kb] >= cand)
        return jnp.where(cnt >= n_sel, cand, p)

    thr = lax.fori_loop(0, 32, thr_body, jnp.full((TQ, 1), INT_MIN, I32))
    need = n_sel - count(lambda kb: key_sc[kb] > thr)

    big = jnp.int32(1 << idx_bits)

    def eq_body(kb, carry):
        key = key_sc[kb]
        eq_sc[kb] = jnp.where((key == thr) & (key != INT_MIN), kb * KB + lane_idx, big)
        return carry

    lax.fori_loop(0, nkb, eq_body, 0)

    def cut_body(i, qc):
        cand = qc + lax.shift_left(jnp.int32(1), idx_bits - 1 - i)
        cnt = count(lambda kb: eq_sc[kb] < cand)
        return jnp.where(cnt < need, cand, qc)

    qcut = lax.fori_loop(0, idx_bits, cut_body, jnp.zeros((TQ, 1), I32))

    q_st = [jnp.concatenate([u[:, (n * G + g) * DH_C:(n * G + g + 1) * DH_C] for g in range(G)], axis=0).astype(BF16)
            for n in range(HKV_C)]
    acc_sc[...] = jnp.zeros_like(acc_sc)
    m_sc[...] = jnp.full_like(m_sc, NEG_BIG)
    l_sc[...] = jnp.zeros_like(l_sc)

    def att_body(kb, carry):
        sel = (key_sc[kb] > thr) | (eq_sc[kb] <= qcut)
        sel_st = jnp.concatenate([sel] * G, axis=0)
        kblk = k_ref[0, kb]
        vblk = v_ref[0, kb]
        for n in range(HKV_C):
            kn = kblk[:, n * DH_C:(n + 1) * DH_C]
            vn = vblk[:, n * DH_C:(n + 1) * DH_C]
            logit = lax.dot_general(q_st[n], kn, (((1,), (1,)), ((), ())), preferred_element_type=F32) * (DH_C ** -0.5)
            logit = jnp.where(sel_st, logit, NEG_BIG)
            m_old = m_sc[n]
            m_new = jnp.maximum(m_old, jnp.max(logit, axis=1, keepdims=True))
            p = jnp.where(sel_st, jnp.exp(logit - m_new), 0.0)
            alpha = jnp.exp(m_old - m_new)
            l_sc[n] = alpha * l_sc[n] + jnp.sum(p, axis=1, keepdims=True)
            acc_sc[n] = alpha * acc_sc[n] + jnp.dot(p.astype(BF16), vn, preferred_element_type=F32)
            m_sc[n] = m_new
        return carry

    lax.fori_loop(0, nkb, att_body, 0)
    outs = []
    for n in range(HKV_C):
        o = acc_sc[n] / l_sc[n]
        outs += [o[g * TQ:(g + 1) * TQ] for g in range(G)]
    y_ref[...] = jnp.concatenate(outs, axis=1)


def _dsa(uc, kb_arr, vb_arr, kit_arr, *, nb, nq, TQ, n_sel, adm_lo, adm_cap, end_base, end_step):
    rows = uc.shape[0]
    nkb_max, KB = kb_arr.shape[1], kb_arr.shape[2]
    idx_bits = max(1, (nkb_max * KB - 1).bit_length())
    G = H_C // HKV_C
    per_b = lambda b, c: (b, 0, 0, 0)
    return pl.pallas_call(
        functools.partial(_dsa_kernel, TQ=TQ, KB=KB, n_sel=n_sel, adm_lo=adm_lo, adm_cap=adm_cap,
                          end_base=end_base, end_step=end_step, idx_bits=idx_bits),
        grid=(nb, nq),
        in_specs=[pl.BlockSpec((TQ, _GROUP_WIDTHS[2]), lambda b, c: (b * nq + c, 0)),
                  pl.BlockSpec((1, nkb_max, KB, HKV_C * DH_C), per_b),
                  pl.BlockSpec((1, nkb_max, KB, HKV_C * DH_C), per_b),
                  pl.BlockSpec((1, nkb_max, D_IDX, KB), per_b)],
        out_specs=pl.BlockSpec((TQ, W_C), lambda b, c: (b * nq + c, 0)),
        out_shape=jax.ShapeDtypeStruct((rows, W_C), F32),
        scratch_shapes=[pltpu.VMEM((nkb_max, TQ, KB), I32), pltpu.VMEM((nkb_max, TQ, KB), I32),
                        pltpu.VMEM((HKV_C, G * TQ, DH_C), F32), pltpu.VMEM((HKV_C, G * TQ, 1), F32),
                        pltpu.VMEM((HKV_C, G * TQ, 1), F32)],
        compiler_params=pltpu.CompilerParams(dimension_semantics=("parallel", "arbitrary"), vmem_limit_bytes=VMEM_LIMIT),
        name="dsa_attend",
    )(uc, kb_arr, vb_arr, kit_arr)


def _outproj_kernel(ya_ref, yb_ref, yc_ref, h_ref, w_ref, g_ref, b_ref, o_ref, *, alpha):
    mix = (jnp.dot(ya_ref[...].astype(BF16), w_ref[0:W_A, :], preferred_element_type=F32)
           + jnp.dot(yb_ref[...].astype(BF16), w_ref[W_A:W_A + W_B, :], preferred_element_type=F32)
           + jnp.dot(yc_ref[...].astype(BF16), w_ref[W_A + W_B:, :], preferred_element_type=F32))
    o_ref[...] = _ln(alpha * h_ref[...] + mix, g_ref[...], b_ref[...])


def _outproj(ya, yb, yc, h, w_out_bf, g, b, alpha):
    n, d = h.shape
    tm = _pick_tile(n, 768)
    row = lambda i: (i, 0)
    fix = lambda i: (0, 0)
    return pl.pallas_call(
        functools.partial(_outproj_kernel, alpha=alpha),
        grid=(n // tm,),
        in_specs=[pl.BlockSpec((tm, W_A), row), pl.BlockSpec((tm, W_B), row), pl.BlockSpec((tm, W_C), row),
                  pl.BlockSpec((tm, d), row), pl.BlockSpec(w_out_bf.shape, fix),
                  pl.BlockSpec((1, d), fix), pl.BlockSpec((1, d), fix)],
        out_specs=pl.BlockSpec((tm, d), row),
        out_shape=jax.ShapeDtypeStruct((n, d), F32),
        compiler_params=pltpu.CompilerParams(dimension_semantics=("parallel",), vmem_limit_bytes=VMEM_LIMIT),
        name="outproj_ln1",
    )(ya, yb, yc, h, w_out_bf, g.reshape(1, d), b.reshape(1, d))


def _moe_kernel(h_ref, wr_ref, br_ref, wg_ref, wu_ref, wd_ref, g_ref, b_ref, o_ref, acc_sc, gate_sc, *, alpha):
    e = pl.program_id(1)
    x = h_ref[...]
    tm = x.shape[0]
    xb = x.astype(BF16)
    eg = N_EXPERTS // N_GROUPS

    @pl.when(e == 0)
    def _():
        s = _sigmoid(lax.dot_general(wr_ref[...], xb, (((1,), (1,)), ((), ())), preferred_element_type=F32))
        sb = s + br_ref[...]
        srow = [s[j:j + 1, :] for j in range(N_EXPERTS)]
        brow = [sb[j:j + 1, :] for j in range(N_EXPERTS)]
        gscore = []
        for g in range(N_GROUPS):
            best = None
            for i in range(eg):
                for j in range(i + 1, eg):
                    pair = brow[g * eg + i] + brow[g * eg + j]
                    best = pair if best is None else jnp.maximum(best, pair)
            gscore.append(best)
        gmax = functools.reduce(jnp.maximum, gscore)
        taken = jnp.zeros_like(gmax) > 1.0
        sel = [None] * N_EXPERTS
        for g in range(N_GROUPS):
            is_g = jnp.logical_and(jnp.logical_not(taken), gscore[g] == gmax)
            taken = jnp.logical_or(taken, is_g)
            for i in range(eg):
                a = brow[g * eg + i]
                rank = jnp.zeros(a.shape, I32)
                for j in range(eg):
                    if j == i:
                        continue
                    o = brow[g * eg + j]
                    ahead = (o >= a) if j < i else (o > a)
                    rank = rank + jnp.where(ahead, 1, 0)
                sel[g * eg + i] = jnp.logical_and(is_g, rank < TOP_K)
        denom = jnp.zeros_like(gmax)
        for j in range(N_EXPERTS):
            denom = denom + jnp.where(sel[j], srow[j], 0.0)
        gates = [jnp.where(sel[j], srow[j] / denom, 0.0) for j in range(N_EXPERTS)]
        gt = jnp.concatenate(gates + [jnp.zeros((LANES - N_EXPERTS, tm), F32)], axis=0)
        gate_sc[...] = gt.T
        acc_sc[...] = jnp.zeros_like(acc_sc)

    hg = jnp.dot(xb, wg_ref[0].astype(BF16), preferred_element_type=F32)
    hu = jnp.dot(xb, wu_ref[0].astype(BF16), preferred_element_type=F32)
    he = (_silu(hg) * hu).astype(BF16)
    ye = jnp.dot(he, wd_ref[0].astype(BF16), preferred_element_type=F32)
    lane = lax.broadcasted_iota(I32, (1, LANES), 1)
    gcol = jnp.sum(jnp.where(lane == e, gate_sc[...], 0.0), axis=1, keepdims=True)
    acc_sc[...] += gcol * ye

    @pl.when(e == N_EXPERTS - 1)
    def _():
        o_ref[...] = _ln(alpha * x + acc_sc[...], g_ref[...], b_ref[...])


def _moe(h, wr_t, br_col, w_gate, w_up, w_down, g, b, alpha):
    n, d = h.shape
    tm = _pick_tile(n, 768)
    if tm % LANES:
        tm = n
    row = lambda i, e: (i, 0)
    fix = lambda i, e: (0, 0)
    per_e = lambda i, e: (e, 0, 0)
    return pl.pallas_call(
        functools.partial(_moe_kernel, alpha=alpha),
        grid=(n // tm, N_EXPERTS),
        in_specs=[pl.BlockSpec((tm, d), row), pl.BlockSpec((LANES, d), fix), pl.BlockSpec((LANES, 1), fix),
                  pl.BlockSpec((1, d, D_EXPERT), per_e), pl.BlockSpec((1, d, D_EXPERT), per_e),
                  pl.BlockSpec((1, D_EXPERT, d), per_e), pl.BlockSpec((1, d), fix), pl.BlockSpec((1, d), fix)],
        out_specs=pl.BlockSpec((tm, d), row),
        out_shape=jax.ShapeDtypeStruct((n, d), F32),
        scratch_shapes=[pltpu.VMEM((tm, d), F32), pltpu.VMEM((tm, LANES), F32)],
        compiler_params=pltpu.CompilerParams(dimension_semantics=("parallel", "arbitrary"), vmem_limit_bytes=VMEM_LIMIT),
        name="moe_ln2",
    )(h, wr_t, br_col, w_gate, w_up, w_down, g.reshape(1, d), b.reshape(1, d))


def _pack_in_weights(w_in_l, b_in_l):
    offs = {}
    off = 0
    for name, n in _SPLITS:
        offs[name] = (off, n)
        off += n
    wcols, bcols = [], []
    for grp in (_GROUP_A, _GROUP_B, _GROUP_C):
        for name, n in grp:
            if name is None:
                wcols.append(jnp.zeros((w_in_l.shape[0], n), w_in_l.dtype))
                bcols.append(jnp.zeros((n,), b_in_l.dtype))
            else:
                o, m = offs[name]
                assert m == n
                wcols.append(w_in_l[:, o:o + n])
                bcols.append(b_in_l[o:o + n])
    return jnp.concatenate(wcols, axis=1).astype(BF16), jnp.concatenate(bcols).reshape(1, -1).astype(F32)


def _block_diag_embed(x):
    B, H, r, c = x.shape
    eye = jnp.eye(H, dtype=x.dtype)
    return (x[:, :, :, None, :] * eye[None, :, None, :, None]).reshape(B, H * r, H * c)


def _block_diag_extract(x, H):
    B, R, C = x.shape
    r, c = R // H, C // H
    x5 = x.reshape(B, H, r, H, c)
    return jnp.stack([x5[:, h, :, h, :] for h in range(H)], axis=1)


def _mixer_states_in(s_gla, s_c, s_n, s_m, s_conv):
    nb = s_gla.shape[0]
    s0t = _block_diag_embed(jnp.swapaxes(s_gla.astype(F32), -1, -2))
    c0 = _block_diag_embed(jnp.swapaxes(s_c.astype(F32), -1, -2))
    n0 = s_n.astype(F32).reshape(nb, 1, W_B)
    m0 = jnp.pad(s_m.astype(F32), ((0, 0), (0, LANES - H_B))).reshape(nb, 1, LANES)
    conv0 = jnp.pad(s_conv.astype(F32), ((0, 0), (SUBLANES - (CONV_W - 1), 0), (0, 0)))
    return s0t, c0, n0, m0, conv0


def _layer(h, stream, l, p):
    (w_in_p, b_in_p, wg, bg, gn_a, cw, cb, bf, gn_b, w_out_bf, ln1_g, ln1_b, wr_t, br_col,
     w_gate, w_up, w_down, ln2_g, ln2_b, ln_in_g, ln_in_b, alpha) = p
    h, ua, ub, uc = _inproj(h, ln_in_g, ln_in_b, w_in_p, b_in_p, apply_ln=(l == 0))
    nb, nc, L, lo, hi = stream['nb'], stream['nc'], stream['L'], stream['lo'], stream['hi']
    s0t, c0, n0, m0, conv0 = stream['states'](l)
    ya, st_out = _gla(ua, wg, bg, gn_a, s0t, nb=nb, nc=nc, L=L, lo=lo, hi=hi)
    yb, c_out, n_out, m_out = _mlstm(ub, cw, cb, bf, gn_b, c0, n0, m0, conv0, nb=nb, nc=nc, L=L, lo=lo, hi=hi)
    kb_arr, vb_arr, kit_arr = stream['keys'](l, uc)
    yc = _dsa(uc, kb_arr, vb_arr, kit_arr, nb=nb, nq=nc, TQ=L, **stream['dsa'])
    h1 = _outproj(ya, yb, yc, h, w_out_bf, ln1_g, ln1_b, alpha)
    h2 = _moe(h1, wr_t, br_col, w_gate, w_up, w_down, ln2_g, ln2_b, alpha)
    S = jnp.swapaxes(_block_diag_extract(st_out, H_A), -1, -2)
    C = jnp.swapaxes(_block_diag_extract(c_out, H_B), -1, -2)
    return h2, (ub, uc, S, C, n_out.reshape(nb, H_B, DH_B), m_out[:, 0, :H_B])


def kernel(x_prompt, x_sample, cache_k, cache_v, cache_kidx, state_gla, state_mlstm_c, state_mlstm_n, state_mlstm_m, state_conv, meta_tokens, ln_in_g, ln_in_b, w_in, b_in, w_gla_gate, b_gla_gate, g_gla_norm, conv_w, conv_b, b_forget, g_mlstm_norm, w_out, ln1_g, ln1_b, w_router, b_router, w_gate, w_up, w_down, ln2_g, ln2_b):
    depth = w_in.shape[0]
    alpha = float((2 * depth) ** 0.25)
    B, S_p, D = x_prompt.shape
    DB, Ts, _ = x_sample.shape
    P = cache_k.shape[2]
    assert S_p % CHUNK == 0 and CHUNK % Ts == 0 and P % CHUNK == 0 and Ts >= CONV_W - 1
    KB = KEY_BLOCK

    wr_t = jnp.pad(w_router.T, ((0, LANES - N_EXPERTS), (0, 0))).astype(BF16)
    br_col = jnp.pad(b_router.astype(F32), (0, LANES - N_EXPERTS)).reshape(LANES, 1)

    def layer_params(l):
        w_in_p, b_in_p = _pack_in_weights(w_in[l], b_in[l])
        wg = jnp.pad(w_gla_gate[l], ((0, LANES - GATE_RANK), (0, 0))).astype(BF16)
        bf = jnp.pad(b_forget[l].astype(F32), (H_B, LANES - 2 * H_B)).reshape(1, LANES)
        return (w_in_p, b_in_p, wg, b_gla_gate[l].reshape(1, -1), g_gla_norm[l].reshape(1, -1),
                conv_w[l], conv_b[l].reshape(1, -1), bf, g_mlstm_norm[l].reshape(1, -1), w_out[l].astype(BF16),
                ln1_g[l], ln1_b[l], wr_t, br_col, w_gate[l], w_up[l], w_down[l], ln2_g[l], ln2_b[l],
                ln_in_g, ln_in_b, alpha)

    params = [layer_params(l) for l in range(depth)]

    front = CHUNK - N_META
    t_real = N_META + S_p
    t_pad = -(-(front + t_real) // KB) * KB
    hp = jnp.concatenate([jnp.zeros((B, front, D), x_prompt.dtype),
                          jnp.broadcast_to(meta_tokens.astype(x_prompt.dtype)[None], (B, N_META, D)), x_prompt,
                          jnp.zeros((B, t_pad - front - t_real, D), x_prompt.dtype)], axis=1).reshape(B * t_pad, D)

    def prompt_keys(l, uc):
        u3 = uc.reshape(B, t_pad, -1)
        kk = u3[:, :, 512:640].astype(BF16).reshape(B, t_pad // KB, KB, HKV_C * DH_C)
        vv = u3[:, :, 640:768].astype(BF16).reshape(B, t_pad // KB, KB, HKV_C * DH_C)
        kit = jnp.swapaxes(u3[:, :, 896:896 + D_IDX].astype(BF16).reshape(B, t_pad // KB, KB, D_IDX), -1, -2)
        return kk, vv, kit

    zero_states = _mixer_states_in(jnp.zeros((B, H_A, DK_A, DV_A), F32), jnp.zeros((B, H_B, DH_B, DH_B), F32),
                                   jnp.zeros((B, H_B, DH_B), F32), jnp.zeros((B, H_B), F32),
                                   jnp.zeros((B, CONV_W - 1, 2 * W_B), F32))
    prompt_stream = dict(nb=B, nc=t_pad // CHUNK, L=CHUNK, lo=front, hi=front + t_real,
                         states=lambda l: zero_states, keys=prompt_keys,
                         dsa=dict(n_sel=min(TOPK_KEYS, S_p // 4), adm_lo=front, adm_cap=front + t_real,
                                  end_base=CHUNK, end_step=CHUNK))
    p_out = [[] for _ in range(8)]
    for l in range(depth):
        hp, (ub, uc, S, C, n_o, m_o) = _layer(hp, prompt_stream, l, params[l])
        ub3 = ub.reshape(B, t_pad, -1)[:, front:front + t_real]
        uc3 = uc.reshape(B, t_pad, -1)[:, front:front + t_real]
        vals = (uc3[:, :, 512:640].reshape(B, t_real, HKV_C, DH_C), uc3[:, :, 640:768].reshape(B, t_real, HKV_C, DH_C),
                uc3[:, :, 896:896 + D_IDX], S, C, n_o, m_o, ub3[:, t_real - (CONV_W - 1):, 0:2 * W_B])
        for acc, a in zip(p_out, vals):
            acc.append(a)
    y_prompt = hp.reshape(B, t_pad, D)[:, front + N_META:front + t_real]

    s_tot = P + Ts
    s_pad = -(-s_tot // KB) * KB

    def sample_keys(l, uc):
        u3 = uc.reshape(DB, Ts, -1)

        def cat(cache, new):
            a = jnp.concatenate([cache.astype(F32).reshape(DB, P, -1), new], axis=1)
            return jnp.pad(a, ((0, 0), (0, s_pad - s_tot), (0, 0))).astype(BF16).reshape(DB, s_pad // KB, KB, -1)

        kk = cat(cache_k[l], u3[:, :, 512:640])
        vv = cat(cache_v[l], u3[:, :, 640:768])
        kit = jnp.swapaxes(cat(cache_kidx[l], u3[:, :, 896:896 + D_IDX]), -1, -2)
        return kk, vv, kit

    sample_stream = dict(nb=DB, nc=1, L=Ts, lo=0, hi=Ts,
                         states=lambda l: _mixer_states_in(state_gla[l], state_mlstm_c[l], state_mlstm_n[l],
                                                           state_mlstm_m[l], state_conv[l]),
                         keys=sample_keys,
                         dsa=dict(n_sel=min(TOPK_KEYS, s_tot // 4), adm_lo=0, adm_cap=s_tot, end_base=s_tot, end_step=0))
    hs = x_sample.reshape(DB * Ts, D)
    s_out = [[] for _ in range(8)]
    for l in range(depth):
        hs, (ub, uc, S, C, n_o, m_o) = _layer(hs, sample_stream, l, params[l])
        ub3 = ub.reshape(DB, Ts, -1)
        uc3 = uc.reshape(DB, Ts, -1)
        vals = (uc3[:, :, 512:640].reshape(DB, Ts, HKV_C, DH_C), uc3[:, :, 640:768].reshape(DB, Ts, HKV_C, DH_C),
                uc3[:, :, 896:896 + D_IDX], S, C, n_o, m_o, ub3[:, Ts - (CONV_W - 1):, 0:2 * W_B])
        for acc, a in zip(s_out, vals):
            acc.append(a)
    y_sample = hs.reshape(DB, Ts, D)

    return (y_prompt, y_sample, *[jnp.stack(a) for a in p_out], *[jnp.stack(a) for a in s_out])
```

```python
import functools

import jax
import jax.numpy as jnp
from jax import lax
from jax.experimental import pallas as pl
from jax.experimental.pallas import tpu as pltpu

F32 = jnp.float32
BF16 = jnp.bfloat16
I32 = jnp.int32

CHUNK = 64
N_META = 16
H_A, DK_A, DV_A = 4, 32, 64
GATE_RANK, GATE_TAU = 16, 16.0
H_B, DH_B, CONV_W = 4, 64, 4
H_C, HKV_C, DH_C = 8, 2, 64
H_IDX, D_IDX = 4, 32
TOPK_KEYS = 256
N_EXPERTS, N_GROUPS, TOP_K, D_EXPERT = 16, 4, 2, 256
W_A, W_B, W_C = H_A * DV_A, H_B * DH_B, H_C * DH_C
LN_EPS = 1e-5

LANES = 128
SUBLANES = 8
VMEM_LIMIT = 56 * 1024 * 1024
KEY_BLOCK = 256
NEG_BIG = -1e30
M_INIT = -1e29
INT_MIN = -2 ** 31
KEY_NEG_INF = INT_MIN + 0x7FFFFF

_SPLITS = (
    ('a_q', H_A * DK_A), ('a_k', H_A * DK_A), ('a_v', W_A), ('a_g', GATE_RANK), ('a_r', W_A),
    ('b_qk', 2 * W_B), ('b_v', W_B), ('b_i', H_B), ('b_f', H_B), ('b_o', W_B),
    ('c_q', W_C), ('c_k', HKV_C * DH_C), ('c_v', HKV_C * DH_C),
    ('c_qi', H_IDX * D_IDX), ('c_w', H_IDX), ('c_ki', D_IDX),
)
_GROUP_A = (('a_q', 128), ('a_k', 128), ('a_v', 256), ('a_r', 256), ('a_g', 16), (None, 112))
_GROUP_B = (('b_qk', 512), ('b_v', 256), ('b_o', 256), ('b_i', 4), ('b_f', 4), (None, 120))
_GROUP_C = (('c_q', 512), ('c_k', 128), ('c_v', 128), ('c_qi', 128), ('c_ki', 32), ('c_w', 4), (None, 92))
_GROUP_WIDTHS = tuple(sum(n for _, n in g) for g in (_GROUP_A, _GROUP_B, _GROUP_C))

_NN = (((1,), (0,)), ((), ()))
_NT = (((1,), (1,)), ((), ()))
_TN = (((0,), (0,)), ((), ()))


def _mm(a, b, precise, dims=_NN):
    if precise:
        return lax.dot_general(a.astype(F32), b.astype(F32), dims, precision=lax.Precision.HIGHEST,
                               preferred_element_type=F32)
    return lax.dot_general(a.astype(BF16), b.astype(BF16), dims, preferred_element_type=F32)


def _log_sigmoid(x):
    return jnp.minimum(x, 0.0) - jnp.log(1.0 + jnp.exp(-jnp.abs(x)))


def _sigmoid(x):
    return 1.0 / (1.0 + jnp.exp(-x))


def _silu(x):
    return x * _sigmoid(x)


def _ln(x, g, b):
    mu = jnp.mean(x, axis=-1, keepdims=True)
    xc = x - mu
    var = jnp.mean(xc * xc, axis=-1, keepdims=True)
    return xc * lax.rsqrt(var + LN_EPS) * g + b


def _pick_tile(n, cap):
    best = None
    for t in range(SUBLANES, min(n, cap) + 1, SUBLANES):
        if n % t == 0:
            best = t
    assert best is not None, n
    return best


def _stack_heads(x, n_heads, width):
    lane_head = lax.broadcasted_iota(I32, (1, n_heads * width), 1) // width
    return jnp.concatenate([jnp.where(lane_head == h, x, 0.0) for h in range(n_heads)], axis=0)


def _head_norm_stacked(o_st, mask_st, n_heads, L, width):
    o_st = jnp.where(mask_st, o_st, 0.0)
    mu = jnp.sum(o_st, axis=-1, keepdims=True) * (1.0 / width)
    xc = jnp.where(mask_st, o_st - mu, 0.0)
    var = jnp.sum(xc * xc, axis=-1, keepdims=True) * (1.0 / width)
    yn = xc * lax.rsqrt(var + LN_EPS)
    y = yn[0:L]
    for h in range(1, n_heads):
        y = y + yn[h * L:(h + 1) * L]
    return y


def _params(sem):
    return pltpu.CompilerParams(dimension_semantics=sem, vmem_limit_bytes=VMEM_LIMIT)


def _inproj_kernel(x_ref, g_ref, b_ref, w_ref, bias_ref, *out_refs, apply_ln, precise):
    x = x_ref[...]
    if apply_ln:
        h_ref, *u_refs = out_refs
        x = _ln(x, g_ref[...], b_ref[...])
        h_ref[...] = x
    else:
        u_refs = out_refs
    xo = x if precise else x.astype(BF16)
    off = 0
    for r, n in zip(u_refs, _GROUP_WIDTHS):
        r[...] = _mm(xo, w_ref[:, off:off + n], precise) + bias_ref[:, off:off + n]
        off += n


def _inproj(x, ln_g, ln_b, w_packed, b_packed, apply_ln, precise):
    n, d = x.shape
    tm = _pick_tile(n, 384)
    dp = w_packed.shape[1]
    row = lambda i: (i, 0)
    fix = lambda i: (0, 0)
    out_shape = [jax.ShapeDtypeStruct((n, w), F32) for w in _GROUP_WIDTHS]
    out_specs = [pl.BlockSpec((tm, w), row) for w in _GROUP_WIDTHS]
    if apply_ln:
        out_shape = [jax.ShapeDtypeStruct((n, d), F32)] + out_shape
        out_specs = [pl.BlockSpec((tm, d), row)] + out_specs
    outs = pl.pallas_call(
        functools.partial(_inproj_kernel, apply_ln=apply_ln, precise=precise),
        grid=(n // tm,),
        in_specs=[pl.BlockSpec((tm, d), row), pl.BlockSpec((1, d), fix), pl.BlockSpec((1, d), fix),
                  pl.BlockSpec((d, dp), fix), pl.BlockSpec((1, dp), fix)],
        out_specs=out_specs, out_shape=out_shape,
        compiler_params=_params(("parallel",)),
        name="inproj_ln" if apply_ln else "inproj",
    )(x, ln_g.reshape(1, d), ln_b.reshape(1, d), w_packed, b_packed)
    if apply_ln:
        return outs[0], outs[1], outs[2], outs[3]
    return x, outs[0], outs[1], outs[2]


def _gla_kernel(u_ref, wg_ref, bg_ref, gn_ref, s0_ref, y_ref, sout_ref, st_sc, *, L, lo, hi, precise):
    c = pl.program_id(1)

    @pl.when(c == 0)
    def _():
        st_sc[...] = s0_ref[0]

    u = u_ref[...]
    q = u[:, 0:128] * (DK_A ** -0.5)
    k = u[:, 128:256]
    v = u[:, 256:512]
    r = u[:, 512:768]
    g = u[:, 768:896]
    la = _log_sigmoid(_mm(g, wg_ref[...], precise) + bg_ref[...]) * (1.0 / GATE_TAU)
    pos = c * L + lax.broadcasted_iota(I32, (L, 1), 0)
    real = (pos >= lo) & (pos < hi)
    la = jnp.where(real, la, 0.0)
    k = jnp.where(real, k, 0.0)
    ti = lax.broadcasted_iota(I32, (L, L), 0)
    si = lax.broadcasted_iota(I32, (L, L), 1)
    b = _mm((ti >= si).astype(F32), la, True)
    bmid = b[L // 2 - 1:L // 2, :]
    bl = b[L - 1:L, :]
    qs = _stack_heads(q * jnp.exp(b - bmid), H_A, DK_A)
    qbs = _stack_heads(q * jnp.exp(b), H_A, DK_A)
    kt = k * jnp.exp(bmid - b)
    kh = k * jnp.exp(bl - b)
    att = _mm(qs, kt, precise, _NT)
    tr = lax.broadcasted_iota(I32, (H_A * L, L), 0) % L
    sc = lax.broadcasted_iota(I32, (H_A * L, L), 1)
    att = jnp.where(sc <= tr, att, 0.0)
    row_head = lax.broadcasted_iota(I32, (H_A * L, W_A), 0) // L
    lane_head = lax.broadcasted_iota(I32, (H_A * L, W_A), 1) // DV_A
    mask_st = row_head == lane_head
    st = st_sc[...]
    o_st = jnp.where(mask_st, _mm(att, v, precise), 0.0) + _mm(qbs, st, precise, _NT)
    y = _head_norm_stacked(o_st, mask_st, H_A, L, DV_A)
    y_ref[...] = y * gn_ref[...] * _silu(r)
    bd = (lax.broadcasted_iota(I32, (W_A, H_A * DK_A), 0) // DV_A) == (lax.broadcasted_iota(I32, (W_A, H_A * DK_A), 1) // DK_A)
    st_new = st * jnp.exp(bl) + jnp.where(bd, _mm(v, kh, precise, _TN), 0.0)
    st_sc[...] = st_new
    sout_ref[0] = st_new


def _gla(ua, wg, bg, gn, s0t, *, nb, nc, L, lo, hi, precise):
    rows = ua.shape[0]
    return pl.pallas_call(
        functools.partial(_gla_kernel, L=L, lo=lo, hi=hi, precise=precise),
        grid=(nb, nc),
        in_specs=[pl.BlockSpec((L, _GROUP_WIDTHS[0]), lambda b, c: (b * nc + c, 0)),
                  pl.BlockSpec((LANES, LANES), lambda b, c: (0, 0)),
                  pl.BlockSpec((1, LANES), lambda b, c: (0, 0)),
                  pl.BlockSpec((1, W_A), lambda b, c: (0, 0)),
                  pl.BlockSpec((1, W_A, H_A * DK_A), lambda b, c: (b, 0, 0))],
        out_specs=[pl.BlockSpec((L, W_A), lambda b, c: (b * nc + c, 0)),
                   pl.BlockSpec((1, W_A, H_A * DK_A), lambda b, c: (b, 0, 0))],
        out_shape=[jax.ShapeDtypeStruct((rows, W_A), F32), jax.ShapeDtypeStruct((nb, W_A, H_A * DK_A), F32)],
        scratch_shapes=[pltpu.VMEM((W_A, H_A * DK_A), F32)],
        compiler_params=_params(("parallel", "arbitrary")),
        name="gla_scan",
    )(ua, wg, bg, gn, s0t)


def _mlstm_kernel(u_ref, cw_ref, cb_ref, bf_ref, gn_ref, c0_ref, n0_ref, m0_ref, conv0_ref,
                  y_ref, cout_ref, nout_ref, mout_ref, c_sc, n_sc, m_sc, buf_sc, *, L, lo, hi, precise):
    c = pl.program_id(1)
    H, D = H_B, DH_B

    @pl.when(c == 0)
    def _():
        c_sc[...] = c0_ref[0]
        n_sc[...] = n0_ref[0]
        m_sc[...] = m0_ref[0]
        buf_sc[0:SUBLANES, :] = conv0_ref[0]

    u = u_ref[...]
    pos = c * L + lax.broadcasted_iota(I32, (L, 1), 0)
    real = (pos >= lo) & (pos < hi)
    buf_sc[SUBLANES:SUBLANES + L, :] = jnp.where(real, u[:, 0:2 * W_B], 0.0)
    hist = SUBLANES - (CONV_W - 1)
    conv = buf_sc[hist:hist + L, :] * cw_ref[0:1, :]
    for j in range(1, CONV_W):
        conv = conv + buf_sc[hist + j:hist + j + L, :] * cw_ref[j:j + 1, :]
    conv = conv + cb_ref[...]
    buf_sc[0:SUBLANES, :] = buf_sc[L:L + SUBLANES, :]
    act = _silu(conv)
    q = act[:, 0:W_B]
    k = act[:, W_B:2 * W_B] * (D ** -0.5)
    v = u[:, 512:768]
    og = u[:, 768:1024]
    gts = u[:, 1024:1152]
    ig_col = jnp.where(real, gts, NEG_BIG)
    lf_col = jnp.where(real, _log_sigmoid(gts + bf_ref[...]), 0.0)
    ti = lax.broadcasted_iota(I32, (L, L), 0)
    si = lax.broadcasted_iota(I32, (L, L), 1)
    b_col = _mm((ti >= si).astype(F32), lf_col, True)
    lane = lax.broadcasted_iota(I32, (1, LANES), 1)
    z = jnp.where(lane < H, ig_col, b_col)
    zt = jnp.concatenate([z, jnp.zeros((LANES - L, LANES), F32)], axis=0).T
    b_st = jnp.concatenate([b_col[:, H + h:H + h + 1] for h in range(H)], axis=0)
    m0_st = jnp.concatenate([jnp.broadcast_to(m_sc[0:1, h:h + 1], (L, 1)) for h in range(H)], axis=0)
    rowv_st = jnp.concatenate(
        [jnp.broadcast_to(zt[h:h + 1, 0:L] - zt[H + h:H + h + 1, 0:L], (L, L)) for h in range(H)], axis=0)
    tr = lax.broadcasted_iota(I32, (H * L, L), 0) % L
    sc = lax.broadcasted_iota(I32, (H * L, L), 1)
    logw = jnp.where(sc <= tr, b_st + rowv_st, NEG_BIG)
    lp = b_st + m0_st
    m_st = jnp.maximum(lp, jnp.max(logw, axis=1, keepdims=True))
    w = jnp.exp(logw - m_st)
    sp = jnp.exp(lp - m_st)
    qs = _stack_heads(q, H, D)
    qk = _mm(qs, k, precise, _NT) * w
    row_head = lax.broadcasted_iota(I32, (H * L, W_B), 0) // L
    lane_head = lax.broadcasted_iota(I32, (H * L, W_B), 1) // D
    mask_st = row_head == lane_head
    cbd = c_sc[...]
    nrow = n_sc[...]
    num = jnp.where(mask_st, _mm(qk, v, precise), 0.0) + sp * _mm(qs, cbd, precise)
    den = jnp.sum(qk, axis=1, keepdims=True) + sp * jnp.sum(qs * nrow, axis=1, keepdims=True)
    hh = num / jnp.maximum(jnp.abs(den), jnp.exp(-m_st))
    y = _head_norm_stacked(hh, mask_st, H, L, D)
    y_ref[...] = y * gn_ref[...] * _sigmoid(og)
    lh = lax.broadcasted_iota(I32, (1, W_B), 1) // D
    wlk = jnp.zeros((L, W_B), F32)
    sl_lanes = jnp.zeros((1, W_B), F32)
    m_new = jnp.zeros((1, LANES), F32)
    for h in range(H):
        ml = m_st[h * L + L - 1:h * L + L, :]
        bl = b_st[h * L + L - 1:h * L + L, :]
        wl = jnp.exp(bl - b_col[:, H + h:H + h + 1] + ig_col[:, h:h + 1] - ml)
        sl = jnp.exp(bl + m_sc[0:1, h:h + 1] - ml)
        wlk = jnp.where(lh == h, wl, wlk)
        sl_lanes = jnp.where(lh == h, sl, sl_lanes)
        m_new = jnp.where(lane == h, ml, m_new)
    kw = k * wlk
    bd = (lax.broadcasted_iota(I32, (W_B, W_B), 0) // D) == (lax.broadcasted_iota(I32, (W_B, W_B), 1) // D)
    c_new = sl_lanes * cbd + jnp.where(bd, _mm(kw, v, precise, _TN), 0.0)
    n_new = sl_lanes * nrow + jnp.sum(kw, axis=0, keepdims=True)
    c_sc[...] = c_new
    n_sc[...] = n_new
    m_sc[...] = m_new
    cout_ref[0] = c_new
    nout_ref[0] = n_new
    mout_ref[0] = m_new


def _mlstm(ub, cw, cb, bf, gn, c0, n0, m0, conv0, *, nb, nc, L, lo, hi, precise):
    rows = ub.shape[0]
    fix2 = lambda b, c: (0, 0)
    per_b = lambda b, c: (b, 0, 0)
    return pl.pallas_call(
        functools.partial(_mlstm_kernel, L=L, lo=lo, hi=hi, precise=precise),
        grid=(nb, nc),
        in_specs=[pl.BlockSpec((L, _GROUP_WIDTHS[1]), lambda b, c: (b * nc + c, 0)),
                  pl.BlockSpec((CONV_W, 2 * W_B), fix2), pl.BlockSpec((1, 2 * W_B), fix2),
                  pl.BlockSpec((1, LANES), fix2), pl.BlockSpec((1, W_B), fix2),
                  pl.BlockSpec((1, W_B, W_B), per_b), pl.BlockSpec((1, 1, W_B), per_b),
                  pl.BlockSpec((1, 1, LANES), per_b), pl.BlockSpec((1, SUBLANES, 2 * W_B), per_b)],
        out_specs=[pl.BlockSpec((L, W_B), lambda b, c: (b * nc + c, 0)),
                   pl.BlockSpec((1, W_B, W_B), per_b), pl.BlockSpec((1, 1, W_B), per_b),
                   pl.BlockSpec((1, 1, LANES), per_b)],
        out_shape=[jax.ShapeDtypeStruct((rows, W_B), F32), jax.ShapeDtypeStruct((nb, W_B, W_B), F32),
                   jax.ShapeDtypeStruct((nb, 1, W_B), F32), jax.ShapeDtypeStruct((nb, 1, LANES), F32)],
        scratch_shapes=[pltpu.VMEM((W_B, W_B), F32), pltpu.VMEM((1, W_B), F32), pltpu.VMEM((1, LANES), F32),
                        pltpu.VMEM((L + SUBLANES, 2 * W_B), F32)],
        compiler_params=_params(("parallel", "arbitrary")),
        name="mlstm_scan",
    )(ub, cw, cb, bf, gn, c0, n0, m0, conv0)


def _dsa_kernel(u_ref, k_ref, vt_ref, ki_ref, y_ref, sc_sc, sa_sc, sb_sc, acc_sc, m_sc, l_sc, y_sc,
                *, KB, nkb_alloc, n_sel, adm_lo, adm_cap, end_base, end_step, end_half, out_rows, out_step, precise):
    b = pl.program_id(0)
    c = pl.program_id(1)
    TQ = LANES
    G = H_C // HKV_C
    lane_q = lax.broadcasted_iota(I32, (1, TQ), 1)
    end_lo = end_base + end_step * c
    end_q = jnp.minimum(jnp.where(lane_q < CHUNK, end_lo, end_lo + end_half), adm_cap)
    end_max = jnp.minimum(end_lo + end_half, adm_cap)
    nkb = (end_max + KB - 1) // KB
    npair = (end_max + 2 * KB - 1) // (2 * KB)
    u = u_ref[...]
    qi_t = (u[:, 768:896] * (D_IDX ** -0.5)).T
    kw_t = u[:, 896:1024].T
    qi_st = jnp.concatenate([qi_t[h * D_IDX:(h + 1) * D_IDX, :] for h in range(H_IDX)], axis=1)
    if not precise:
        qi_st = qi_st.astype(BF16)
    wi = [kw_t[D_IDX + h:D_IDX + h + 1, :] * (H_IDX ** -0.5) for h in range(H_IDX)]
    key_idx = lax.broadcasted_iota(I32, (KB, TQ), 0)

    def score_block(kb):
        raw = jnp.maximum(_mm(ki_ref[0, kb], qi_st, precise), 0.0)
        isc = wi[0] * raw[:, 0:TQ]
        for h in range(1, H_IDX):
            isc = isc + wi[h] * raw[:, h * TQ:(h + 1) * TQ]
        idx = kb * KB + key_idx
        sc_sc[kb] = jnp.where((idx >= adm_lo) & (idx < end_q), isc, -jnp.inf)

    def score_body(i, carry):
        score_block(2 * i)
        score_block(2 * i + 1)
        return carry

    lax.fori_loop(0, npair, score_body, 0)

    def count(pred_fn):
        def body(i, acc):
            for kb in (2 * i, 2 * i + 1):
                hit = jnp.where(pred_fn(sc_sc[kb]), 1, 0)
                acc = acc + jnp.sum(hit.reshape(KB // SUBLANES, SUBLANES, TQ), axis=0)
            return acc
        acc = lax.fori_loop(0, npair, body, jnp.zeros((SUBLANES, TQ), I32))
        return jnp.sum(acc, axis=0, keepdims=True)

    def key_to_float(key):
        return lax.bitcast_convert_type(jnp.where(key >= 0, key, key ^ 0x7FFFFFFF), F32)

    def thr_body(i, p):
        cand = p + lax.shift_left(jnp.int32(1), 31 - i)
        cand_f = key_to_float(cand)
        cnt = count(lambda s: s >= cand_f)
        return jnp.where(cnt >= n_sel, cand, p)

    thr_key = lax.fori_loop(0, 32, thr_body, jnp.full((1, TQ), INT_MIN, I32))
    thr = key_to_float(jnp.maximum(thr_key, KEY_NEG_INF))
    need = (n_sel - count(lambda s: s > thr)).astype(F32)

    q_t = [(u[:, j * LANES:(j + 1) * LANES] * (DH_C ** -0.5)).T for j in range(W_C // LANES)]
    zero_half = jnp.zeros((DH_C, G * TQ), F32)
    qz = []
    for n in range(HKV_C):
        heads = [n * G + g for g in range(G)]
        qn = jnp.concatenate([q_t[hc // 2][(hc % 2) * DH_C:(hc % 2 + 1) * DH_C, :] for hc in heads], axis=1)
        qn = jnp.concatenate([qn, zero_half] if n == 0 else [zero_half, qn], axis=0)
        qz.append(qn if precise else qn.astype(BF16))
    tri = (lax.broadcasted_iota(I32, (KB, KB), 0) >= lax.broadcasted_iota(I32, (KB, KB), 1)).astype(BF16)

    def mask_body(i, tie_seen):
        for kb in (2 * i, 2 * i + 1):
            s_idx = sc_sc[kb]
            eq = (s_idx == thr) & (s_idx > -jnp.inf)
            rank = tie_seen + jnp.dot(tri, jnp.where(eq, 1.0, 0.0).astype(BF16), preferred_element_type=F32)
            sel = (s_idx > thr) | (eq & (rank <= need))
            sc_sc[kb] = jnp.where(sel, 0.0, NEG_BIG)
            tie_seen = rank[KB - 1:KB, :]
        return tie_seen

    lax.fori_loop(0, npair, mask_body, jnp.zeros((1, TQ), F32))

    acc_sc[...] = jnp.zeros_like(acc_sc)
    m_sc[...] = jnp.full_like(m_sc, M_INIT)
    l_sc[...] = jnp.zeros_like(l_sc)
    W2 = 2 * TQ
    units = [(n, slice(gp * W2, (gp + 1) * W2)) for n in range(HKV_C) for gp in range(G // 2)]

    def logits(kb, dst):
        kblk = k_ref[0, kb]
        for ui, (n, cols) in enumerate(units):
            dst[ui] = _mm(kblk, qz[n][:, cols], precise)

    def att_step(kb, cur, nxt):
        logits(jnp.minimum(kb + 1, nkb_alloc - 1), nxt)
        bias = sc_sc[kb]
        bias2 = jnp.concatenate([bias, bias], axis=1)
        vtblk = vt_ref[0, kb]
        for ui, (n, cols) in enumerate(units):
            s = cur[ui] + bias2
            m_old = m_sc[n, :, cols]
            m_new = jnp.maximum(m_old, jnp.max(s, axis=0, keepdims=True))
            p = jnp.exp(s - m_new)
            alpha = jnp.exp(m_old - m_new)
            l_sc[n, :, cols] = alpha * l_sc[n, :, cols] + jnp.sum(p, axis=0, keepdims=True)
            acc_sc[n, :, cols] = alpha * acc_sc[n, :, cols] + _mm(vtblk[n * DH_C:(n + 1) * DH_C, :], p, precise)
            m_sc[n, :, cols] = m_new

    logits(0, sa_sc)

    def att_body(i, carry):
        att_step(2 * i, sa_sc, sb_sc)
        att_step(2 * i + 1, sb_sc, sa_sc)
        return carry

    lax.fori_loop(0, npair, att_body, 0)
    o = [acc_sc[n] / l_sc[n] for n in range(HKV_C)]
    for j in range(W_C // LANES):
        n, g0 = (2 * j) // G, (2 * j) % G
        pair = jnp.concatenate([o[n][:, g0 * TQ:(g0 + 1) * TQ], o[n][:, (g0 + 1) * TQ:(g0 + 2) * TQ]], axis=0)
        y_sc[:, j * LANES:(j + 1) * LANES] = pair.T
    off = pl.multiple_of(b * out_step, SUBLANES)
    y_ref[...] = y_sc[pl.ds(off, out_rows), :]


def _dsa(uc, k_arr, vt_arr, ki_arr, *, nb, nq, n_sel, adm_lo, adm_cap, end_base, end_step, end_half,
         out_rows, out_step, precise):
    rows = uc.shape[0]
    nkb_max, KB = k_arr.shape[1], k_arr.shape[2]
    assert nkb_max % 2 == 0
    G = H_C // HKV_C
    per_b = lambda b, c: (b, 0, 0, 0)
    u_map = (lambda b, c: (b * nq + c, 0)) if out_step == 0 else (lambda b, c: (0, 0))
    return pl.pallas_call(
        functools.partial(_dsa_kernel, KB=KB, nkb_alloc=nkb_max, n_sel=n_sel, adm_lo=adm_lo, adm_cap=adm_cap, end_base=end_base,
                          end_step=end_step, end_half=end_half, out_rows=out_rows, out_step=out_step,
                          precise=precise),
        grid=(nb, nq),
        in_specs=[pl.BlockSpec((LANES, _GROUP_WIDTHS[2]), u_map),
                  pl.BlockSpec((1, nkb_max, KB, HKV_C * DH_C), per_b),
                  pl.BlockSpec((1, nkb_max, HKV_C * DH_C, KB), per_b),
                  pl.BlockSpec((1, nkb_max, KB, D_IDX), per_b)],
        out_specs=pl.BlockSpec((out_rows, W_C), lambda b, c: (b * nq + c, 0)),
        out_shape=jax.ShapeDtypeStruct((rows, W_C), F32),
        scratch_shapes=[pltpu.VMEM((nkb_max, KB, LANES), F32), pltpu.VMEM((HKV_C * (G // 2), KB, 2 * LANES), F32),
                        pltpu.VMEM((HKV_C * (G // 2), KB, 2 * LANES), F32),
                        pltpu.VMEM((HKV_C, DH_C, G * LANES), F32), pltpu.VMEM((HKV_C, 1, G * LANES), F32),
                        pltpu.VMEM((HKV_C, 1, G * LANES), F32), pltpu.VMEM((LANES, W_C), F32)],
        compiler_params=_params(("parallel", "arbitrary")),
        name="dsa_attend",
    )(uc, k_arr, vt_arr, ki_arr)


def _outproj_kernel(ya_ref, yb_ref, yc_ref, h_ref, w_ref, g_ref, b_ref, o_ref, *, alpha, precise):
    mix = (_mm(ya_ref[...], w_ref[0:W_A, :], precise) + _mm(yb_ref[...], w_ref[W_A:W_A + W_B, :], precise)
           + _mm(yc_ref[...], w_ref[W_A + W_B:, :], precise))
    o_ref[...] = _ln(alpha * h_ref[...] + mix, g_ref[...], b_ref[...])


def _outproj(ya, yb, yc, h, w_out_l, g, b, alpha, precise):
    n, d = h.shape
    tm = _pick_tile(n, 768)
    row = lambda i: (i, 0)
    fix = lambda i: (0, 0)
    return pl.pallas_call(
        functools.partial(_outproj_kernel, alpha=alpha, precise=precise),
        grid=(n // tm,),
        in_specs=[pl.BlockSpec((tm, W_A), row), pl.BlockSpec((tm, W_B), row), pl.BlockSpec((tm, W_C), row),
                  pl.BlockSpec((tm, d), row), pl.BlockSpec(w_out_l.shape, fix),
                  pl.BlockSpec((1, d), fix), pl.BlockSpec((1, d), fix)],
        out_specs=pl.BlockSpec((tm, d), row),
        out_shape=jax.ShapeDtypeStruct((n, d), F32),
        compiler_params=_params(("parallel",)),
        name="outproj_ln1",
    )(ya, yb, yc, h, w_out_l, g.reshape(1, d), b.reshape(1, d))


def _moe_kernel(h_ref, wr_ref, br_ref, wg_ref, wu_ref, wd_ref, g_ref, b_ref, o_ref, acc_sc, gate_sc, *, alpha, precise):
    e = pl.program_id(1)
    x = h_ref[...]
    tm = x.shape[0]
    xo = x if precise else x.astype(BF16)
    eg = N_EXPERTS // N_GROUPS

    @pl.when(e == 0)
    def _():
        s = _sigmoid(_mm(wr_ref[...], xo, precise, _NT))
        sb = s + br_ref[...]
        srow = [s[j:j + 1, :] for j in range(N_EXPERTS)]
        brow = [sb[j:j + 1, :] for j in range(N_EXPERTS)]
        gscore = []
        for g in range(N_GROUPS):
            best = None
            for i in range(eg):
                for j in range(i + 1, eg):
                    pair = brow[g * eg + i] + brow[g * eg + j]
                    best = pair if best is None else jnp.maximum(best, pair)
            gscore.append(best)
        gmax = functools.reduce(jnp.maximum, gscore)
        taken = jnp.zeros_like(gmax) > 1.0
        sel = [None] * N_EXPERTS
        for g in range(N_GROUPS):
            is_g = jnp.logical_and(jnp.logical_not(taken), gscore[g] == gmax)
            taken = jnp.logical_or(taken, is_g)
            for i in range(eg):
                a = brow[g * eg + i]
                rank = jnp.zeros(a.shape, I32)
                for j in range(eg):
                    if j == i:
                        continue
                    o = brow[g * eg + j]
                    ahead = (o >= a) if j < i else (o > a)
                    rank = rank + jnp.where(ahead, 1, 0)
                sel[g * eg + i] = jnp.logical_and(is_g, rank < TOP_K)
        denom = jnp.zeros_like(gmax)
        for j in range(N_EXPERTS):
            denom = denom + jnp.where(sel[j], srow[j], 0.0)
        gates = [jnp.where(sel[j], srow[j] / denom, 0.0) for j in range(N_EXPERTS)]
        gt = jnp.concatenate(gates + [jnp.zeros((LANES - N_EXPERTS, tm), F32)], axis=0)
        gate_sc[...] = gt.T
        acc_sc[...] = jnp.zeros_like(acc_sc)

    hg = _mm(xo, wg_ref[0], precise)
    hu = _mm(xo, wu_ref[0], precise)
    ye = _mm(_silu(hg) * hu, wd_ref[0], precise)
    lane = lax.broadcasted_iota(I32, (1, LANES), 1)
    gcol = jnp.sum(jnp.where(lane == e, gate_sc[...], 0.0), axis=1, keepdims=True)
    acc_sc[...] += gcol * ye

    @pl.when(e == N_EXPERTS - 1)
    def _():
        o_ref[...] = _ln(alpha * x + acc_sc[...], g_ref[...], b_ref[...])


def _moe(h, wr_t, br_col, w_gate, w_up, w_down, g, b, alpha, precise):
    n, d = h.shape
    tm = _pick_tile(n, 768)
    assert tm % LANES == 0, tm
    row = lambda i, e: (i, 0)
    fix = lambda i, e: (0, 0)
    per_e = lambda i, e: (e, 0, 0)
    return pl.pallas_call(
        functools.partial(_moe_kernel, alpha=alpha, precise=precise),
        grid=(n // tm, N_EXPERTS),
        in_specs=[pl.BlockSpec((tm, d), row), pl.BlockSpec((LANES, d), fix), pl.BlockSpec((LANES, 1), fix),
                  pl.BlockSpec((1, d, D_EXPERT), per_e), pl.BlockSpec((1, d, D_EXPERT), per_e),
                  pl.BlockSpec((1, D_EXPERT, d), per_e), pl.BlockSpec((1, d), fix), pl.BlockSpec((1, d), fix)],
        out_specs=pl.BlockSpec((tm, d), row),
        out_shape=jax.ShapeDtypeStruct((n, d), F32),
        scratch_shapes=[pltpu.VMEM((tm, d), F32), pltpu.VMEM((tm, LANES), F32)],
        compiler_params=_params(("parallel", "arbitrary")),
        name="moe_ln2",
    )(h, wr_t, br_col, w_gate, w_up, w_down, g.reshape(1, d), b.reshape(1, d))


def _pack_in_weights(w_in_l, b_in_l):
    offs = {}
    off = 0
    for name, n in _SPLITS:
        offs[name] = (off, n)
        off += n
    wcols, bcols = [], []
    for grp in (_GROUP_A, _GROUP_B, _GROUP_C):
        for name, n in grp:
            if name is None:
                wcols.append(jnp.zeros((w_in_l.shape[0], n), w_in_l.dtype))
                bcols.append(jnp.zeros((n,), b_in_l.dtype))
            else:
                o, m = offs[name]
                assert m == n
                wcols.append(w_in_l[:, o:o + n])
                bcols.append(b_in_l[o:o + n])
    return jnp.concatenate(wcols, axis=1), jnp.concatenate(bcols).reshape(1, -1).astype(F32)


def _block_diag_embed(x):
    B, H, r, c = x.shape
    eye = jnp.eye(H, dtype=x.dtype)
    return (x[:, :, :, None, :] * eye[None, :, None, :, None]).reshape(B, H * r, H * c)


def _block_diag_extract(x, H):
    B, R, C = x.shape
    r, c = R // H, C // H
    x5 = x.reshape(B, H, r, H, c)
    return jnp.stack([x5[:, h, :, h, :] for h in range(H)], axis=1)


def _mixer_states_in(s_gla, s_c, s_n, s_m, s_conv):
    nb = s_gla.shape[0]
    s0t = _block_diag_embed(jnp.swapaxes(s_gla.astype(F32), -1, -2))
    c0 = _block_diag_embed(jnp.swapaxes(s_c.astype(F32), -1, -2))
    n0 = s_n.astype(F32).reshape(nb, 1, W_B)
    m0 = jnp.pad(s_m.astype(F32), ((0, 0), (0, LANES - H_B))).reshape(nb, 1, LANES)
    conv0 = jnp.pad(s_conv.astype(F32), ((0, 0), (SUBLANES - (CONV_W - 1), 0), (0, 0)))
    return s0t, c0, n0, m0, conv0


def _key_blocks(k, v, ki, dtype):
    nb, s, _ = k.shape
    s_even = -(-s // (2 * KEY_BLOCK)) * 2 * KEY_BLOCK
    k, v, ki = (jnp.pad(a, ((0, 0), (0, s_even - s), (0, 0))) for a in (k, v, ki))
    nkb = s_even // KEY_BLOCK
    kk = k.astype(dtype).reshape(nb, nkb, KEY_BLOCK, -1)
    vt = jnp.swapaxes(v.astype(dtype).reshape(nb, nkb, KEY_BLOCK, -1), -1, -2)
    kib = ki.astype(dtype).reshape(nb, nkb, KEY_BLOCK, -1)
    return kk, vt, kib


def _layer(h, stream, l, p):
    precise = stream['precise']
    wdt = F32 if precise else BF16
    (w_in_p, b_in_p, wg, bg, gn_a, cw, cb, bf, gn_b, w_out_l, ln1_g, ln1_b, wr_t, br_col,
     w_gate, w_up, w_down, ln2_g, ln2_b, ln_in_g, ln_in_b, alpha) = p
    h, ua, ub, uc = _inproj(h, ln_in_g, ln_in_b, w_in_p.astype(wdt), b_in_p, apply_ln=(l == 0), precise=precise)
    nb, nc, L, lo, hi = stream['nb'], stream['nc'], stream['L'], stream['lo'], stream['hi']
    s0t, c0, n0, m0, conv0 = stream['states'](l)
    ya, st_out = _gla(ua, wg.astype(wdt), bg, gn_a, s0t, nb=nb, nc=nc, L=L, lo=lo, hi=hi, precise=precise)
    yb, c_out, n_out, m_out = _mlstm(ub, cw, cb, bf, gn_b, c0, n0, m0, conv0, nb=nb, nc=nc, L=L, lo=lo, hi=hi,
                                     precise=precise)
    k_arr, vt_arr, ki_arr = stream['keys'](l, uc, wdt)
    yc = _dsa(uc, k_arr, vt_arr, ki_arr, nb=nb, precise=precise, **stream['dsa'])
    h1 = _outproj(ya, yb, yc, h, w_out_l.astype(wdt), ln1_g, ln1_b, alpha, precise)
    h2 = _moe(h1, wr_t.astype(wdt), br_col, w_gate, w_up, w_down, ln2_g, ln2_b, alpha, precise)
    S = jnp.swapaxes(_block_diag_extract(st_out, H_A), -1, -2)
    C = jnp.swapaxes(_block_diag_extract(c_out, H_B), -1, -2)
    return h2, (ub, uc, S, C, n_out.reshape(nb, H_B, DH_B), m_out[:, 0, :H_B])


def kernel(x_prompt, x_sample, cache_k, cache_v, cache_kidx, state_gla, state_mlstm_c, state_mlstm_n, state_mlstm_m, state_conv, meta_tokens, ln_in_g, ln_in_b, w_in, b_in, w_gla_gate, b_gla_gate, g_gla_norm, conv_w, conv_b, b_forget, g_mlstm_norm, w_out, ln1_g, ln1_b, w_router, b_router, w_gate, w_up, w_down, ln2_g, ln2_b):
    depth = w_in.shape[0]
    alpha = float((2 * depth) ** 0.25)
    B, S_p, D = x_prompt.shape
    DB, Ts, _ = x_sample.shape
    P = cache_k.shape[2]
    assert S_p % CHUNK == 0 and CHUNK % Ts == 0 and P % CHUNK == 0 and Ts >= CONV_W - 1
    assert DB * Ts == LANES and Ts % SUBLANES == 0
    KB = KEY_BLOCK

    wr_t = jnp.pad(w_router.T, ((0, LANES - N_EXPERTS), (0, 0)))
    br_col = jnp.pad(b_router.astype(F32), (0, LANES - N_EXPERTS)).reshape(LANES, 1)

    def layer_params(l):
        w_in_p, b_in_p = _pack_in_weights(w_in[l], b_in[l])
        wg = jnp.pad(w_gla_gate[l], ((0, LANES - GATE_RANK), (0, 0)))
        bf = jnp.pad(b_forget[l].astype(F32), (H_B, LANES - 2 * H_B)).reshape(1, LANES)
        return (w_in_p, b_in_p, wg, b_gla_gate[l].reshape(1, -1), g_gla_norm[l].reshape(1, -1),
                conv_w[l], conv_b[l].reshape(1, -1), bf, g_mlstm_norm[l].reshape(1, -1), w_out[l],
                ln1_g[l], ln1_b[l], wr_t, br_col, w_gate[l], w_up[l], w_down[l], ln2_g[l], ln2_b[l],
                ln_in_g, ln_in_b, alpha)

    params = [layer_params(l) for l in range(depth)]

    front = CHUNK - N_META
    t_real = N_META + S_p
    t_pad = -(-(front + t_real) // KB) * KB
    hp = jnp.concatenate([jnp.zeros((B, front, D), x_prompt.dtype),
                          jnp.broadcast_to(meta_tokens.astype(x_prompt.dtype)[None], (B, N_META, D)), x_prompt,
                          jnp.zeros((B, t_pad - front - t_real, D), x_prompt.dtype)], axis=1).reshape(B * t_pad, D)

    def prompt_keys(l, uc, dtype):
        u3 = uc.reshape(B, t_pad, -1)
        return _key_blocks(u3[:, :, 512:640], u3[:, :, 640:768], u3[:, :, 896:896 + D_IDX], dtype)

    zero_states = _mixer_states_in(jnp.zeros((B, H_A, DK_A, DV_A), F32), jnp.zeros((B, H_B, DH_B, DH_B), F32),
                                   jnp.zeros((B, H_B, DH_B), F32), jnp.zeros((B, H_B), F32),
                                   jnp.zeros((B, CONV_W - 1, 2 * W_B), F32))
    prompt_stream = dict(nb=B, nc=t_pad // CHUNK, L=CHUNK, lo=front, hi=front + t_real, precise=False,
                         states=lambda l: zero_states, keys=prompt_keys,
                         dsa=dict(nq=t_pad // LANES, n_sel=min(TOPK_KEYS, S_p // 4), adm_lo=front,
                                  adm_cap=front + t_real, end_base=CHUNK, end_step=2 * CHUNK, end_half=CHUNK,
                                  out_rows=LANES, out_step=0))
    p_out = [[] for _ in range(8)]
    for l in range(depth):
        hp, (ub, uc, S, C, n_o, m_o) = _layer(hp, prompt_stream, l, params[l])
        ub3 = ub.reshape(B, t_pad, -1)[:, front:front + t_real]
        uc3 = uc.reshape(B, t_pad, -1)[:, front:front + t_real]
        vals = (uc3[:, :, 512:640].reshape(B, t_real, HKV_C, DH_C), uc3[:, :, 640:768].reshape(B, t_real, HKV_C, DH_C),
                uc3[:, :, 896:896 + D_IDX], S, C, n_o, m_o, ub3[:, t_real - (CONV_W - 1):, 0:2 * W_B])
        for acc, a in zip(p_out, vals):
            acc.append(a)
    y_prompt = hp.reshape(B, t_pad, D)[:, front + N_META:front + t_real]

    s_tot = P + Ts
    s_pad = -(-s_tot // KB) * KB

    def sample_keys(l, uc, dtype):
        u3 = uc.reshape(DB, Ts, -1)

        def cat(cache, new):
            a = jnp.concatenate([cache.astype(F32).reshape(DB, P, -1), new], axis=1)
            return jnp.pad(a, ((0, 0), (0, s_pad - s_tot), (0, 0)))

        return _key_blocks(cat(cache_k[l], u3[:, :, 512:640]), cat(cache_v[l], u3[:, :, 640:768]),
                           cat(cache_kidx[l], u3[:, :, 896:896 + D_IDX]), dtype)

    sample_stream = dict(nb=DB, nc=1, L=Ts, lo=0, hi=Ts, precise=True,
                         states=lambda l: _mixer_states_in(state_gla[l], state_mlstm_c[l], state_mlstm_n[l],
                                                           state_mlstm_m[l], state_conv[l]),
                         keys=sample_keys,
                         dsa=dict(nq=1, n_sel=min(TOPK_KEYS, s_tot // 4), adm_lo=0, adm_cap=s_tot, end_base=s_tot,
                                  end_step=0, end_half=0, out_rows=Ts, out_step=Ts))
    hs = x_sample.reshape(DB * Ts, D)
    s_out = [[] for _ in range(8)]
    for l in range(depth):
        hs, (ub, uc, S, C, n_o, m_o) = _layer(hs, sample_stream, l, params[l])
        ub3 = ub.reshape(DB, Ts, -1)
        uc3 = uc.reshape(DB, Ts, -1)
        vals = (uc3[:, :, 512:640].reshape(DB, Ts, HKV_C, DH_C), uc3[:, :, 640:768].reshape(DB, Ts, HKV_C, DH_C),
                uc3[:, :, 896:896 + D_IDX], S, C, n_o, m_o, ub3[:, Ts - (CONV_W - 1):, 0:2 * W_B])
        for acc, a in zip(s_out, vals):
            acc.append(a)
    y_sample = hs.reshape(DB, Ts, D)

    return (y_prompt, y_sample, *[jnp.stack(a) for a in p_out], *[jnp.stack(a) for a in s_out])
```

```python
import functools

import jax
import jax.numpy as jnp
from jax import lax
from jax.experimental import pallas as pl
from jax.experimental.pallas import tpu as pltpu

F32 = jnp.float32
BF16 = jnp.bfloat16
I32 = jnp.int32

CHUNK = 64
N_META = 16
H_A, DK_A, DV_A = 4, 32, 64
GATE_RANK, GATE_TAU = 16, 16.0
H_B, DH_B, CONV_W = 4, 64, 4
H_C, HKV_C, DH_C = 8, 2, 64
H_IDX, D_IDX = 4, 32
TOPK_KEYS = 256
N_EXPERTS, N_GROUPS, TOP_K, D_EXPERT = 16, 4, 2, 256
W_A, W_B, W_C = H_A * DV_A, H_B * DH_B, H_C * DH_C
LN_EPS = 1e-5

LANES = 128
SUBLANES = 8
VMEM_LIMIT = 56 * 1024 * 1024
KEY_BLOCK = 256
MOE_TILE = 768
NEG_BIG = -1e30
M_INIT = -1e29
INT_MIN = -2 ** 31
KEY_NEG_INF = INT_MIN + 0x7FFFFF

_SPLITS = (
    ('a_q', H_A * DK_A), ('a_k', H_A * DK_A), ('a_v', W_A), ('a_g', GATE_RANK), ('a_r', W_A),
    ('b_qk', 2 * W_B), ('b_v', W_B), ('b_i', H_B), ('b_f', H_B), ('b_o', W_B),
    ('c_q', W_C), ('c_k', HKV_C * DH_C), ('c_v', HKV_C * DH_C),
    ('c_qi', H_IDX * D_IDX), ('c_w', H_IDX), ('c_ki', D_IDX),
)
_GROUP_A = (('a_q', 128), ('a_k', 128), ('a_v', 256), ('a_r', 256), ('a_g', 16), (None, 112))
_GROUP_B = (('b_qk', 512), ('b_v', 256), ('b_o', 256), ('b_i', 4), ('b_f', 4), (None, 120))
_GROUP_C = (('c_q', 512), ('c_k', 128), ('c_v', 128), ('c_qi', 128), ('c_ki', 32), ('c_w', 4), (None, 92))
_GROUP_WIDTHS = tuple(sum(n for _, n in g) for g in (_GROUP_A, _GROUP_B, _GROUP_C))

_NN = (((1,), (0,)), ((), ()))
_NT = (((1,), (1,)), ((), ()))
_TN = (((0,), (0,)), ((), ()))


def _mm(a, b, precise, dims=_NN):
    if precise:
        return lax.dot_general(a.astype(F32), b.astype(F32), dims, precision=lax.Precision.HIGHEST,
                               preferred_element_type=F32)
    return lax.dot_general(a.astype(BF16), b.astype(BF16), dims, preferred_element_type=F32)


def _log_sigmoid(x):
    return jnp.minimum(x, 0.0) - jnp.log(1.0 + jnp.exp(-jnp.abs(x)))


def _sigmoid(x):
    return 1.0 / (1.0 + jnp.exp(-x))


def _silu(x):
    return x * _sigmoid(x)


def _ln(x, g, b):
    mu = jnp.mean(x, axis=-1, keepdims=True)
    xc = x - mu
    var = jnp.mean(xc * xc, axis=-1, keepdims=True)
    return xc * lax.rsqrt(var + LN_EPS) * g + b


def _pick_tile(n, cap):
    best = None
    for t in range(SUBLANES, min(n, cap) + 1, SUBLANES):
        if n % t == 0:
            best = t
    assert best is not None, n
    return best


def _stack_heads(x, n_heads, width):
    lane_head = lax.broadcasted_iota(I32, (1, n_heads * width), 1) // width
    return jnp.concatenate([jnp.where(lane_head == h, x, 0.0) for h in range(n_heads)], axis=0)


def _head_norm_stacked(o_st, mask_st, n_heads, L, width):
    o_st = jnp.where(mask_st, o_st, 0.0)
    mu = jnp.sum(o_st, axis=-1, keepdims=True) * (1.0 / width)
    xc = jnp.where(mask_st, o_st - mu, 0.0)
    var = jnp.sum(xc * xc, axis=-1, keepdims=True) * (1.0 / width)
    yn = xc * lax.rsqrt(var + LN_EPS)
    y = yn[0:L]
    for h in range(1, n_heads):
        y = y + yn[h * L:(h + 1) * L]
    return y


def _params(sem):
    return pltpu.CompilerParams(dimension_semantics=sem, vmem_limit_bytes=VMEM_LIMIT)


def _inproj_kernel(x_ref, g_ref, b_ref, w_ref, bias_ref, *out_refs, apply_ln, precise):
    x = x_ref[...]
    if apply_ln:
        h_ref, *u_refs = out_refs
        x = _ln(x, g_ref[...], b_ref[...])
        h_ref[...] = x
    else:
        u_refs = out_refs
    xo = x if precise else x.astype(BF16)
    off = 0
    for r, n in zip(u_refs, _GROUP_WIDTHS):
        r[...] = _mm(xo, w_ref[:, off:off + n], precise) + bias_ref[:, off:off + n]
        off += n


def _inproj(x, ln_g, ln_b, w_packed, b_packed, apply_ln, precise):
    n, d = x.shape
    tm = _pick_tile(n, 384)
    dp = w_packed.shape[1]
    row = lambda i: (i, 0)
    fix = lambda i: (0, 0)
    out_shape = [jax.ShapeDtypeStruct((n, w), F32) for w in _GROUP_WIDTHS]
    out_specs = [pl.BlockSpec((tm, w), row) for w in _GROUP_WIDTHS]
    if apply_ln:
        out_shape = [jax.ShapeDtypeStruct((n, d), F32)] + out_shape
        out_specs = [pl.BlockSpec((tm, d), row)] + out_specs
    outs = pl.pallas_call(
        functools.partial(_inproj_kernel, apply_ln=apply_ln, precise=precise),
        grid=(n // tm,),
        in_specs=[pl.BlockSpec((tm, d), row), pl.BlockSpec((1, d), fix), pl.BlockSpec((1, d), fix),
                  pl.BlockSpec((d, dp), fix), pl.BlockSpec((1, dp), fix)],
        out_specs=out_specs, out_shape=out_shape,
        compiler_params=_params(("parallel",)),
        name="inproj_ln" if apply_ln else "inproj",
    )(x, ln_g.reshape(1, d), ln_b.reshape(1, d), w_packed, b_packed)
    if apply_ln:
        return outs[0], outs[1], outs[2], outs[3]
    return x, outs[0], outs[1], outs[2]


def _gla_step(u, wg, bg, gn, st, c, *, L, lo, hi, precise):
    q = u[:, 0:128] * (DK_A ** -0.5)
    k = u[:, 128:256]
    v = u[:, 256:512]
    r = u[:, 512:768]
    g = u[:, 768:896]
    la = _log_sigmoid(_mm(g, wg, precise) + bg) * (1.0 / GATE_TAU)
    pos = c * L + lax.broadcasted_iota(I32, (L, 1), 0)
    real = (pos >= lo) & (pos < hi)
    la = jnp.where(real, la, 0.0)
    k = jnp.where(real, k, 0.0)
    ti = lax.broadcasted_iota(I32, (L, L), 0)
    si = lax.broadcasted_iota(I32, (L, L), 1)
    b = _mm((ti >= si).astype(F32), la, True)
    bmid = b[L // 2 - 1:L // 2, :]
    bl = b[L - 1:L, :]
    qs = _stack_heads(q * jnp.exp(b - bmid), H_A, DK_A)
    qbs = _stack_heads(q * jnp.exp(b), H_A, DK_A)
    kt = k * jnp.exp(bmid - b)
    kh = k * jnp.exp(bl - b)
    att = _mm(qs, kt, precise, _NT)
    tr = lax.broadcasted_iota(I32, (H_A * L, L), 0) % L
    sc = lax.broadcasted_iota(I32, (H_A * L, L), 1)
    att = jnp.where(sc <= tr, att, 0.0)
    row_head = lax.broadcasted_iota(I32, (H_A * L, W_A), 0) // L
    lane_head = lax.broadcasted_iota(I32, (H_A * L, W_A), 1) // DV_A
    mask_st = row_head == lane_head
    o_st = jnp.where(mask_st, _mm(att, v, precise), 0.0) + _mm(qbs, st, precise, _NT)
    y = _head_norm_stacked(o_st, mask_st, H_A, L, DV_A) * gn * _silu(r)
    bd = (lax.broadcasted_iota(I32, (W_A, H_A * DK_A), 0) // DV_A) == (lax.broadcasted_iota(I32, (W_A, H_A * DK_A), 1) // DK_A)
    return y, st * jnp.exp(bl) + jnp.where(bd, _mm(v, kh, precise, _TN), 0.0)


def _mlstm_step(u, cw, cb, bf, gn, cbd, nrow, m_prev, buf, c, *, L, lo, hi, precise):
    H, D = H_B, DH_B
    pos = c * L + lax.broadcasted_iota(I32, (L, 1), 0)
    real = (pos >= lo) & (pos < hi)
    buf[SUBLANES:SUBLANES + L, :] = jnp.where(real, u[:, 0:2 * W_B], 0.0)
    hist = SUBLANES - (CONV_W - 1)
    conv = buf[hist:hist + L, :] * cw[0:1, :]
    for j in range(1, CONV_W):
        conv = conv + buf[hist + j:hist + j + L, :] * cw[j:j + 1, :]
    conv = conv + cb
    buf[0:SUBLANES, :] = buf[L:L + SUBLANES, :]
    act = _silu(conv)
    q = act[:, 0:W_B]
    k = act[:, W_B:2 * W_B] * (D ** -0.5)
    v = u[:, 512:768]
    og = u[:, 768:1024]
    gts = u[:, 1024:1152]
    ig_col = jnp.where(real, gts, NEG_BIG)
    lf_col = jnp.where(real, _log_sigmoid(gts + bf), 0.0)
    ti = lax.broadcasted_iota(I32, (L, L), 0)
    si = lax.broadcasted_iota(I32, (L, L), 1)
    b_col = _mm((ti >= si).astype(F32), lf_col, True)
    lane = lax.broadcasted_iota(I32, (1, LANES), 1)
    z = jnp.where(lane < H, ig_col, b_col)
    zt = jnp.concatenate([z, jnp.zeros((LANES - L, LANES), F32)], axis=0).T
    b_st = jnp.concatenate([b_col[:, H + h:H + h + 1] for h in range(H)], axis=0)
    m0_st = jnp.concatenate([jnp.broadcast_to(m_prev[:, h:h + 1], (L, 1)) for h in range(H)], axis=0)
    rowv_st = jnp.concatenate(
        [jnp.broadcast_to(zt[h:h + 1, 0:L] - zt[H + h:H + h + 1, 0:L], (L, L)) for h in range(H)], axis=0)
    tr = lax.broadcasted_iota(I32, (H * L, L), 0) % L
    sc = lax.broadcasted_iota(I32, (H * L, L), 1)
    logw = jnp.where(sc <= tr, b_st + rowv_st, NEG_BIG)
    lp = b_st + m0_st
    m_st = jnp.maximum(lp, jnp.max(logw, axis=1, keepdims=True))
    w = jnp.exp(logw - m_st)
    sp = jnp.exp(lp - m_st)
    qs = _stack_heads(q, H, D)
    qk = _mm(qs, k, precise, _NT) * w
    row_head = lax.broadcasted_iota(I32, (H * L, W_B), 0) // L
    lane_head = lax.broadcasted_iota(I32, (H * L, W_B), 1) // D
    mask_st = row_head == lane_head
    num = jnp.where(mask_st, _mm(qk, v, precise), 0.0) + sp * _mm(qs, cbd, precise)
    den = jnp.sum(qk, axis=1, keepdims=True) + sp * jnp.sum(qs * nrow, axis=1, keepdims=True)
    hh = num / jnp.maximum(jnp.abs(den), jnp.exp(-m_st))
    y = _head_norm_stacked(hh, mask_st, H, L, D) * gn * _sigmoid(og)
    lh = lax.broadcasted_iota(I32, (1, W_B), 1) // D
    wlk = jnp.zeros((L, W_B), F32)
    sl_lanes = jnp.zeros((1, W_B), F32)
    m_new = jnp.zeros((1, LANES), F32)
    for h in range(H):
        ml = m_st[h * L + L - 1:h * L + L, :]
        bl = b_st[h * L + L - 1:h * L + L, :]
        wl = jnp.exp(bl - b_col[:, H + h:H + h + 1] + ig_col[:, h:h + 1] - ml)
        sl = jnp.exp(bl + m_prev[:, h:h + 1] - ml)
        wlk = jnp.where(lh == h, wl, wlk)
        sl_lanes = jnp.where(lh == h, sl, sl_lanes)
        m_new = jnp.where(lane == h, ml, m_new)
    kw = k * wlk
    bd = (lax.broadcasted_iota(I32, (W_B, W_B), 0) // D) == (lax.broadcasted_iota(I32, (W_B, W_B), 1) // D)
    c_new = sl_lanes * cbd + jnp.where(bd, _mm(kw, v, precise, _TN), 0.0)
    n_new = sl_lanes * nrow + jnp.sum(kw, axis=0, keepdims=True)
    return y, c_new, n_new, m_new


def _scan_kernel(ua_ref, ub_ref, wg_ref, bg_ref, gna_ref, cw_ref, cb_ref, bf_ref, gnb_ref,
                 s0_ref, c0_ref, n0_ref, m0_ref, conv0_ref,
                 ya_ref, yb_ref, sout_ref, cout_ref, nout_ref, mout_ref,
                 st_sc, c_sc, n_sc, m_sc, buf_sc, *, L, lo, hi, precise, bpg):
    c = pl.program_id(1)

    @pl.when(c == 0)
    def _():
        st_sc[...] = s0_ref[...]
        c_sc[...] = c0_ref[...]
        n_sc[...] = n0_ref[...]
        m_sc[...] = m0_ref[...]
        buf_sc[:, 0:SUBLANES, :] = conv0_ref[...]

    for b in range(bpg):
        y, st_new = _gla_step(ua_ref[b], wg_ref[...], bg_ref[...], gna_ref[...], st_sc[b], c,
                              L=L, lo=lo, hi=hi, precise=precise)
        ya_ref[b] = y
        st_sc[b] = st_new
        sout_ref[b] = st_new
    for b in range(bpg):
        y, c_new, n_new, m_new = _mlstm_step(ub_ref[b], cw_ref[...], cb_ref[...], bf_ref[...], gnb_ref[...],
                                             c_sc[b], n_sc[b], m_sc[b], buf_sc.at[b], c,
                                             L=L, lo=lo, hi=hi, precise=precise)
        yb_ref[b] = y
        c_sc[b] = c_new
        n_sc[b] = n_new
        m_sc[b] = m_new
        cout_ref[b] = c_new
        nout_ref[b] = n_new
        mout_ref[b] = m_new


def _scan(ua, ub, wg, bg, gna, cw, cb, bf, gnb, s0t, c0, n0, m0, conv0, *, nb, nc, L, lo, hi, precise):
    bpg = 2
    assert nb % bpg == 0
    rows = ua.shape[0]
    t = rows // nb
    fix2 = lambda g, c: (0, 0)
    tok = lambda g, c: (g, c, 0)
    per_g = lambda g, c: (g, 0, 0)
    kd = H_A * DK_A
    ya, yb, st_out, c_out, n_out, m_out = pl.pallas_call(
        functools.partial(_scan_kernel, L=L, lo=lo, hi=hi, precise=precise, bpg=bpg),
        grid=(nb // bpg, nc),
        in_specs=[pl.BlockSpec((bpg, L, _GROUP_WIDTHS[0]), tok), pl.BlockSpec((bpg, L, _GROUP_WIDTHS[1]), tok),
                  pl.BlockSpec((LANES, LANES), fix2), pl.BlockSpec((1, LANES), fix2), pl.BlockSpec((1, W_A), fix2),
                  pl.BlockSpec((CONV_W, 2 * W_B), fix2), pl.BlockSpec((1, 2 * W_B), fix2),
                  pl.BlockSpec((1, LANES), fix2), pl.BlockSpec((1, W_B), fix2),
                  pl.BlockSpec((bpg, W_A, kd), per_g), pl.BlockSpec((bpg, W_B, W_B), per_g),
                  pl.BlockSpec((bpg, 1, W_B), per_g), pl.BlockSpec((bpg, 1, LANES), per_g),
                  pl.BlockSpec((bpg, SUBLANES, 2 * W_B), per_g)],
        out_specs=[pl.BlockSpec((bpg, L, W_A), tok), pl.BlockSpec((bpg, L, W_B), tok),
                   pl.BlockSpec((bpg, W_A, kd), per_g), pl.BlockSpec((bpg, W_B, W_B), per_g),
                   pl.BlockSpec((bpg, 1, W_B), per_g), pl.BlockSpec((bpg, 1, LANES), per_g)],
        out_shape=[jax.ShapeDtypeStruct((nb, t, W_A), F32), jax.ShapeDtypeStruct((nb, t, W_B), F32),
                   jax.ShapeDtypeStruct((nb, W_A, kd), F32), jax.ShapeDtypeStruct((nb, W_B, W_B), F32),
                   jax.ShapeDtypeStruct((nb, 1, W_B), F32), jax.ShapeDtypeStruct((nb, 1, LANES), F32)],
        scratch_shapes=[pltpu.VMEM((bpg, W_A, kd), F32), pltpu.VMEM((bpg, W_B, W_B), F32),
                        pltpu.VMEM((bpg, 1, W_B), F32), pltpu.VMEM((bpg, 1, LANES), F32),
                        pltpu.VMEM((bpg, L + SUBLANES, 2 * W_B), F32)],
        compiler_params=_params(("parallel", "arbitrary")),
        name="mixer_scans",
    )(ua.reshape(nb, t, -1), ub.reshape(nb, t, -1), wg, bg, gna, cw, cb, bf, gnb, s0t, c0, n0, m0, conv0)
    return ya.reshape(rows, W_A), yb.reshape(rows, W_B), st_out, c_out, n_out, m_out


def _dsa_kernel(u_ref, k_ref, vt_ref, ki_ref, y_ref, sc_sc, sa_sc, sb_sc, acc_sc, m_sc, l_sc, y_sc,
                *, KB, nkb_alloc, n_sel, adm_lo, adm_cap, end_base, end_step, end_half, out_rows, out_step, precise):
    b = pl.program_id(0)
    c = pl.program_id(1)
    TQ = LANES
    G = H_C // HKV_C
    lane_q = lax.broadcasted_iota(I32, (1, TQ), 1)
    end_lo = end_base + end_step * c
    end_q = jnp.minimum(jnp.where(lane_q < CHUNK, end_lo, end_lo + end_half), adm_cap)
    end_max = jnp.minimum(end_lo + end_half, adm_cap)
    npair = (end_max + 2 * KB - 1) // (2 * KB)
    u = u_ref[...]
    qi_t = (u[:, 768:896] * (D_IDX ** -0.5)).T
    kw_t = u[:, 896:1024].T
    qi_st = jnp.concatenate([qi_t[h * D_IDX:(h + 1) * D_IDX, :] for h in range(H_IDX)], axis=1)
    if not precise:
        qi_st = qi_st.astype(BF16)
    wi = [kw_t[D_IDX + h:D_IDX + h + 1, :] * (H_IDX ** -0.5) for h in range(H_IDX)]
    key_idx = lax.broadcasted_iota(I32, (KB, TQ), 0)

    def score_block(kb):
        raw = jnp.maximum(_mm(ki_ref[0, kb], qi_st, precise), 0.0)
        isc = wi[0] * raw[:, 0:TQ]
        for h in range(1, H_IDX):
            isc = isc + wi[h] * raw[:, h * TQ:(h + 1) * TQ]
        idx = kb * KB + key_idx
        sc_sc[kb] = jnp.where((idx >= adm_lo) & (idx < end_q), isc, -jnp.inf)

    def score_body(i, carry):
        score_block(2 * i)
        score_block(2 * i + 1)
        return carry

    lax.fori_loop(0, npair, score_body, 0)

    def count(pred_fn):
        def body(i, acc):
            for kb in (2 * i, 2 * i + 1):
                hit = jnp.where(pred_fn(sc_sc[kb]), 1, 0)
                acc = acc + jnp.sum(hit.reshape(KB // SUBLANES, SUBLANES, TQ), axis=0)
            return acc
        acc = lax.fori_loop(0, npair, body, jnp.zeros((SUBLANES, TQ), I32))
        return jnp.sum(acc, axis=0, keepdims=True)

    def key_to_float(key):
        return lax.bitcast_convert_type(jnp.where(key >= 0, key, key ^ 0x7FFFFFFF), F32)

    def thr_body(i, p):
        cand = p + lax.shift_left(jnp.int32(1), 31 - i)
        cand_f = key_to_float(cand)
        cnt = count(lambda s: s >= cand_f)
        return jnp.where(cnt >= n_sel, cand, p)

    thr_key = lax.fori_loop(0, 32, thr_body, jnp.full((1, TQ), INT_MIN, I32))
    thr = key_to_float(jnp.maximum(thr_key, KEY_NEG_INF))
    need = (n_sel - count(lambda s: s > thr)).astype(F32)

    q_t = [(u[:, j * LANES:(j + 1) * LANES] * (DH_C ** -0.5)).T for j in range(W_C // LANES)]
    zero_half = jnp.zeros((DH_C, G * TQ), F32)
    qz = []
    for n in range(HKV_C):
        heads = [n * G + g for g in range(G)]
        qn = jnp.concatenate([q_t[hc // 2][(hc % 2) * DH_C:(hc % 2 + 1) * DH_C, :] for hc in heads], axis=1)
        qn = jnp.concatenate([qn, zero_half] if n == 0 else [zero_half, qn], axis=0)
        qz.append(qn if precise else qn.astype(BF16))
    tri = (lax.broadcasted_iota(I32, (KB, KB), 0) >= lax.broadcasted_iota(I32, (KB, KB), 1)).astype(BF16)

    def mask_body(i, tie_seen):
        for kb in (2 * i, 2 * i + 1):
            s_idx = sc_sc[kb]
            eq = (s_idx == thr) & (s_idx > -jnp.inf)
            rank = tie_seen + jnp.dot(tri, jnp.where(eq, 1.0, 0.0).astype(BF16), preferred_element_type=F32)
            sel = (s_idx > thr) | (eq & (rank <= need))
            sc_sc[kb] = jnp.where(sel, 0.0, NEG_BIG)
            tie_seen = rank[KB - 1:KB, :]
        return tie_seen

    lax.fori_loop(0, npair, mask_body, jnp.zeros((1, TQ), F32))

    acc_sc[...] = jnp.zeros_like(acc_sc)
    m_sc[...] = jnp.full_like(m_sc, M_INIT)
    l_sc[...] = jnp.zeros_like(l_sc)
    W2 = 2 * TQ
    units = [(n, slice(gp * W2, (gp + 1) * W2)) for n in range(HKV_C) for gp in range(G // 2)]

    def logits(kb, dst):
        kblk = k_ref[0, kb]
        for ui, (n, cols) in enumerate(units):
            dst[ui] = _mm(kblk, qz[n][:, cols], precise)

    def att_step(kb, cur, nxt):
        logits(jnp.minimum(kb + 1, nkb_alloc - 1), nxt)
        bias = sc_sc[kb]
        bias2 = jnp.concatenate([bias, bias], axis=1)
        vtblk = vt_ref[0, kb]
        for ui, (n, cols) in enumerate(units):
            s = cur[ui] + bias2
            m_old = m_sc[n, :, cols]
            m_new = jnp.maximum(m_old, jnp.max(s, axis=0, keepdims=True))
            p = jnp.exp(s - m_new)
            alpha = jnp.exp(m_old - m_new)
            l_sc[n, :, cols] = alpha * l_sc[n, :, cols] + jnp.sum(p, axis=0, keepdims=True)
            acc_sc[n, :, cols] = alpha * acc_sc[n, :, cols] + _mm(vtblk[n * DH_C:(n + 1) * DH_C, :], p, precise)
            m_sc[n, :, cols] = m_new

    logits(0, sa_sc)

    def att_body(i, carry):
        att_step(2 * i, sa_sc, sb_sc)
        att_step(2 * i + 1, sb_sc, sa_sc)
        return carry

    lax.fori_loop(0, npair, att_body, 0)
    o = [acc_sc[n] / l_sc[n] for n in range(HKV_C)]
    for j in range(W_C // LANES):
        n, g0 = (2 * j) // G, (2 * j) % G
        pair = jnp.concatenate([o[n][:, g0 * TQ:(g0 + 1) * TQ], o[n][:, (g0 + 1) * TQ:(g0 + 2) * TQ]], axis=0)
        y_sc[:, j * LANES:(j + 1) * LANES] = pair.T
    off = pl.multiple_of(b * out_step, SUBLANES)
    y_ref[...] = y_sc[pl.ds(off, out_rows), :]


def _dsa(uc, k_arr, vt_arr, ki_arr, *, nb, nq, n_sel, adm_lo, adm_cap, end_base, end_step, end_half,
         out_rows, out_step, precise):
    rows = uc.shape[0]
    nkb_max, KB = k_arr.shape[1], k_arr.shape[2]
    assert nkb_max % 2 == 0
    G = H_C // HKV_C
    per_b = lambda b, c: (b, 0, 0, 0)
    u_map = (lambda b, c: (b * nq + c, 0)) if out_step == 0 else (lambda b, c: (0, 0))
    return pl.pallas_call(
        functools.partial(_dsa_kernel, KB=KB, nkb_alloc=nkb_max, n_sel=n_sel, adm_lo=adm_lo, adm_cap=adm_cap,
                          end_base=end_base, end_step=end_step, end_half=end_half, out_rows=out_rows,
                          out_step=out_step, precise=precise),
        grid=(nb, nq),
        in_specs=[pl.BlockSpec((LANES, _GROUP_WIDTHS[2]), u_map),
                  pl.BlockSpec((1, nkb_max, KB, HKV_C * DH_C), per_b),
                  pl.BlockSpec((1, nkb_max, HKV_C * DH_C, KB), per_b),
                  pl.BlockSpec((1, nkb_max, KB, D_IDX), per_b)],
        out_specs=pl.BlockSpec((out_rows, W_C), lambda b, c: (b * nq + c, 0)),
        out_shape=jax.ShapeDtypeStruct((rows, W_C), F32),
        scratch_shapes=[pltpu.VMEM((nkb_max, KB, LANES), F32), pltpu.VMEM((HKV_C * (G // 2), KB, 2 * LANES), F32),
                        pltpu.VMEM((HKV_C * (G // 2), KB, 2 * LANES), F32),
                        pltpu.VMEM((HKV_C, DH_C, G * LANES), F32), pltpu.VMEM((HKV_C, 1, G * LANES), F32),
                        pltpu.VMEM((HKV_C, 1, G * LANES), F32), pltpu.VMEM((LANES, W_C), F32)],
        compiler_params=_params(("parallel", "arbitrary")),
        name="dsa_attend",
    )(uc, k_arr, vt_arr, ki_arr)


def _outproj_kernel(ya_ref, yb_ref, yc_ref, h_ref, w_ref, g_ref, b_ref, o_ref, *, alpha, precise):
    mix = (_mm(ya_ref[...], w_ref[0:W_A, :], precise) + _mm(yb_ref[...], w_ref[W_A:W_A + W_B, :], precise)
           + _mm(yc_ref[...], w_ref[W_A + W_B:, :], precise))
    o_ref[...] = _ln(alpha * h_ref[...] + mix, g_ref[...], b_ref[...])


def _outproj(ya, yb, yc, h, w_out_l, g, b, alpha, precise):
    n, d = h.shape
    tm = _pick_tile(n, 768)
    row = lambda i: (i, 0)
    fix = lambda i: (0, 0)
    return pl.pallas_call(
        functools.partial(_outproj_kernel, alpha=alpha, precise=precise),
        grid=(n // tm,),
        in_specs=[pl.BlockSpec((tm, W_A), row), pl.BlockSpec((tm, W_B), row), pl.BlockSpec((tm, W_C), row),
                  pl.BlockSpec((tm, d), row), pl.BlockSpec(w_out_l.shape, fix),
                  pl.BlockSpec((1, d), fix), pl.BlockSpec((1, d), fix)],
        out_specs=pl.BlockSpec((tm, d), row),
        out_shape=jax.ShapeDtypeStruct((n, d), F32),
        compiler_params=_params(("parallel",)),
        name="outproj_ln1",
    )(ya, yb, yc, h, w_out_l, g.reshape(1, d), b.reshape(1, d))


def _moe_kernel(h_ref, wr_ref, br_ref, wg_ref, wu_ref, wd_ref, g_ref, b_ref, o_ref, acc_sc, gate_sc, *, alpha, precise):
    grp = pl.program_id(1)
    x = h_ref[...]
    tm = x.shape[0]
    xo = x if precise else x.astype(BF16)
    eg = N_EXPERTS // N_GROUPS

    @pl.when(grp == 0)
    def _():
        s = _sigmoid(_mm(wr_ref[...], xo, precise, _NT))
        sb = s + br_ref[...]
        srow = [s[j:j + 1, :] for j in range(N_EXPERTS)]
        brow = [sb[j:j + 1, :] for j in range(N_EXPERTS)]
        gscore = []
        for g in range(N_GROUPS):
            best = None
            for i in range(eg):
                for j in range(i + 1, eg):
                    pair = brow[g * eg + i] + brow[g * eg + j]
                    best = pair if best is None else jnp.maximum(best, pair)
            gscore.append(best)
        gmax = functools.reduce(jnp.maximum, gscore)
        taken = jnp.zeros_like(gmax) > 1.0
        sel = [None] * N_EXPERTS
        for g in range(N_GROUPS):
            is_g = jnp.logical_and(jnp.logical_not(taken), gscore[g] == gmax)
            taken = jnp.logical_or(taken, is_g)
            for i in range(eg):
                a = brow[g * eg + i]
                rank = jnp.zeros(a.shape, I32)
                for j in range(eg):
                    if j == i:
                        continue
                    o = brow[g * eg + j]
                    ahead = (o >= a) if j < i else (o > a)
                    rank = rank + jnp.where(ahead, 1, 0)
                sel[g * eg + i] = jnp.logical_and(is_g, rank < TOP_K)
        denom = jnp.zeros_like(gmax)
        for j in range(N_EXPERTS):
            denom = denom + jnp.where(sel[j], srow[j], 0.0)
        gates = [jnp.where(sel[j], srow[j] / denom, 0.0) for j in range(N_EXPERTS)]
        gt = jnp.concatenate(gates + [jnp.zeros((LANES - N_EXPERTS, tm), F32)], axis=0)
        gate_sc[...] = gt.T
        acc_sc[...] = jnp.zeros_like(acc_sc)

    lane = lax.broadcasted_iota(I32, (1, LANES), 1)
    gate = gate_sc[...]
    hes = []
    for j in range(eg):
        gcol = jnp.sum(jnp.where(lane == grp * eg + j, gate, 0.0), axis=1, keepdims=True)
        hes.append(gcol * (_silu(_mm(xo, wg_ref[j], precise)) * _mm(xo, wu_ref[j], precise)))
    acc_sc[...] += _mm(jnp.concatenate(hes, axis=1), wd_ref[0], precise)

    @pl.when(grp == N_GROUPS - 1)
    def _():
        o_ref[...] = _ln(alpha * x + acc_sc[...], g_ref[...], b_ref[...])


def _moe(h, wr_t, br_col, w_gate, w_up, w_down, g, b, alpha, precise):
    n, d = h.shape
    tm = _pick_tile(n, MOE_TILE)
    assert tm % LANES == 0, tm
    eg = N_EXPERTS // N_GROUPS
    row = lambda i, e: (i, 0)
    fix = lambda i, e: (0, 0)
    per_g = lambda i, e: (e, 0, 0)
    return pl.pallas_call(
        functools.partial(_moe_kernel, alpha=alpha, precise=precise),
        grid=(n // tm, N_GROUPS),
        in_specs=[pl.BlockSpec((tm, d), row), pl.BlockSpec((LANES, d), fix), pl.BlockSpec((LANES, 1), fix),
                  pl.BlockSpec((eg, d, D_EXPERT), per_g), pl.BlockSpec((eg, d, D_EXPERT), per_g),
                  pl.BlockSpec((1, eg * D_EXPERT, d), per_g), pl.BlockSpec((1, d), fix), pl.BlockSpec((1, d), fix)],
        out_specs=pl.BlockSpec((tm, d), row),
        out_shape=jax.ShapeDtypeStruct((n, d), F32),
        scratch_shapes=[pltpu.VMEM((tm, d), F32), pltpu.VMEM((tm, LANES), F32)],
        compiler_params=_params(("parallel", "arbitrary")),
        name="moe_ln2",
    )(h, wr_t, br_col, w_gate, w_up, w_down.reshape(N_GROUPS, eg * D_EXPERT, d), g.reshape(1, d), b.reshape(1, d))


def _pack_in_weights(w_in_l, b_in_l):
    offs = {}
    off = 0
    for name, n in _SPLITS:
        offs[name] = (off, n)
        off += n
    wcols, bcols = [], []
    for grp in (_GROUP_A, _GROUP_B, _GROUP_C):
        for name, n in grp:
            if name is None:
                wcols.append(jnp.zeros((w_in_l.shape[0], n), w_in_l.dtype))
                bcols.append(jnp.zeros((n,), b_in_l.dtype))
            else:
                o, m = offs[name]
                assert m == n
                wcols.append(w_in_l[:, o:o + n])
                bcols.append(b_in_l[o:o + n])
    return jnp.concatenate(wcols, axis=1), jnp.concatenate(bcols).reshape(1, -1).astype(F32)


def _block_diag_embed(x):
    B, H, r, c = x.shape
    eye = jnp.eye(H, dtype=x.dtype)
    return (x[:, :, :, None, :] * eye[None, :, None, :, None]).reshape(B, H * r, H * c)


def _block_diag_extract(x, H):
    B, R, C = x.shape
    r, c = R // H, C // H
    x5 = x.reshape(B, H, r, H, c)
    return jnp.stack([x5[:, h, :, h, :] for h in range(H)], axis=1)


def _mixer_states_in(s_gla, s_c, s_n, s_m, s_conv):
    nb = s_gla.shape[0]
    s0t = _block_diag_embed(jnp.swapaxes(s_gla.astype(F32), -1, -2))
    c0 = _block_diag_embed(jnp.swapaxes(s_c.astype(F32), -1, -2))
    n0 = s_n.astype(F32).reshape(nb, 1, W_B)
    m0 = jnp.pad(s_m.astype(F32), ((0, 0), (0, LANES - H_B))).reshape(nb, 1, LANES)
    conv0 = jnp.pad(s_conv.astype(F32), ((0, 0), (SUBLANES - (CONV_W - 1), 0), (0, 0)))
    return s0t, c0, n0, m0, conv0


def _key_blocks(k, v, ki, dtype):
    nb, s, _ = k.shape
    s_even = -(-s // (2 * KEY_BLOCK)) * 2 * KEY_BLOCK
    k, v, ki = (jnp.pad(a, ((0, 0), (0, s_even - s), (0, 0))) for a in (k, v, ki))
    nkb = s_even // KEY_BLOCK
    kk = k.astype(dtype).reshape(nb, nkb, KEY_BLOCK, -1)
    vt = jnp.swapaxes(v.astype(dtype).reshape(nb, nkb, KEY_BLOCK, -1), -1, -2)
    kib = ki.astype(dtype).reshape(nb, nkb, KEY_BLOCK, -1)
    return kk, vt, kib


def _layer(h, stream, l, p):
    precise = stream['precise']
    wdt = F32 if precise else BF16
    (w_in_p, b_in_p, wg, bg, gn_a, cw, cb, bf, gn_b, w_out_l, ln1_g, ln1_b, wr_t, br_col,
     w_gate, w_up, w_down, ln2_g, ln2_b, ln_in_g, ln_in_b, alpha) = p
    h, ua, ub, uc = _inproj(h, ln_in_g, ln_in_b, w_in_p.astype(wdt), b_in_p, apply_ln=(l == 0), precise=precise)
    nb, nc, L, lo, hi = stream['nb'], stream['nc'], stream['L'], stream['lo'], stream['hi']
    s0t, c0, n0, m0, conv0 = stream['states'](l)
    ya, yb, st_out, c_out, n_out, m_out = _scan(ua, ub, wg.astype(wdt), bg, gn_a, cw, cb, bf, gn_b, s0t, c0, n0, m0,
                                                conv0, nb=nb, nc=nc, L=L, lo=lo, hi=hi, precise=precise)
    k_arr, vt_arr, ki_arr = stream['keys'](l, uc, wdt)
    yc = _dsa(uc, k_arr, vt_arr, ki_arr, nb=nb, precise=precise, **stream['dsa'])
    h1 = _outproj(ya, yb, yc, h, w_out_l.astype(wdt), ln1_g, ln1_b, alpha, precise)
    h2 = _moe(h1, wr_t.astype(wdt), br_col, w_gate.astype(wdt), w_up.astype(wdt), w_down.astype(wdt), ln2_g, ln2_b,
              alpha, precise)
    S = jnp.swapaxes(_block_diag_extract(st_out, H_A), -1, -2)
    C = jnp.swapaxes(_block_diag_extract(c_out, H_B), -1, -2)
    return h2, (ub, uc, S, C, n_out.reshape(nb, H_B, DH_B), m_out[:, 0, :H_B])


def kernel(x_prompt, x_sample, cache_k, cache_v, cache_kidx, state_gla, state_mlstm_c, state_mlstm_n, state_mlstm_m, state_conv, meta_tokens, ln_in_g, ln_in_b, w_in, b_in, w_gla_gate, b_gla_gate, g_gla_norm, conv_w, conv_b, b_forget, g_mlstm_norm, w_out, ln1_g, ln1_b, w_router, b_router, w_gate, w_up, w_down, ln2_g, ln2_b):
    depth = w_in.shape[0]
    alpha = float((2 * depth) ** 0.25)
    B, S_p, D = x_prompt.shape
    DB, Ts, _ = x_sample.shape
    P = cache_k.shape[2]
    assert S_p % CHUNK == 0 and CHUNK % Ts == 0 and P % CHUNK == 0 and Ts >= CONV_W - 1
    assert DB * Ts == LANES and Ts % SUBLANES == 0
    KB = KEY_BLOCK

    wr_t = jnp.pad(w_router.T, ((0, LANES - N_EXPERTS), (0, 0)))
    br_col = jnp.pad(b_router.astype(F32), (0, LANES - N_EXPERTS)).reshape(LANES, 1)

    def layer_params(l):
        w_in_p, b_in_p = _pack_in_weights(w_in[l], b_in[l])
        wg = jnp.pad(w_gla_gate[l], ((0, LANES - GATE_RANK), (0, 0)))
        bf = jnp.pad(b_forget[l].astype(F32), (H_B, LANES - 2 * H_B)).reshape(1, LANES)
        return (w_in_p, b_in_p, wg, b_gla_gate[l].reshape(1, -1), g_gla_norm[l].reshape(1, -1),
                conv_w[l], conv_b[l].reshape(1, -1), bf, g_mlstm_norm[l].reshape(1, -1), w_out[l],
                ln1_g[l], ln1_b[l], wr_t, br_col, w_gate[l], w_up[l], w_down[l], ln2_g[l], ln2_b[l],
                ln_in_g, ln_in_b, alpha)

    params = [layer_params(l) for l in range(depth)]

    front = CHUNK - N_META
    t_real = N_META + S_p
    t_pad = -(-(front + t_real) // KB) * KB
    hp = jnp.concatenate([jnp.zeros((B, front, D), x_prompt.dtype),
                          jnp.broadcast_to(meta_tokens.astype(x_prompt.dtype)[None], (B, N_META, D)), x_prompt,
                          jnp.zeros((B, t_pad - front - t_real, D), x_prompt.dtype)], axis=1).reshape(B * t_pad, D)

    def prompt_keys(l, uc, dtype):
        u3 = uc.reshape(B, t_pad, -1)
        return _key_blocks(u3[:, :, 512:640], u3[:, :, 640:768], u3[:, :, 896:896 + D_IDX], dtype)

    zero_states = _mixer_states_in(jnp.zeros((B, H_A, DK_A, DV_A), F32), jnp.zeros((B, H_B, DH_B, DH_B), F32),
                                   jnp.zeros((B, H_B, DH_B), F32), jnp.zeros((B, H_B), F32),
                                   jnp.zeros((B, CONV_W - 1, 2 * W_B), F32))
    prompt_stream = dict(nb=B, nc=t_pad // CHUNK, L=CHUNK, lo=front, hi=front + t_real, precise=False,
                         states=lambda l: zero_states, keys=prompt_keys,
                         dsa=dict(nq=t_pad // LANES, n_sel=min(TOPK_KEYS, S_p // 4), adm_lo=front,
                                  adm_cap=front + t_real, end_base=CHUNK, end_step=2 * CHUNK, end_half=CHUNK,
                                  out_rows=LANES, out_step=0))
    p_out = [[] for _ in range(8)]
    for l in range(depth):
        hp, (ub, uc, S, C, n_o, m_o) = _layer(hp, prompt_stream, l, params[l])
        ub3 = ub.reshape(B, t_pad, -1)[:, front:front + t_real]
        uc3 = uc.reshape(B, t_pad, -1)[:, front:front + t_real]
        vals = (uc3[:, :, 512:640].reshape(B, t_real, HKV_C, DH_C), uc3[:, :, 640:768].reshape(B, t_real, HKV_C, DH_C),
                uc3[:, :, 896:896 + D_IDX], S, C, n_o, m_o, ub3[:, t_real - (CONV_W - 1):, 0:2 * W_B])
        for acc, a in zip(p_out, vals):
            acc.append(a)
    y_prompt = hp.reshape(B, t_pad, D)[:, front + N_META:front + t_real]

    s_tot = P + Ts
    s_pad = -(-s_tot // KB) * KB

    def sample_keys(l, uc, dtype):
        u3 = uc.reshape(DB, Ts, -1)

        def cat(cache, new):
            a = jnp.concatenate([cache.astype(F32).reshape(DB, P, -1), new], axis=1)
            return jnp.pad(a, ((0, 0), (0, s_pad - s_tot), (0, 0)))

        return _key_blocks(cat(cache_k[l], u3[:, :, 512:640]), cat(cache_v[l], u3[:, :, 640:768]),
                           cat(cache_kidx[l], u3[:, :, 896:896 + D_IDX]), dtype)

    sample_stream = dict(nb=DB, nc=1, L=Ts, lo=0, hi=Ts, precise=True,
                         states=lambda l: _mixer_states_in(state_gla[l], state_mlstm_c[l], state_mlstm_n[l],
                                                           state_mlstm_m[l], state_conv[l]),
                         keys=sample_keys,
                         dsa=dict(nq=1, n_sel=min(TOPK_KEYS, s_tot // 4), adm_lo=0, adm_cap=s_tot, end_base=s_tot,
                                  end_step=0, end_half=0, out_rows=Ts, out_step=Ts))
    hs = x_sample.reshape(DB * Ts, D)
    s_out = [[] for _ in range(8)]
    for l in range(depth):
        hs, (ub, uc, S, C, n_o, m_o) = _layer(hs, sample_stream, l, params[l])
        ub3 = ub.reshape(DB, Ts, -1)
        uc3 = uc.reshape(DB, Ts, -1)
        vals = (uc3[:, :, 512:640].reshape(DB, Ts, HKV_C, DH_C), uc3[:, :, 640:768].reshape(DB, Ts, HKV_C, DH_C),
                uc3[:, :, 896:896 + D_IDX], S, C, n_o, m_o, ub3[:, Ts - (CONV_W - 1):, 0:2 * W_B])
        for acc, a in zip(s_out, vals):
            acc.append(a)
    y_sample = hs.reshape(DB, Ts, D)

    return (y_prompt, y_sample, *[jnp.stack(a) for a in p_out], *[jnp.stack(a) for a in s_out])
```

```python
import functools

import jax
import jax.numpy as jnp
from jax import lax
from jax.experimental import pallas as pl
from jax.experimental.pallas import tpu as pltpu

F32 = jnp.float32
BF16 = jnp.bfloat16
I32 = jnp.int32

CHUNK = 64
N_META = 16
H_A, DK_A, DV_A = 4, 32, 64
GATE_RANK, GATE_TAU = 16, 16.0
H_B, DH_B, CONV_W = 4, 64, 4
H_C, HKV_C, DH_C = 8, 2, 64
H_IDX, D_IDX = 4, 32
TOPK_KEYS = 256
N_EXPERTS, N_GROUPS, TOP_K, D_EXPERT = 16, 4, 2, 256
W_A, W_B, W_C = H_A * DV_A, H_B * DH_B, H_C * DH_C
LN_EPS = 1e-5

LANES = 128
SUBLANES = 8
VMEM_LIMIT = 56 * 1024 * 1024
KEY_BLOCK = 256
MOE_TILE = 768
PLANE_CHUNK = 8
NEG_BIG = -1e30
M_INIT = -1e29
INT_MIN = -2 ** 31
KEY_NEG_INF = INT_MIN + 0x7FFFFF

_SPLITS = (
    ('a_q', H_A * DK_A), ('a_k', H_A * DK_A), ('a_v', W_A), ('a_g', GATE_RANK), ('a_r', W_A),
    ('b_qk', 2 * W_B), ('b_v', W_B), ('b_i', H_B), ('b_f', H_B), ('b_o', W_B),
    ('c_q', W_C), ('c_k', HKV_C * DH_C), ('c_v', HKV_C * DH_C),
    ('c_qi', H_IDX * D_IDX), ('c_w', H_IDX), ('c_ki', D_IDX),
)
_GROUP_A = (('a_q', 128), ('a_k', 128), ('a_v', 256), ('a_r', 256), ('a_g', 16), (None, 112))
_GROUP_B = (('b_qk', 512), ('b_v', 256), ('b_o', 256), ('b_i', 4), ('b_f', 4), (None, 120))
_GROUP_C = (('c_q', 512), ('c_k', 128), ('c_v', 128), ('c_qi', 128), ('c_ki', 32), ('c_w', 4), (None, 92))
_GROUP_WIDTHS = tuple(sum(n for _, n in g) for g in (_GROUP_A, _GROUP_B, _GROUP_C))

_NN = (((1,), (0,)), ((), ()))
_NT = (((1,), (1,)), ((), ()))
_TN = (((0,), (0,)), ((), ()))


def _mm(a, b, precise, dims=_NN):
    if precise:
        return lax.dot_general(a.astype(F32), b.astype(F32), dims, precision=lax.Precision.HIGHEST,
                               preferred_element_type=F32)
    return lax.dot_general(a.astype(BF16), b.astype(BF16), dims, preferred_element_type=F32)


def _log_sigmoid(x):
    return jnp.minimum(x, 0.0) - jnp.log(1.0 + jnp.exp(-jnp.abs(x)))


def _sigmoid(x):
    return 1.0 / (1.0 + jnp.exp(-x))


def _silu(x):
    return x * _sigmoid(x)


def _ln(x, g, b):
    mu = jnp.mean(x, axis=-1, keepdims=True)
    xc = x - mu
    var = jnp.mean(xc * xc, axis=-1, keepdims=True)
    return xc * lax.rsqrt(var + LN_EPS) * g + b


def _pick_tile(n, cap):
    best = None
    for t in range(SUBLANES, min(n, cap) + 1, SUBLANES):
        if n % t == 0:
            best = t
    assert best is not None, n
    return best


def _stack_heads(x, n_heads, width):
    lane_head = lax.broadcasted_iota(I32, (1, n_heads * width), 1) // width
    return jnp.concatenate([jnp.where(lane_head == h, x, 0.0) for h in range(n_heads)], axis=0)


def _head_norm_stacked(o_st, mask_st, n_heads, L, width):
    o_st = jnp.where(mask_st, o_st, 0.0)
    mu = jnp.sum(o_st, axis=-1, keepdims=True) * (1.0 / width)
    xc = jnp.where(mask_st, o_st - mu, 0.0)
    var = jnp.sum(xc * xc, axis=-1, keepdims=True) * (1.0 / width)
    yn = xc * lax.rsqrt(var + LN_EPS)
    y = yn[0:L]
    for h in range(1, n_heads):
        y = y + yn[h * L:(h + 1) * L]
    return y


def _params(sem):
    return pltpu.CompilerParams(dimension_semantics=sem, vmem_limit_bytes=VMEM_LIMIT)


def _inproj_kernel(x_ref, g_ref, b_ref, w_ref, bias_ref, *out_refs, apply_ln, precise):
    x = x_ref[...]
    if apply_ln:
        h_ref, *u_refs = out_refs
        x = _ln(x, g_ref[...], b_ref[...])
        h_ref[...] = x
    else:
        u_refs = out_refs
    xo = x if precise else x.astype(BF16)
    off = 0
    for r, n in zip(u_refs, _GROUP_WIDTHS):
        r[...] = _mm(xo, w_ref[:, off:off + n], precise) + bias_ref[:, off:off + n]
        off += n


def _inproj(x, ln_g, ln_b, w_packed, b_packed, apply_ln, precise):
    n, d = x.shape
    tm = _pick_tile(n, 384)
    dp = w_packed.shape[1]
    row = lambda i: (i, 0)
    fix = lambda i: (0, 0)
    out_shape = [jax.ShapeDtypeStruct((n, w), F32) for w in _GROUP_WIDTHS]
    out_specs = [pl.BlockSpec((tm, w), row) for w in _GROUP_WIDTHS]
    if apply_ln:
        out_shape = [jax.ShapeDtypeStruct((n, d), F32)] + out_shape
        out_specs = [pl.BlockSpec((tm, d), row)] + out_specs
    outs = pl.pallas_call(
        functools.partial(_inproj_kernel, apply_ln=apply_ln, precise=precise),
        grid=(n // tm,),
        in_specs=[pl.BlockSpec((tm, d), row), pl.BlockSpec((1, d), fix), pl.BlockSpec((1, d), fix),
                  pl.BlockSpec((d, dp), fix), pl.BlockSpec((1, dp), fix)],
        out_specs=out_specs, out_shape=out_shape,
        compiler_params=_params(("parallel",)),
        name="inproj_ln" if apply_ln else "inproj",
    )(x, ln_g.reshape(1, d), ln_b.reshape(1, d), w_packed, b_packed)
    if apply_ln:
        return outs[0], outs[1], outs[2], outs[3]
    return x, outs[0], outs[1], outs[2]


def _gla_step(u, wg, bg, gn, st, c, *, L, lo, hi, precise):
    q = u[:, 0:128] * (DK_A ** -0.5)
    k = u[:, 128:256]
    v = u[:, 256:512]
    r = u[:, 512:768]
    g = u[:, 768:896]
    la = _log_sigmoid(_mm(g, wg, precise) + bg) * (1.0 / GATE_TAU)
    pos = c * L + lax.broadcasted_iota(I32, (L, 1), 0)
    real = (pos >= lo) & (pos < hi)
    la = jnp.where(real, la, 0.0)
    k = jnp.where(real, k, 0.0)
    ti = lax.broadcasted_iota(I32, (L, L), 0)
    si = lax.broadcasted_iota(I32, (L, L), 1)
    b = _mm((ti >= si).astype(F32), la, True)
    bmid = b[L // 2 - 1:L // 2, :]
    bl = b[L - 1:L, :]
    qs = _stack_heads(q * jnp.exp(b - bmid), H_A, DK_A)
    qbs = _stack_heads(q * jnp.exp(b), H_A, DK_A)
    kt = k * jnp.exp(bmid - b)
    kh = k * jnp.exp(bl - b)
    att = _mm(qs, kt, precise, _NT)
    tr = lax.broadcasted_iota(I32, (H_A * L, L), 0) % L
    sc = lax.broadcasted_iota(I32, (H_A * L, L), 1)
    att = jnp.where(sc <= tr, att, 0.0)
    row_head = lax.broadcasted_iota(I32, (H_A * L, W_A), 0) // L
    lane_head = lax.broadcasted_iota(I32, (H_A * L, W_A), 1) // DV_A
    mask_st = row_head == lane_head
    o_st = jnp.where(mask_st, _mm(att, v, precise), 0.0) + _mm(qbs, st, precise, _NT)
    y = _head_norm_stacked(o_st, mask_st, H_A, L, DV_A) * gn * _silu(r)
    bd = (lax.broadcasted_iota(I32, (W_A, H_A * DK_A), 0) // DV_A) == (lax.broadcasted_iota(I32, (W_A, H_A * DK_A), 1) // DK_A)
    return y, st * jnp.exp(bl) + jnp.where(bd, _mm(v, kh, precise, _TN), 0.0)


def _mlstm_step(u, cw, cb, bf, gn, cbd, nrow, m_prev, buf, c, *, L, lo, hi, precise):
    H, D = H_B, DH_B
    pos = c * L + lax.broadcasted_iota(I32, (L, 1), 0)
    real = (pos >= lo) & (pos < hi)
    buf[SUBLANES:SUBLANES + L, :] = jnp.where(real, u[:, 0:2 * W_B], 0.0)
    hist = SUBLANES - (CONV_W - 1)
    conv = buf[hist:hist + L, :] * cw[0:1, :]
    for j in range(1, CONV_W):
        conv = conv + buf[hist + j:hist + j + L, :] * cw[j:j + 1, :]
    conv = conv + cb
    buf[0:SUBLANES, :] = buf[L:L + SUBLANES, :]
    act = _silu(conv)
    q = act[:, 0:W_B]
    k = act[:, W_B:2 * W_B] * (D ** -0.5)
    v = u[:, 512:768]
    og = u[:, 768:1024]
    gts = u[:, 1024:1152]
    ig_col = jnp.where(real, gts, NEG_BIG)
    lf_col = jnp.where(real, _log_sigmoid(gts + bf), 0.0)
    ti = lax.broadcasted_iota(I32, (L, L), 0)
    si = lax.broadcasted_iota(I32, (L, L), 1)
    b_col = _mm((ti >= si).astype(F32), lf_col, True)
    lane = lax.broadcasted_iota(I32, (1, LANES), 1)
    z = jnp.where(lane < H, ig_col, b_col)
    zt = jnp.concatenate([z, jnp.zeros((LANES - L, LANES), F32)], axis=0).T
    b_st = jnp.concatenate([b_col[:, H + h:H + h + 1] for h in range(H)], axis=0)
    m0_st = jnp.concatenate([jnp.broadcast_to(m_prev[:, h:h + 1], (L, 1)) for h in range(H)], axis=0)
    rowv_st = jnp.concatenate(
        [jnp.broadcast_to(zt[h:h + 1, 0:L] - zt[H + h:H + h + 1, 0:L], (L, L)) for h in range(H)], axis=0)
    tr = lax.broadcasted_iota(I32, (H * L, L), 0) % L
    sc = lax.broadcasted_iota(I32, (H * L, L), 1)
    logw = jnp.where(sc <= tr, b_st + rowv_st, NEG_BIG)
    lp = b_st + m0_st
    m_st = jnp.maximum(lp, jnp.max(logw, axis=1, keepdims=True))
    w = jnp.exp(logw - m_st)
    sp = jnp.exp(lp - m_st)
    qs = _stack_heads(q, H, D)
    qk = _mm(qs, k, precise, _NT) * w
    row_head = lax.broadcasted_iota(I32, (H * L, W_B), 0) // L
    lane_head = lax.broadcasted_iota(I32, (H * L, W_B), 1) // D
    mask_st = row_head == lane_head
    num = jnp.where(mask_st, _mm(qk, v, precise), 0.0) + sp * _mm(qs, cbd, precise)
    den = jnp.sum(qk, axis=1, keepdims=True) + sp * jnp.sum(qs * nrow, axis=1, keepdims=True)
    hh = num / jnp.maximum(jnp.abs(den), jnp.exp(-m_st))
    y = _head_norm_stacked(hh, mask_st, H, L, D) * gn * _sigmoid(og)
    lh = lax.broadcasted_iota(I32, (1, W_B), 1) // D
    wlk = jnp.zeros((L, W_B), F32)
    sl_lanes = jnp.zeros((1, W_B), F32)
    m_new = jnp.zeros((1, LANES), F32)
    for h in range(H):
        ml = m_st[h * L + L - 1:h * L + L, :]
        bl = b_st[h * L + L - 1:h * L + L, :]
        wl = jnp.exp(bl - b_col[:, H + h:H + h + 1] + ig_col[:, h:h + 1] - ml)
        sl = jnp.exp(bl + m_prev[:, h:h + 1] - ml)
        wlk = jnp.where(lh == h, wl, wlk)
        sl_lanes = jnp.where(lh == h, sl, sl_lanes)
        m_new = jnp.where(lane == h, ml, m_new)
    kw = k * wlk
    bd = (lax.broadcasted_iota(I32, (W_B, W_B), 0) // D) == (lax.broadcasted_iota(I32, (W_B, W_B), 1) // D)
    c_new = sl_lanes * cbd + jnp.where(bd, _mm(kw, v, precise, _TN), 0.0)
    n_new = sl_lanes * nrow + jnp.sum(kw, axis=0, keepdims=True)
    return y, c_new, n_new, m_new


def _scan_kernel(ua_ref, ub_ref, wg_ref, bg_ref, gna_ref, cw_ref, cb_ref, bf_ref, gnb_ref,
                 s0_ref, c0_ref, n0_ref, m0_ref, conv0_ref,
                 ya_ref, yb_ref, sout_ref, cout_ref, nout_ref, mout_ref,
                 st_sc, c_sc, n_sc, m_sc, buf_sc, *, L, lo, hi, precise, bpg):
    c = pl.program_id(1)

    @pl.when(c == 0)
    def _():
        st_sc[...] = s0_ref[...]
        c_sc[...] = c0_ref[...]
        n_sc[...] = n0_ref[...]
        m_sc[...] = m0_ref[...]
        buf_sc[:, 0:SUBLANES, :] = conv0_ref[...]

    for b in range(bpg):
        y, st_new = _gla_step(ua_ref[b], wg_ref[...], bg_ref[...], gna_ref[...], st_sc[b], c,
                              L=L, lo=lo, hi=hi, precise=precise)
        ya_ref[b] = y
        st_sc[b] = st_new
        sout_ref[b] = st_new
    for b in range(bpg):
        y, c_new, n_new, m_new = _mlstm_step(ub_ref[b], cw_ref[...], cb_ref[...], bf_ref[...], gnb_ref[...],
                                             c_sc[b], n_sc[b], m_sc[b], buf_sc.at[b], c,
                                             L=L, lo=lo, hi=hi, precise=precise)
        yb_ref[b] = y
        c_sc[b] = c_new
        n_sc[b] = n_new
        m_sc[b] = m_new
        cout_ref[b] = c_new
        nout_ref[b] = n_new
        mout_ref[b] = m_new


def _scan(ua, ub, wg, bg, gna, cw, cb, bf, gnb, s0t, c0, n0, m0, conv0, *, nb, nc, L, lo, hi, precise):
    bpg = 2
    assert nb % bpg == 0
    rows = ua.shape[0]
    t = rows // nb
    fix2 = lambda g, c: (0, 0)
    tok = lambda g, c: (g, c, 0)
    per_g = lambda g, c: (g, 0, 0)
    kd = H_A * DK_A
    ya, yb, st_out, c_out, n_out, m_out = pl.pallas_call(
        functools.partial(_scan_kernel, L=L, lo=lo, hi=hi, precise=precise, bpg=bpg),
        grid=(nb // bpg, nc),
        in_specs=[pl.BlockSpec((bpg, L, _GROUP_WIDTHS[0]), tok), pl.BlockSpec((bpg, L, _GROUP_WIDTHS[1]), tok),
                  pl.BlockSpec((LANES, LANES), fix2), pl.BlockSpec((1, LANES), fix2), pl.BlockSpec((1, W_A), fix2),
                  pl.BlockSpec((CONV_W, 2 * W_B), fix2), pl.BlockSpec((1, 2 * W_B), fix2),
                  pl.BlockSpec((1, LANES), fix2), pl.BlockSpec((1, W_B), fix2),
                  pl.BlockSpec((bpg, W_A, kd), per_g), pl.BlockSpec((bpg, W_B, W_B), per_g),
                  pl.BlockSpec((bpg, 1, W_B), per_g), pl.BlockSpec((bpg, 1, LANES), per_g),
                  pl.BlockSpec((bpg, SUBLANES, 2 * W_B), per_g)],
        out_specs=[pl.BlockSpec((bpg, L, W_A), tok), pl.BlockSpec((bpg, L, W_B), tok),
                   pl.BlockSpec((bpg, W_A, kd), per_g), pl.BlockSpec((bpg, W_B, W_B), per_g),
                   pl.BlockSpec((bpg, 1, W_B), per_g), pl.BlockSpec((bpg, 1, LANES), per_g)],
        out_shape=[jax.ShapeDtypeStruct((nb, t, W_A), F32), jax.ShapeDtypeStruct((nb, t, W_B), F32),
                   jax.ShapeDtypeStruct((nb, W_A, kd), F32), jax.ShapeDtypeStruct((nb, W_B, W_B), F32),
                   jax.ShapeDtypeStruct((nb, 1, W_B), F32), jax.ShapeDtypeStruct((nb, 1, LANES), F32)],
        scratch_shapes=[pltpu.VMEM((bpg, W_A, kd), F32), pltpu.VMEM((bpg, W_B, W_B), F32),
                        pltpu.VMEM((bpg, 1, W_B), F32), pltpu.VMEM((bpg, 1, LANES), F32),
                        pltpu.VMEM((bpg, L + SUBLANES, 2 * W_B), F32)],
        compiler_params=_params(("parallel", "arbitrary")),
        name="mixer_scans",
    )(ua.reshape(nb, t, -1), ub.reshape(nb, t, -1), wg, bg, gna, cw, cb, bf, gnb, s0t, c0, n0, m0, conv0)
    return ya.reshape(rows, W_A), yb.reshape(rows, W_B), st_out, c_out, n_out, m_out


def _dsa_kernel(u_ref, k_ref, vt_ref, ki_ref, y_ref, sc_sc, pl_sc, act_sc, sa_sc, sb_sc, ba_sc, bb_sc, acc_sc, m_sc, l_sc, y_sc,
                *, KB, nkb_alloc, n_sel, adm_lo, adm_cap, end_base, end_step, end_half, out_rows, out_step, precise):
    b = pl.program_id(0)
    c = pl.program_id(1)
    TQ = LANES
    G = H_C // HKV_C
    lane_q = lax.broadcasted_iota(I32, (1, TQ), 1)
    end_lo = end_base + end_step * c
    end_q = jnp.minimum(jnp.where(lane_q < CHUNK, end_lo, end_lo + end_half), adm_cap)
    end_max = jnp.minimum(end_lo + end_half, adm_cap)
    npair = (end_max + 2 * KB - 1) // (2 * KB)
    u = u_ref[...]
    qi_t = (u[:, 768:896] * (D_IDX ** -0.5)).T
    kw_t = u[:, 896:1024].T
    qi_st = jnp.concatenate([qi_t[h * D_IDX:(h + 1) * D_IDX, :] for h in range(H_IDX)], axis=1)
    if not precise:
        qi_st = qi_st.astype(BF16)
    wi = [kw_t[D_IDX + h:D_IDX + h + 1, :] * (H_IDX ** -0.5) for h in range(H_IDX)]
    key_idx = lax.broadcasted_iota(I32, (KB, TQ), 0)

    def score_block(kb):
        raw = jnp.maximum(_mm(ki_ref[0, kb], qi_st, precise), 0.0)
        isc = wi[0] * raw[:, 0:TQ]
        for h in range(1, H_IDX):
            isc = isc + wi[h] * raw[:, h * TQ:(h + 1) * TQ]
        idx = kb * KB + key_idx
        sc_sc[kb] = jnp.where((idx >= adm_lo) & (idx < end_q), isc, -jnp.inf)

    def score_body(i, carry):
        score_block(2 * i)
        score_block(2 * i + 1)
        return carry

    lax.fori_loop(0, npair, score_body, 0)

    def count(pred_fn):
        def body(i, acc):
            for kb in (2 * i, 2 * i + 1):
                hit = jnp.where(pred_fn(sc_sc[kb]), 1, 0)
                acc = acc + jnp.sum(hit.reshape(KB // SUBLANES, SUBLANES, TQ), axis=0)
            return acc
        acc = lax.fori_loop(0, npair, body, jnp.zeros((SUBLANES, TQ), I32))
        return jnp.sum(acc, axis=0, keepdims=True)

    def key_to_float(key):
        return lax.bitcast_convert_type(jnp.where(key >= 0, key, key ^ 0x7FFFFFFF), F32)

    U32 = jnp.uint32
    nch = (2 * npair + PLANE_CHUNK - 1) // PLANE_CHUNK

    def plane_body(kb, carry):
        bits = lax.bitcast_convert_type(sc_sc[kb], U32)
        flip = jnp.where(bits >= U32(0x80000000), U32(0xFFFFFFFF), U32(0x80000000))
        uk = bits ^ flip
        uk = jnp.where(uk == U32(0x7FFFFFFF), U32(0x80000000), uk)
        a = [uk[SUBLANES * j:SUBLANES * (j + 1), :] for j in range(32)]
        j, m = 16, 0x0000FFFF
        while j:
            k = 0
            while k < 32:
                t = (a[k] ^ (a[k + j] >> U32(j))) & U32(m)
                a[k] = a[k] ^ t
                a[k + j] = a[k + j] ^ (t << U32(j))
                k = (k + j + 1) & ~j
            j >>= 1
            m = (m ^ (m << j)) & 0xFFFFFFFF
        for p in range(32):
            pl_sc[p, kb] = a[p]
        return carry

    lax.fori_loop(0, 2 * npair, plane_body, 0)

    def plane_pad_body(kb, carry):
        for p in range(32):
            pl_sc[p, kb] = jnp.zeros((SUBLANES, TQ), U32)
        return carry

    lax.fori_loop(2 * npair, nch * PLANE_CHUNK, plane_pad_body, 0)

    def chunk(i):
        return pl.ds(pl.multiple_of(i * PLANE_CHUNK, PLANE_CHUNK), PLANE_CHUNK)

    def popsum(words):
        return jnp.sum(lax.population_count(words).astype(I32), axis=0)

    def first_body(i, acc):
        act_sc[chunk(i)] = jnp.full((PLANE_CHUNK, SUBLANES, TQ), 0xFFFFFFFF, U32)
        return acc + popsum(pl_sc[0, chunk(i)])

    cnt1 = jnp.sum(lax.fori_loop(0, nch, first_body, jnp.zeros((SUBLANES, TQ), I32)), axis=0, keepdims=True)

    def radix_body(p, carry):
        rem, ukey, c1 = carry
        take = c1 >= rem

        def body(i, acc):
            act = act_sc[chunk(i)]
            prev = pl_sc[p - 1, chunk(i)]
            act = jnp.where(take, act & prev, act & ~prev)
            act_sc[chunk(i)] = act
            return acc + popsum(act & pl_sc[p, chunk(i)])

        acc = lax.fori_loop(0, nch, body, jnp.zeros((SUBLANES, TQ), I32))
        bit = lax.shift_left(U32(1), (32 - p).astype(U32))
        return (jnp.where(take, rem, rem - c1), jnp.where(take, ukey | bit, ukey),
                jnp.sum(acc, axis=0, keepdims=True))

    rem, ukey, c1 = lax.fori_loop(1, 32, radix_body,
                                  (jnp.full((1, TQ), n_sel, I32), jnp.zeros((1, TQ), U32), cnt1))
    ukey = jnp.where(c1 >= rem, ukey | U32(1), ukey)
    thr_fast = key_to_float(lax.bitcast_convert_type(ukey ^ U32(0x80000000), I32))
    c_ge = count(lambda s: s >= thr_fast)
    c_gt = count(lambda s: s > thr_fast)
    ok = jnp.min(jnp.where((c_gt < n_sel) & (c_ge >= n_sel), 1, 0)) == 1

    def float_search():
        def thr_body(i, p):
            cand = p + lax.shift_left(jnp.int32(1), 31 - i)
            cand_f = key_to_float(cand)
            cnt = count(lambda s: s >= cand_f)
            return jnp.where(cnt >= n_sel, cand, p)

        thr_key = lax.fori_loop(0, 32, thr_body, jnp.full((1, TQ), INT_MIN, I32))
        t = key_to_float(jnp.maximum(thr_key, KEY_NEG_INF))
        return t, count(lambda s: s > t)

    thr, n_above = lax.cond(ok, lambda: (thr_fast, c_gt), float_search)
    need = (n_sel - n_above).astype(F32)

    q_t = [(u[:, j * LANES:(j + 1) * LANES] * (DH_C ** -0.5)).T for j in range(W_C // LANES)]
    zero_half = jnp.zeros((DH_C, G * TQ), F32)
    qz = []
    for n in range(HKV_C):
        heads = [n * G + g for g in range(G)]
        qn = jnp.concatenate([q_t[hc // 2][(hc % 2) * DH_C:(hc % 2 + 1) * DH_C, :] for hc in heads], axis=1)
        qn = jnp.concatenate([qn, zero_half] if n == 0 else [zero_half, qn], axis=0)
        qz.append(qn if precise else qn.astype(BF16))
    tri = (lax.broadcasted_iota(I32, (KB, KB), 0) >= lax.broadcasted_iota(I32, (KB, KB), 1)).astype(BF16)

    def mask_block(kb, tie_seen, dst):
        s_idx = sc_sc[kb]
        eq = (s_idx == thr) & (s_idx > -jnp.inf)
        rank = tie_seen + jnp.dot(tri, jnp.where(eq, 1.0, 0.0).astype(BF16), preferred_element_type=F32)
        sel = (s_idx > thr) | (eq & (rank <= need))
        dst[...] = jnp.where(sel, 0.0, NEG_BIG)
        return rank[KB - 1:KB, :]

    acc_sc[...] = jnp.zeros_like(acc_sc)
    m_sc[...] = jnp.full_like(m_sc, M_INIT)
    l_sc[...] = jnp.zeros_like(l_sc)
    W2 = 2 * TQ
    units = [(n, slice(gp * W2, (gp + 1) * W2)) for n in range(HKV_C) for gp in range(G // 2)]

    def logits(kb, dst):
        kblk = k_ref[0, kb]
        for ui, (n, cols) in enumerate(units):
            dst[ui] = _mm(kblk, qz[n][:, cols], precise)

    def att_step(kb, tie_seen, cur, nxt, bias_cur, bias_nxt):
        logits(jnp.minimum(kb + 1, nkb_alloc - 1), nxt)
        tie_seen = mask_block(jnp.minimum(kb + 1, 2 * npair - 1), tie_seen, bias_nxt)
        bias = bias_cur[...]
        bias2 = jnp.concatenate([bias, bias], axis=1)
        vtblk = vt_ref[0, kb]
        for ui, (n, cols) in enumerate(units):
            s = cur[ui] + bias2
            m_old = m_sc[n, :, cols]
            m_new = jnp.maximum(m_old, jnp.max(s, axis=0, keepdims=True))
            p = jnp.exp(s - m_new)
            alpha = jnp.exp(m_old - m_new)
            l_sc[n, :, cols] = alpha * l_sc[n, :, cols] + jnp.sum(p, axis=0, keepdims=True)
            acc_sc[n, :, cols] = alpha * acc_sc[n, :, cols] + _mm(vtblk[n * DH_C:(n + 1) * DH_C, :], p, precise)
            m_sc[n, :, cols] = m_new
        return tie_seen

    logits(0, sa_sc)
    tie0 = mask_block(0, jnp.zeros((1, TQ), F32), ba_sc)

    def att_body(i, tie_seen):
        tie_seen = att_step(2 * i, tie_seen, sa_sc, sb_sc, ba_sc, bb_sc)
        return att_step(2 * i + 1, tie_seen, sb_sc, sa_sc, bb_sc, ba_sc)

    lax.fori_loop(0, npair, att_body, tie0)
    o = [acc_sc[n] / l_sc[n] for n in range(HKV_C)]
    for j in range(W_C // LANES):
        n, g0 = (2 * j) // G, (2 * j) % G
        pair = jnp.concatenate([o[n][:, g0 * TQ:(g0 + 1) * TQ], o[n][:, (g0 + 1) * TQ:(g0 + 2) * TQ]], axis=0)
        y_sc[:, j * LANES:(j + 1) * LANES] = pair.T
    off = pl.multiple_of(b * out_step, SUBLANES)
    y_ref[...] = y_sc[pl.ds(off, out_rows), :]


def _dsa(uc, k_arr, vt_arr, ki_arr, *, nb, nq, n_sel, adm_lo, adm_cap, end_base, end_step, end_half,
         out_rows, out_step, precise):
    rows = uc.shape[0]
    nkb_max, KB = k_arr.shape[1], k_arr.shape[2]
    assert nkb_max % 2 == 0
    n_plane = -(-nkb_max // PLANE_CHUNK) * PLANE_CHUNK
    G = H_C // HKV_C
    per_b = lambda b, c: (b, 0, 0, 0)
    u_map = (lambda b, c: (b * nq + c, 0)) if out_step == 0 else (lambda b, c: (0, 0))
    return pl.pallas_call(
        functools.partial(_dsa_kernel, KB=KB, nkb_alloc=nkb_max, n_sel=n_sel, adm_lo=adm_lo, adm_cap=adm_cap,
                          end_base=end_base, end_step=end_step, end_half=end_half, out_rows=out_rows,
                          out_step=out_step, precise=precise),
        grid=(nb, nq),
        in_specs=[pl.BlockSpec((LANES, _GROUP_WIDTHS[2]), u_map),
                  pl.BlockSpec((1, nkb_max, KB, HKV_C * DH_C), per_b),
                  pl.BlockSpec((1, nkb_max, HKV_C * DH_C, KB), per_b),
                  pl.BlockSpec((1, nkb_max, KB, D_IDX), per_b)],
        out_specs=pl.BlockSpec((out_rows, W_C), lambda b, c: (b * nq + c, 0)),
        out_shape=jax.ShapeDtypeStruct((rows, W_C), F32),
        scratch_shapes=[pltpu.VMEM((nkb_max, KB, LANES), F32),
                        pltpu.VMEM((32, n_plane, SUBLANES, LANES), jnp.uint32),
                        pltpu.VMEM((n_plane, SUBLANES, LANES), jnp.uint32),
                        pltpu.VMEM((HKV_C * (G // 2), KB, 2 * LANES), F32),
                        pltpu.VMEM((HKV_C * (G // 2), KB, 2 * LANES), F32),
                        pltpu.VMEM((KB, LANES), F32), pltpu.VMEM((KB, LANES), F32),
                        pltpu.VMEM((HKV_C, DH_C, G * LANES), F32), pltpu.VMEM((HKV_C, 1, G * LANES), F32),
                        pltpu.VMEM((HKV_C, 1, G * LANES), F32), pltpu.VMEM((LANES, W_C), F32)],
        compiler_params=_params(("parallel", "arbitrary")),
        name="dsa_attend",
    )(uc, k_arr, vt_arr, ki_arr)


def _outproj_kernel(ya_ref, yb_ref, yc_ref, h_ref, w_ref, g_ref, b_ref, o_ref, *, alpha, precise):
    mix = (_mm(ya_ref[...], w_ref[0:W_A, :], precise) + _mm(yb_ref[...], w_ref[W_A:W_A + W_B, :], precise)
           + _mm(yc_ref[...], w_ref[W_A + W_B:, :], precise))
    o_ref[...] = _ln(alpha * h_ref[...] + mix, g_ref[...], b_ref[...])


def _outproj(ya, yb, yc, h, w_out_l, g, b, alpha, precise):
    n, d = h.shape
    tm = _pick_tile(n, 768)
    row = lambda i: (i, 0)
    fix = lambda i: (0, 0)
    return pl.pallas_call(
        functools.partial(_outproj_kernel, alpha=alpha, precise=precise),
        grid=(n // tm,),
        in_specs=[pl.BlockSpec((tm, W_A), row), pl.BlockSpec((tm, W_B), row), pl.BlockSpec((tm, W_C), row),
                  pl.BlockSpec((tm, d), row), pl.BlockSpec(w_out_l.shape, fix),
                  pl.BlockSpec((1, d), fix), pl.BlockSpec((1, d), fix)],
        out_specs=pl.BlockSpec((tm, d), row),
        out_shape=jax.ShapeDtypeStruct((n, d), F32),
        compiler_params=_params(("parallel",)),
        name="outproj_ln1",
    )(ya, yb, yc, h, w_out_l, g.reshape(1, d), b.reshape(1, d))


def _moe_kernel(h_ref, wr_ref, br_ref, wg_ref, wu_ref, wd_ref, g_ref, b_ref, o_ref, acc_sc, gate_sc, *, alpha, precise):
    grp = pl.program_id(1)
    x = h_ref[...]
    tm = x.shape[0]
    xo = x if precise else x.astype(BF16)
    eg = N_EXPERTS // N_GROUPS

    @pl.when(grp == 0)
    def _():
        s = _sigmoid(_mm(wr_ref[...], xo, precise, _NT))
        sb = s + br_ref[...]
        srow = [s[j:j + 1, :] for j in range(N_EXPERTS)]
        brow = [sb[j:j + 1, :] for j in range(N_EXPERTS)]
        gscore = []
        for g in range(N_GROUPS):
            best = None
            for i in range(eg):
                for j in range(i + 1, eg):
                    pair = brow[g * eg + i] + brow[g * eg + j]
                    best = pair if best is None else jnp.maximum(best, pair)
            gscore.append(best)
        gmax = functools.reduce(jnp.maximum, gscore)
        taken = jnp.zeros_like(gmax) > 1.0
        sel = [None] * N_EXPERTS
        for g in range(N_GROUPS):
            is_g = jnp.logical_and(jnp.logical_not(taken), gscore[g] == gmax)
            taken = jnp.logical_or(taken, is_g)
            for i in range(eg):
                a = brow[g * eg + i]
                rank = jnp.zeros(a.shape, I32)
                for j in range(eg):
                    if j == i:
                        continue
                    o = brow[g * eg + j]
                    ahead = (o >= a) if j < i else (o > a)
                    rank = rank + jnp.where(ahead, 1, 0)
                sel[g * eg + i] = jnp.logical_and(is_g, rank < TOP_K)
        denom = jnp.zeros_like(gmax)
        for j in range(N_EXPERTS):
            denom = denom + jnp.where(sel[j], srow[j], 0.0)
        gates = [jnp.where(sel[j], srow[j] / denom, 0.0) for j in range(N_EXPERTS)]
        gt = jnp.concatenate(gates + [jnp.zeros((LANES - N_EXPERTS, tm), F32)], axis=0)
        gate_sc[...] = gt.T
        acc_sc[...] = jnp.zeros_like(acc_sc)

    lane = lax.broadcasted_iota(I32, (1, LANES), 1)
    gate = gate_sc[...]
    hes = []
    for j in range(eg):
        gcol = jnp.sum(jnp.where(lane == grp * eg + j, gate, 0.0), axis=1, keepdims=True)
        hes.append(gcol * (_silu(_mm(xo, wg_ref[j], precise)) * _mm(xo, wu_ref[j], precise)))
    acc_sc[...] += _mm(jnp.concatenate(hes, axis=1), wd_ref[0], precise)

    @pl.when(grp == N_GROUPS - 1)
    def _():
        o_ref[...] = _ln(alpha * x + acc_sc[...], g_ref[...], b_ref[...])


def _moe(h, wr_t, br_col, w_gate, w_up, w_down, g, b, alpha, precise):
    n, d = h.shape
    tm = _pick_tile(n, MOE_TILE)
    assert tm % LANES == 0, tm
    eg = N_EXPERTS // N_GROUPS
    row = lambda i, e: (i, 0)
    fix = lambda i, e: (0, 0)
    per_g = lambda i, e: (e, 0, 0)
    return pl.pallas_call(
        functools.partial(_moe_kernel, alpha=alpha, precise=precise),
        grid=(n // tm, N_GROUPS),
        in_specs=[pl.BlockSpec((tm, d), row), pl.BlockSpec((LANES, d), fix), pl.BlockSpec((LANES, 1), fix),
                  pl.BlockSpec((eg, d, D_EXPERT), per_g), pl.BlockSpec((eg, d, D_EXPERT), per_g),
                  pl.BlockSpec((1, eg * D_EXPERT, d), per_g), pl.BlockSpec((1, d), fix), pl.BlockSpec((1, d), fix)],
        out_specs=pl.BlockSpec((tm, d), row),
        out_shape=jax.ShapeDtypeStruct((n, d), F32),
        scratch_shapes=[pltpu.VMEM((tm, d), F32), pltpu.VMEM((tm, LANES), F32)],
        compiler_params=_params(("parallel", "arbitrary")),
        name="moe_ln2",
    )(h, wr_t, br_col, w_gate, w_up, w_down.reshape(N_GROUPS, eg * D_EXPERT, d), g.reshape(1, d), b.reshape(1, d))


def _pack_in_weights(w_in_l, b_in_l):
    offs = {}
    off = 0
    for name, n in _SPLITS:
        offs[name] = (off, n)
        off += n
    wcols, bcols = [], []
    for grp in (_GROUP_A, _GROUP_B, _GROUP_C):
        for name, n in grp:
            if name is None:
                wcols.append(jnp.zeros((w_in_l.shape[0], n), w_in_l.dtype))
                bcols.append(jnp.zeros((n,), b_in_l.dtype))
            else:
                o, m = offs[name]
                assert m == n
                wcols.append(w_in_l[:, o:o + n])
                bcols.append(b_in_l[o:o + n])
    return jnp.concatenate(wcols, axis=1), jnp.concatenate(bcols).reshape(1, -1).astype(F32)


def _block_diag_embed(x):
    B, H, r, c = x.shape
    eye = jnp.eye(H, dtype=x.dtype)
    return (x[:, :, :, None, :] * eye[None, :, None, :, None]).reshape(B, H * r, H * c)


def _block_diag_extract(x, H):
    B, R, C = x.shape
    r, c = R // H, C // H
    x5 = x.reshape(B, H, r, H, c)
    return jnp.stack([x5[:, h, :, h, :] for h in range(H)], axis=1)


def _mixer_states_in(s_gla, s_c, s_n, s_m, s_conv):
    nb = s_gla.shape[0]
    s0t = _block_diag_embed(jnp.swapaxes(s_gla.astype(F32), -1, -2))
    c0 = _block_diag_embed(jnp.swapaxes(s_c.astype(F32), -1, -2))
    n0 = s_n.astype(F32).reshape(nb, 1, W_B)
    m0 = jnp.pad(s_m.astype(F32), ((0, 0), (0, LANES - H_B))).reshape(nb, 1, LANES)
    conv0 = jnp.pad(s_conv.astype(F32), ((0, 0), (SUBLANES - (CONV_W - 1), 0), (0, 0)))
    return s0t, c0, n0, m0, conv0


def _key_blocks(k, v, ki, dtype):
    nb, s, _ = k.shape
    s_even = -(-s // (2 * KEY_BLOCK)) * 2 * KEY_BLOCK
    k, v, ki = (jnp.pad(a, ((0, 0), (0, s_even - s), (0, 0))) for a in (k, v, ki))
    nkb = s_even // KEY_BLOCK
    kk = k.astype(dtype).reshape(nb, nkb, KEY_BLOCK, -1)
    vt = jnp.swapaxes(v.astype(dtype).reshape(nb, nkb, KEY_BLOCK, -1), -1, -2)
    kib = ki.astype(dtype).reshape(nb, nkb, KEY_BLOCK, -1)
    return kk, vt, kib


def _layer(h, stream, l, p):
    precise = stream['precise']
    wdt = F32 if precise else BF16
    (w_in_p, b_in_p, wg, bg, gn_a, cw, cb, bf, gn_b, w_out_l, ln1_g, ln1_b, wr_t, br_col,
     w_gate, w_up, w_down, ln2_g, ln2_b, ln_in_g, ln_in_b, alpha) = p
    h, ua, ub, uc = _inproj(h, ln_in_g, ln_in_b, w_in_p.astype(wdt), b_in_p, apply_ln=(l == 0), precise=precise)
    nb, nc, L, lo, hi = stream['nb'], stream['nc'], stream['L'], stream['lo'], stream['hi']
    s0t, c0, n0, m0, conv0 = stream['states'](l)
    ya, yb, st_out, c_out, n_out, m_out = _scan(ua, ub, wg.astype(wdt), bg, gn_a, cw, cb, bf, gn_b, s0t, c0, n0, m0,
                                                conv0, nb=nb, nc=nc, L=L, lo=lo, hi=hi, precise=precise)
    k_arr, vt_arr, ki_arr = stream['keys'](l, uc, wdt)
    yc = _dsa(uc, k_arr, vt_arr, ki_arr, nb=nb, precise=precise, **stream['dsa'])
    h1 = _outproj(ya, yb, yc, h, w_out_l.astype(wdt), ln1_g, ln1_b, alpha, precise)
    h2 = _moe(h1, wr_t.astype(wdt), br_col, w_gate.astype(wdt), w_up.astype(wdt), w_down.astype(wdt), ln2_g, ln2_b,
              alpha, precise)
    S = jnp.swapaxes(_block_diag_extract(st_out, H_A), -1, -2)
    C = jnp.swapaxes(_block_diag_extract(c_out, H_B), -1, -2)
    return h2, (ub, uc, S, C, n_out.reshape(nb, H_B, DH_B), m_out[:, 0, :H_B])


def kernel(x_prompt, x_sample, cache_k, cache_v, cache_kidx, state_gla, state_mlstm_c, state_mlstm_n, state_mlstm_m, state_conv, meta_tokens, ln_in_g, ln_in_b, w_in, b_in, w_gla_gate, b_gla_gate, g_gla_norm, conv_w, conv_b, b_forget, g_mlstm_norm, w_out, ln1_g, ln1_b, w_router, b_router, w_gate, w_up, w_down, ln2_g, ln2_b):
    depth = w_in.shape[0]
    alpha = float((2 * depth) ** 0.25)
    B, S_p, D = x_prompt.shape
    DB, Ts, _ = x_sample.shape
    P = cache_k.shape[2]
    assert S_p % CHUNK == 0 and CHUNK % Ts == 0 and P % CHUNK == 0 and Ts >= CONV_W - 1
    assert DB * Ts == LANES and Ts % SUBLANES == 0
    KB = KEY_BLOCK

    wr_t = jnp.pad(w_router.T, ((0, LANES - N_EXPERTS), (0, 0)))
    br_col = jnp.pad(b_router.astype(F32), (0, LANES - N_EXPERTS)).reshape(LANES, 1)

    def layer_params(l):
        w_in_p, b_in_p = _pack_in_weights(w_in[l], b_in[l])
        wg = jnp.pad(w_gla_gate[l], ((0, LANES - GATE_RANK), (0, 0)))
        bf = jnp.pad(b_forget[l].astype(F32), (H_B, LANES - 2 * H_B)).reshape(1, LANES)
        return (w_in_p, b_in_p, wg, b_gla_gate[l].reshape(1, -1), g_gla_norm[l].reshape(1, -1),
                conv_w[l], conv_b[l].reshape(1, -1), bf, g_mlstm_norm[l].reshape(1, -1), w_out[l],
                ln1_g[l], ln1_b[l], wr_t, br_col, w_gate[l], w_up[l], w_down[l], ln2_g[l], ln2_b[l],
                ln_in_g, ln_in_b, alpha)

    params = [layer_params(l) for l in range(depth)]

    front = CHUNK - N_META
    t_real = N_META + S_p
    t_pad = -(-(front + t_real) // KB) * KB
    hp = jnp.concatenate([jnp.zeros((B, front, D), x_prompt.dtype),
                          jnp.broadcast_to(meta_tokens.astype(x_prompt.dtype)[None], (B, N_META, D)), x_prompt,
                          jnp.zeros((B, t_pad - front - t_real, D), x_prompt.dtype)], axis=1).reshape(B * t_pad, D)

    def prompt_keys(l, uc, dtype):
        u3 = uc.reshape(B, t_pad, -1)
        return _key_blocks(u3[:, :, 512:640], u3[:, :, 640:768], u3[:, :, 896:896 + D_IDX], dtype)

    zero_states = _mixer_states_in(jnp.zeros((B, H_A, DK_A, DV_A), F32), jnp.zeros((B, H_B, DH_B, DH_B), F32),
                                   jnp.zeros((B, H_B, DH_B), F32), jnp.zeros((B, H_B), F32),
                                   jnp.zeros((B, CONV_W - 1, 2 * W_B), F32))
    prompt_stream = dict(nb=B, nc=t_pad // CHUNK, L=CHUNK, lo=front, hi=front + t_real, precise=False,
                         states=lambda l: zero_states, keys=prompt_keys,
                         dsa=dict(nq=t_pad // LANES, n_sel=min(TOPK_KEYS, S_p // 4), adm_lo=front,
                                  adm_cap=front + t_real, end_base=CHUNK, end_step=2 * CHUNK, end_half=CHUNK,
                                  out_rows=LANES, out_step=0))
    p_out = [[] for _ in range(8)]
    for l in range(depth):
        hp, (ub, uc, S, C, n_o, m_o) = _layer(hp, prompt_stream, l, params[l])
        ub3 = ub.reshape(B, t_pad, -1)[:, front:front + t_real]
        uc3 = uc.reshape(B, t_pad, -1)[:, front:front + t_real]
        vals = (uc3[:, :, 512:640].reshape(B, t_real, HKV_C, DH_C), uc3[:, :, 640:768].reshape(B, t_real, HKV_C, DH_C),
                uc3[:, :, 896:896 + D_IDX], S, C, n_o, m_o, ub3[:, t_real - (CONV_W - 1):, 0:2 * W_B])
        for acc, a in zip(p_out, vals):
            acc.append(a)
    y_prompt = hp.reshape(B, t_pad, D)[:, front + N_META:front + t_real]

    s_tot = P + Ts
    s_pad = -(-s_tot // KB) * KB

    def sample_keys(l, uc, dtype):
        u3 = uc.reshape(DB, Ts, -1)

        def cat(cache, new):
            a = jnp.concatenate([cache.astype(F32).reshape(DB, P, -1), new], axis=1)
            return jnp.pad(a, ((0, 0), (0, s_pad - s_tot), (0, 0)))

        return _key_blocks(cat(cache_k[l], u3[:, :, 512:640]), cat(cache_v[l], u3[:, :, 640:768]),
                           cat(cache_kidx[l], u3[:, :, 896:896 + D_IDX]), dtype)

    sample_stream = dict(nb=DB, nc=1, L=Ts, lo=0, hi=Ts, precise=True,
                         states=lambda l: _mixer_states_in(state_gla[l], state_mlstm_c[l], state_mlstm_n[l],
                                                           state_mlstm_m[l], state_conv[l]),
                         keys=sample_keys,
                         dsa=dict(nq=1, n_sel=min(TOPK_KEYS, s_tot // 4), adm_lo=0, adm_cap=s_tot, end_base=s_tot,
                                  end_step=0, end_half=0, out_rows=Ts, out_step=Ts))
    hs = x_sample.reshape(DB * Ts, D)
    s_out = [[] for _ in range(8)]
    for l in range(depth):
        hs, (ub, uc, S, C, n_o, m_o) = _layer(hs, sample_stream, l, params[l])
        ub3 = ub.reshape(DB, Ts, -1)
        uc3 = uc.reshape(DB, Ts, -1)
        vals = (uc3[:, :, 512:640].reshape(DB, Ts, HKV_C, DH_C), uc3[:, :, 640:768].reshape(DB, Ts, HKV_C, DH_C),
                uc3[:, :, 896:896 + D_IDX], S, C, n_o, m_o, ub3[:, Ts - (CONV_W - 1):, 0:2 * W_B])
        for acc, a in zip(s_out, vals):
            acc.append(a)
    y_sample = hs.reshape(DB, Ts, D)

    return (y_prompt, y_sample, *[jnp.stack(a) for a in p_out], *[jnp.stack(a) for a in s_out])
```

```python
import functools

import jax
import jax.numpy as jnp
from jax import lax
from jax.experimental import pallas as pl
from jax.experimental.pallas import tpu as pltpu

F32 = jnp.float32
BF16 = jnp.bfloat16
I32 = jnp.int32

CHUNK = 64
N_META = 16
H_A, DK_A, DV_A = 4, 32, 64
GATE_RANK, GATE_TAU = 16, 16.0
H_B, DH_B, CONV_W = 4, 64, 4
H_C, HKV_C, DH_C = 8, 2, 64
H_IDX, D_IDX = 4, 32
TOPK_KEYS = 256
N_EXPERTS, N_GROUPS, TOP_K, D_EXPERT = 16, 4, 2, 256
W_A, W_B, W_C = H_A * DV_A, H_B * DH_B, H_C * DH_C
LN_EPS = 1e-5

LANES = 128
SUBLANES = 8
VMEM_LIMIT = 56 * 1024 * 1024
KEY_BLOCK = 256
MOE_TILE = 768
PLANE_CHUNK = 8
SCAN_CHUNK = 128
NEG_BIG = -1e30
M_INIT = -1e29
INT_MIN = -2 ** 31
LOG2_E = 1.4426950408889634
KEY_NEG_INF = INT_MIN + 0x7FFFFF

_SPLITS = (
    ('a_q', H_A * DK_A), ('a_k', H_A * DK_A), ('a_v', W_A), ('a_g', GATE_RANK), ('a_r', W_A),
    ('b_qk', 2 * W_B), ('b_v', W_B), ('b_i', H_B), ('b_f', H_B), ('b_o', W_B),
    ('c_q', W_C), ('c_k', HKV_C * DH_C), ('c_v', HKV_C * DH_C),
    ('c_qi', H_IDX * D_IDX), ('c_w', H_IDX), ('c_ki', D_IDX),
)
_GROUP_A = (('a_q', 128), ('a_k', 128), ('a_v', 256), ('a_r', 256), ('a_g', 16), (None, 112))
_GROUP_B = (('b_qk', 512), ('b_v', 256), ('b_o', 256), ('b_i', 4), ('b_f', 4), (None, 120))
_GROUP_C = (('c_q', 512), ('c_k', 128), ('c_v', 128), ('c_qi', 128), ('c_ki', 32), ('c_w', 4), (None, 92))
_GROUP_WIDTHS = tuple(sum(n for _, n in g) for g in (_GROUP_A, _GROUP_B, _GROUP_C))

_NN = (((1,), (0,)), ((), ()))
_NT = (((1,), (1,)), ((), ()))
_TN = (((0,), (0,)), ((), ()))


def _mm(a, b, precise, dims=_NN):
    if precise:
        return lax.dot_general(a.astype(F32), b.astype(F32), dims, precision=lax.Precision.HIGHEST,
                               preferred_element_type=F32)
    return lax.dot_general(a.astype(BF16), b.astype(BF16), dims, preferred_element_type=F32)


def _log_sigmoid(x):
    return jnp.minimum(x, 0.0) - jnp.log(1.0 + jnp.exp(-jnp.abs(x)))


def _sigmoid(x):
    return 1.0 / (1.0 + jnp.exp(-x))


def _silu(x):
    return x * _sigmoid(x)


def _ln(x, g, b):
    mu = jnp.mean(x, axis=-1, keepdims=True)
    xc = x - mu
    var = jnp.mean(xc * xc, axis=-1, keepdims=True)
    return xc * lax.rsqrt(var + LN_EPS) * g + b


def _pick_tile(n, cap):
    best = None
    for t in range(SUBLANES, min(n, cap) + 1, SUBLANES):
        if n % t == 0:
            best = t
    assert best is not None, n
    return best


def _stack_heads(x, n_heads, width):
    lane_head = lax.broadcasted_iota(I32, (1, n_heads * width), 1) // width
    return jnp.concatenate([jnp.where(lane_head == h, x, 0.0) for h in range(n_heads)], axis=0)


def _head_norm_stacked(o_st, mask_st, n_heads, L, width):
    o_st = jnp.where(mask_st, o_st, 0.0)
    mu = jnp.sum(o_st, axis=-1, keepdims=True) * (1.0 / width)
    xc = jnp.where(mask_st, o_st - mu, 0.0)
    var = jnp.sum(xc * xc, axis=-1, keepdims=True) * (1.0 / width)
    yn = xc * lax.rsqrt(var + LN_EPS)
    y = yn[0:L]
    for h in range(1, n_heads):
        y = y + yn[h * L:(h + 1) * L]
    return y


def _params(sem):
    return pltpu.CompilerParams(dimension_semantics=sem, vmem_limit_bytes=VMEM_LIMIT)


def _inproj_kernel(x_ref, g_ref, b_ref, w_ref, bias_ref, *out_refs, apply_ln, precise):
    x = x_ref[...]
    if apply_ln:
        h_ref, *u_refs = out_refs
        x = _ln(x, g_ref[...], b_ref[...])
        h_ref[...] = x
    else:
        u_refs = out_refs
    xo = x if precise else x.astype(BF16)
    off = 0
    for r, n in zip(u_refs, _GROUP_WIDTHS):
        r[...] = _mm(xo, w_ref[:, off:off + n], precise) + bias_ref[:, off:off + n]
        off += n


def _inproj(x, ln_g, ln_b, w_packed, b_packed, apply_ln, precise):
    n, d = x.shape
    tm = _pick_tile(n, 384)
    dp = w_packed.shape[1]
    row = lambda i: (i, 0)
    fix = lambda i: (0, 0)
    out_shape = [jax.ShapeDtypeStruct((n, w), F32) for w in _GROUP_WIDTHS]
    out_specs = [pl.BlockSpec((tm, w), row) for w in _GROUP_WIDTHS]
    if apply_ln:
        out_shape = [jax.ShapeDtypeStruct((n, d), F32)] + out_shape
        out_specs = [pl.BlockSpec((tm, d), row)] + out_specs
    outs = pl.pallas_call(
        functools.partial(_inproj_kernel, apply_ln=apply_ln, precise=precise),
        grid=(n // tm,),
        in_specs=[pl.BlockSpec((tm, d), row), pl.BlockSpec((1, d), fix), pl.BlockSpec((1, d), fix),
                  pl.BlockSpec((d, dp), fix), pl.BlockSpec((1, dp), fix)],
        out_specs=out_specs, out_shape=out_shape,
        compiler_params=_params(("parallel",)),
        name="inproj_ln" if apply_ln else "inproj",
    )(x, ln_g.reshape(1, d), ln_b.reshape(1, d), w_packed, b_packed)
    if apply_ln:
        return outs[0], outs[1], outs[2], outs[3]
    return x, outs[0], outs[1], outs[2]


def _gla_step(u, wg, bg, gn, st, c, *, L, lo, hi, precise):
    q = u[:, 0:128] * (DK_A ** -0.5)
    k = u[:, 128:256]
    v = u[:, 256:512]
    r = u[:, 512:768]
    g = u[:, 768:896]
    la = _log_sigmoid(_mm(g, wg, precise) + bg) * (1.0 / GATE_TAU)
    pos = c * L + lax.broadcasted_iota(I32, (L, 1), 0)
    real = (pos >= lo) & (pos < hi)
    la = jnp.where(real, la, 0.0)
    k = jnp.where(real, k, 0.0)
    ti = lax.broadcasted_iota(I32, (L, L), 0)
    si = lax.broadcasted_iota(I32, (L, L), 1)
    b = _mm((ti >= si).astype(F32), la, True)
    bmid = b[L // 2 - 1:L // 2, :]
    bl = b[L - 1:L, :]
    qs = _stack_heads(q * jnp.exp(b - bmid), H_A, DK_A)
    qbs = _stack_heads(q * jnp.exp(b), H_A, DK_A)
    kt = k * jnp.exp(bmid - b)
    kh = k * jnp.exp(bl - b)
    att = _mm(qs, kt, precise, _NT)
    tr = lax.broadcasted_iota(I32, (H_A * L, L), 0) % L
    sc = lax.broadcasted_iota(I32, (H_A * L, L), 1)
    att = jnp.where(sc <= tr, att, 0.0)
    row_head = lax.broadcasted_iota(I32, (H_A * L, W_A), 0) // L
    lane_head = lax.broadcasted_iota(I32, (H_A * L, W_A), 1) // DV_A
    mask_st = row_head == lane_head
    o_st = jnp.where(mask_st, _mm(att, v, precise), 0.0) + _mm(qbs, st, precise, _NT)
    y = _head_norm_stacked(o_st, mask_st, H_A, L, DV_A) * gn * _silu(r)
    bd = (lax.broadcasted_iota(I32, (W_A, H_A * DK_A), 0) // DV_A) == (lax.broadcasted_iota(I32, (W_A, H_A * DK_A), 1) // DK_A)
    return y, st * jnp.exp(bl) + jnp.where(bd, _mm(v, kh, precise, _TN), 0.0)


def _mlstm_step(u, cw, cb, bf, gn, cbd, nrow, m_prev, buf, c, *, L, lo, hi, precise):
    H, D = H_B, DH_B
    pos = c * L + lax.broadcasted_iota(I32, (L, 1), 0)
    real = (pos >= lo) & (pos < hi)
    buf[SUBLANES:SUBLANES + L, :] = jnp.where(real, u[:, 0:2 * W_B], 0.0)
    hist = SUBLANES - (CONV_W - 1)
    conv = buf[hist:hist + L, :] * cw[0:1, :]
    for j in range(1, CONV_W):
        conv = conv + buf[hist + j:hist + j + L, :] * cw[j:j + 1, :]
    conv = conv + cb
    buf[0:SUBLANES, :] = buf[L:L + SUBLANES, :]
    act = _silu(conv)
    q = act[:, 0:W_B]
    k = act[:, W_B:2 * W_B] * (D ** -0.5)
    v = u[:, 512:768]
    og = u[:, 768:1024]
    gts = u[:, 1024:1152]
    ig_col = jnp.where(real, gts, NEG_BIG)
    lf_col = jnp.where(real, _log_sigmoid(gts + bf), 0.0)
    ti = lax.broadcasted_iota(I32, (L, L), 0)
    si = lax.broadcasted_iota(I32, (L, L), 1)
    b_col = _mm((ti >= si).astype(F32), lf_col, True)
    lane = lax.broadcasted_iota(I32, (1, LANES), 1)
    z = jnp.where(lane < H, ig_col, b_col)
    zp = z if L == LANES else jnp.concatenate([z, jnp.zeros((LANES - L, LANES), F32)], axis=0)
    zt = zp.T
    b_st = jnp.concatenate([b_col[:, H + h:H + h + 1] for h in range(H)], axis=0)
    m0_st = jnp.concatenate([jnp.broadcast_to(m_prev[:, h:h + 1], (L, 1)) for h in range(H)], axis=0)
    rowv_st = jnp.concatenate(
        [jnp.broadcast_to(zt[h:h + 1, 0:L] - zt[H + h:H + h + 1, 0:L], (L, L)) for h in range(H)], axis=0)
    tr = lax.broadcasted_iota(I32, (H * L, L), 0) % L
    sc = lax.broadcasted_iota(I32, (H * L, L), 1)
    logw = jnp.where(sc <= tr, b_st + rowv_st, NEG_BIG)
    lp = b_st + m0_st
    m_st = jnp.maximum(lp, jnp.max(logw, axis=1, keepdims=True))
    w = jnp.exp(logw - m_st)
    sp = jnp.exp(lp - m_st)
    qs = _stack_heads(q, H, D)
    qk = _mm(qs, k, precise, _NT) * w
    row_head = lax.broadcasted_iota(I32, (H * L, W_B), 0) // L
    lane_head = lax.broadcasted_iota(I32, (H * L, W_B), 1) // D
    mask_st = row_head == lane_head
    num = jnp.where(mask_st, _mm(qk, v, precise), 0.0) + sp * _mm(qs, cbd, precise)
    den = jnp.sum(qk, axis=1, keepdims=True) + sp * jnp.sum(qs * nrow, axis=1, keepdims=True)
    hh = num / jnp.maximum(jnp.abs(den), jnp.exp(-m_st))
    y = _head_norm_stacked(hh, mask_st, H, L, D) * gn * _sigmoid(og)
    lh = lax.broadcasted_iota(I32, (1, W_B), 1) // D
    wlk = jnp.zeros((L, W_B), F32)
    sl_lanes = jnp.zeros((1, W_B), F32)
    m_new = jnp.zeros((1, LANES), F32)
    for h in range(H):
        ml = m_st[h * L + L - 1:h * L + L, :]
        bl = b_st[h * L + L - 1:h * L + L, :]
        wl = jnp.exp(bl - b_col[:, H + h:H + h + 1] + ig_col[:, h:h + 1] - ml)
        sl = jnp.exp(bl + m_prev[:, h:h + 1] - ml)
        wlk = jnp.where(lh == h, wl, wlk)
        sl_lanes = jnp.where(lh == h, sl, sl_lanes)
        m_new = jnp.where(lane == h, ml, m_new)
    kw = k * wlk
    bd = (lax.broadcasted_iota(I32, (W_B, W_B), 0) // D) == (lax.broadcasted_iota(I32, (W_B, W_B), 1) // D)
    c_new = sl_lanes * cbd + jnp.where(bd, _mm(kw, v, precise, _TN), 0.0)
    n_new = sl_lanes * nrow + jnp.sum(kw, axis=0, keepdims=True)
    return y, c_new, n_new, m_new


def _scan_kernel(ua_ref, ub_ref, wg_ref, bg_ref, gna_ref, cw_ref, cb_ref, bf_ref, gnb_ref,
                 s0_ref, c0_ref, n0_ref, m0_ref, conv0_ref,
                 ya_ref, yb_ref, sout_ref, cout_ref, nout_ref, mout_ref,
                 st_sc, c_sc, n_sc, m_sc, buf_sc, *, L, lo, hi, precise, bpg):
    c = pl.program_id(1)

    @pl.when(c == 0)
    def _():
        st_sc[...] = s0_ref[...]
        c_sc[...] = c0_ref[...]
        n_sc[...] = n0_ref[...]
        m_sc[...] = m0_ref[...]
        buf_sc[:, 0:SUBLANES, :] = conv0_ref[...]

    for b in range(bpg):
        y, st_new = _gla_step(ua_ref[b], wg_ref[...], bg_ref[...], gna_ref[...], st_sc[b], c,
                              L=L, lo=lo, hi=hi, precise=precise)
        ya_ref[b] = y
        st_sc[b] = st_new
        sout_ref[b] = st_new
    for b in range(bpg):
        y, c_new, n_new, m_new = _mlstm_step(ub_ref[b], cw_ref[...], cb_ref[...], bf_ref[...], gnb_ref[...],
                                             c_sc[b], n_sc[b], m_sc[b], buf_sc.at[b], c,
                                             L=L, lo=lo, hi=hi, precise=precise)
        yb_ref[b] = y
        c_sc[b] = c_new
        n_sc[b] = n_new
        m_sc[b] = m_new
        cout_ref[b] = c_new
        nout_ref[b] = n_new
        mout_ref[b] = m_new


def _scan(ua, ub, wg, bg, gna, cw, cb, bf, gnb, s0t, c0, n0, m0, conv0, *, nb, nc, L, lo, hi, precise):
    bpg = 2
    assert nb % bpg == 0
    rows = ua.shape[0]
    t = rows // nb
    fix2 = lambda g, c: (0, 0)
    tok = lambda g, c: (g, c, 0)
    per_g = lambda g, c: (g, 0, 0)
    kd = H_A * DK_A
    ya, yb, st_out, c_out, n_out, m_out = pl.pallas_call(
        functools.partial(_scan_kernel, L=L, lo=lo, hi=hi, precise=precise, bpg=bpg),
        grid=(nb // bpg, nc),
        in_specs=[pl.BlockSpec((bpg, L, _GROUP_WIDTHS[0]), tok), pl.BlockSpec((bpg, L, _GROUP_WIDTHS[1]), tok),
                  pl.BlockSpec((LANES, LANES), fix2), pl.BlockSpec((1, LANES), fix2), pl.BlockSpec((1, W_A), fix2),
                  pl.BlockSpec((CONV_W, 2 * W_B), fix2), pl.BlockSpec((1, 2 * W_B), fix2),
                  pl.BlockSpec((1, LANES), fix2), pl.BlockSpec((1, W_B), fix2),
                  pl.BlockSpec((bpg, W_A, kd), per_g), pl.BlockSpec((bpg, W_B, W_B), per_g),
                  pl.BlockSpec((bpg, 1, W_B), per_g), pl.BlockSpec((bpg, 1, LANES), per_g),
                  pl.BlockSpec((bpg, SUBLANES, 2 * W_B), per_g)],
        out_specs=[pl.BlockSpec((bpg, L, W_A), tok), pl.BlockSpec((bpg, L, W_B), tok),
                   pl.BlockSpec((bpg, W_A, kd), per_g), pl.BlockSpec((bpg, W_B, W_B), per_g),
                   pl.BlockSpec((bpg, 1, W_B), per_g), pl.BlockSpec((bpg, 1, LANES), per_g)],
        out_shape=[jax.ShapeDtypeStruct((nb, t, W_A), F32), jax.ShapeDtypeStruct((nb, t, W_B), F32),
                   jax.ShapeDtypeStruct((nb, W_A, kd), F32), jax.ShapeDtypeStruct((nb, W_B, W_B), F32),
                   jax.ShapeDtypeStruct((nb, 1, W_B), F32), jax.ShapeDtypeStruct((nb, 1, LANES), F32)],
        scratch_shapes=[pltpu.VMEM((bpg, W_A, kd), F32), pltpu.VMEM((bpg, W_B, W_B), F32),
                        pltpu.VMEM((bpg, 1, W_B), F32), pltpu.VMEM((bpg, 1, LANES), F32),
                        pltpu.VMEM((bpg, L + SUBLANES, 2 * W_B), F32)],
        compiler_params=_params(("parallel", "arbitrary")),
        name="mixer_scans",
    )(ua.reshape(nb, t, -1), ub.reshape(nb, t, -1), wg, bg, gna, cw, cb, bf, gnb, s0t, c0, n0, m0, conv0)
    return ya.reshape(rows, W_A), yb.reshape(rows, W_B), st_out, c_out, n_out, m_out


def _dsa_kernel(u_ref, k_ref, vt_ref, ki_ref, y_ref, sc_sc, pl_sc, act_sc, sa_sc, sb_sc, ba_sc, bb_sc, acc_sc, m_sc, l_sc, y_sc,
                *, KB, nkb_alloc, n_sel, adm_lo, adm_cap, end_base, end_step, end_half, out_rows, out_step, precise):
    b = pl.program_id(0)
    c = pl.program_id(1)
    TQ = LANES
    G = H_C // HKV_C
    lane_q = lax.broadcasted_iota(I32, (1, TQ), 1)
    end_lo = end_base + end_step * c
    end_q = jnp.minimum(jnp.where(lane_q < CHUNK, end_lo, end_lo + end_half), adm_cap)
    end_max = jnp.minimum(end_lo + end_half, adm_cap)
    npair = (end_max + 2 * KB - 1) // (2 * KB)
    u = u_ref[...]
    qi_t = (u[:, 768:896] * (D_IDX ** -0.5)).T
    kw_t = u[:, 896:1024].T
    qi_st = jnp.concatenate([qi_t[h * D_IDX:(h + 1) * D_IDX, :] for h in range(H_IDX)], axis=1)
    if not precise:
        qi_st = qi_st.astype(BF16)
    wi = [kw_t[D_IDX + h:D_IDX + h + 1, :] * (H_IDX ** -0.5) for h in range(H_IDX)]
    key_idx = lax.broadcasted_iota(I32, (KB, TQ), 0)

    def score_block(kb):
        raw = jnp.maximum(_mm(ki_ref[0, kb], qi_st, precise), 0.0)
        isc = wi[0] * raw[:, 0:TQ]
        for h in range(1, H_IDX):
            isc = isc + wi[h] * raw[:, h * TQ:(h + 1) * TQ]
        idx = kb * KB + key_idx
        sc_sc[kb] = jnp.where((idx >= adm_lo) & (idx < end_q), isc, -jnp.inf)

    def score_body(i, carry):
        score_block(2 * i)
        score_block(2 * i + 1)
        return carry

    lax.fori_loop(0, npair, score_body, 0)

    def count(pred_fn):
        def body(i, acc):
            for kb in (2 * i, 2 * i + 1):
                hit = jnp.where(pred_fn(sc_sc[kb]), 1, 0)
                acc = acc + jnp.sum(hit.reshape(KB // SUBLANES, SUBLANES, TQ), axis=0)
            return acc
        acc = lax.fori_loop(0, npair, body, jnp.zeros((SUBLANES, TQ), I32))
        return jnp.sum(acc, axis=0, keepdims=True)

    def key_to_float(key):
        return lax.bitcast_convert_type(jnp.where(key >= 0, key, key ^ 0x7FFFFFFF), F32)

    U32 = jnp.uint32
    nch = (2 * npair + PLANE_CHUNK - 1) // PLANE_CHUNK

    def plane_body(kb, carry):
        bits = lax.bitcast_convert_type(sc_sc[kb], U32)
        flip = jnp.where(bits >= U32(0x80000000), U32(0xFFFFFFFF), U32(0x80000000))
        uk = bits ^ flip
        uk = jnp.where(uk == U32(0x7FFFFFFF), U32(0x80000000), uk)
        a = [uk[SUBLANES * j:SUBLANES * (j + 1), :] for j in range(32)]
        j, m = 16, 0x0000FFFF
        while j:
            k = 0
            while k < 32:
                t = (a[k] ^ (a[k + j] >> U32(j))) & U32(m)
                a[k] = a[k] ^ t
                a[k + j] = a[k + j] ^ (t << U32(j))
                k = (k + j + 1) & ~j
            j >>= 1
            m = (m ^ (m << j)) & 0xFFFFFFFF
        for p in range(32):
            pl_sc[p, kb] = a[p]
        return carry

    lax.fori_loop(0, 2 * npair, plane_body, 0)

    def plane_pad_body(kb, carry):
        for p in range(32):
            pl_sc[p, kb] = jnp.zeros((SUBLANES, TQ), U32)
        return carry

    lax.fori_loop(2 * npair, nch * PLANE_CHUNK, plane_pad_body, 0)

    def chunk(i):
        return pl.ds(pl.multiple_of(i * PLANE_CHUNK, PLANE_CHUNK), PLANE_CHUNK)

    def popsum(words):
        return jnp.sum(lax.population_count(words).astype(I32), axis=0)

    def first_body(i, acc):
        act_sc[chunk(i)] = jnp.full((PLANE_CHUNK, SUBLANES, TQ), 0xFFFFFFFF, U32)
        return acc + popsum(pl_sc[0, chunk(i)])

    cnt1 = jnp.sum(lax.fori_loop(0, nch, first_body, jnp.zeros((SUBLANES, TQ), I32)), axis=0, keepdims=True)

    def radix_body(p, carry):
        rem, ukey, c1 = carry
        take = c1 >= rem

        def body(i, acc):
            act = act_sc[chunk(i)]
            prev = pl_sc[p - 1, chunk(i)]
            act = jnp.where(take, act & prev, act & ~prev)
            act_sc[chunk(i)] = act
            return acc + popsum(act & pl_sc[p, chunk(i)])

        acc = lax.fori_loop(0, nch, body, jnp.zeros((SUBLANES, TQ), I32))
        bit = lax.shift_left(U32(1), jnp.asarray(32 - p, U32))
        return (jnp.where(take, rem, rem - c1), jnp.where(take, ukey | bit, ukey),
                jnp.sum(acc, axis=0, keepdims=True))

    rem, ukey, c1 = lax.fori_loop(1, 32, radix_body,
                                  (jnp.full((1, TQ), n_sel, I32), jnp.zeros((1, TQ), U32), cnt1))
    ukey = jnp.where(c1 >= rem, ukey | U32(1), ukey)
    thr_fast = key_to_float(lax.bitcast_convert_type(ukey ^ U32(0x80000000), I32))
    c_ge = count(lambda s: s >= thr_fast)
    c_gt = count(lambda s: s > thr_fast)
    ok = jnp.min(jnp.where((c_gt < n_sel) & (c_ge >= n_sel), 1, 0)) == 1

    def float_search():
        def thr_body(i, p):
            cand = p + lax.shift_left(jnp.int32(1), 31 - i)
            cand_f = key_to_float(cand)
            cnt = count(lambda s: s >= cand_f)
            return jnp.where(cnt >= n_sel, cand, p)

        thr_key = lax.fori_loop(0, 32, thr_body, jnp.full((1, TQ), INT_MIN, I32))
        t = key_to_float(jnp.maximum(thr_key, KEY_NEG_INF))
        return t, count(lambda s: s > t)

    thr, n_above = lax.cond(ok, lambda: (thr_fast, c_gt), float_search)
    need = (n_sel - n_above).astype(F32)

    q_t = [(u[:, j * LANES:(j + 1) * LANES] * (DH_C ** -0.5 * LOG2_E)).T for j in range(W_C // LANES)]
    zero_half = jnp.zeros((DH_C, G * TQ), F32)
    qz = []
    for n in range(HKV_C):
        heads = [n * G + g for g in range(G)]
        qn = jnp.concatenate([q_t[hc // 2][(hc % 2) * DH_C:(hc % 2 + 1) * DH_C, :] for hc in heads], axis=1)
        qn = jnp.concatenate([qn, zero_half] if n == 0 else [zero_half, qn], axis=0)
        qz.append(qn if precise else qn.astype(BF16))
    tri = (lax.broadcasted_iota(I32, (KB, KB), 0) >= lax.broadcasted_iota(I32, (KB, KB), 1)).astype(BF16)

    def mask_block(kb, tie_seen, dst):
        s_idx = sc_sc[kb]
        eq = (s_idx == thr) & (s_idx > -jnp.inf)
        rank = tie_seen + jnp.dot(tri, jnp.where(eq, 1.0, 0.0).astype(BF16), preferred_element_type=F32)
        sel = (s_idx > thr) | (eq & (rank <= need))
        dst[...] = jnp.where(sel, 0.0, NEG_BIG)
        return rank[KB - 1:KB, :]

    acc_sc[...] = jnp.zeros_like(acc_sc)
    m_sc[...] = jnp.full_like(m_sc, M_INIT)
    l_sc[...] = jnp.zeros_like(l_sc)
    W2 = 2 * TQ
    units = [(n, slice(gp * W2, (gp + 1) * W2)) for n in range(HKV_C) for gp in range(G // 2)]

    def logits(kb, dst):
        kblk = k_ref[0, kb]
        for ui, (n, cols) in enumerate(units):
            dst[ui] = _mm(kblk, qz[n][:, cols], precise)

    def att_step(kb, tie_seen, cur, nxt, bias_cur, bias_nxt):
        logits(jnp.minimum(kb + 1, nkb_alloc - 1), nxt)
        tie_seen = mask_block(jnp.minimum(kb + 1, 2 * npair - 1), tie_seen, bias_nxt)
        bias = bias_cur[...]
        bias2 = jnp.concatenate([bias, bias], axis=1)
        vtblk = vt_ref[0, kb]
        for ui, (n, cols) in enumerate(units):
            s = cur[ui] + bias2
            m_old = m_sc[n, :, cols]
            m_new = jnp.maximum(m_old, jnp.max(s, axis=0, keepdims=True))
            p = jnp.exp2(s - m_new)
            alpha = jnp.exp2(m_old - m_new)
            l_sc[n, :, cols] = alpha * l_sc[n, :, cols] + jnp.sum(p, axis=0, keepdims=True)
            acc_sc[n, :, cols] = alpha * acc_sc[n, :, cols] + _mm(vtblk[n * DH_C:(n + 1) * DH_C, :], p, precise)
            m_sc[n, :, cols] = m_new
        return tie_seen

    logits(0, sa_sc)
    tie0 = mask_block(0, jnp.zeros((1, TQ), F32), ba_sc)

    def att_body(i, tie_seen):
        tie_seen = att_step(2 * i, tie_seen, sa_sc, sb_sc, ba_sc, bb_sc)
        return att_step(2 * i + 1, tie_seen, sb_sc, sa_sc, bb_sc, ba_sc)

    lax.fori_loop(0, npair, att_body, tie0)
    o = [acc_sc[n] / l_sc[n] for n in range(HKV_C)]
    for j in range(W_C // LANES):
        n, g0 = (2 * j) // G, (2 * j) % G
        pair = jnp.concatenate([o[n][:, g0 * TQ:(g0 + 1) * TQ], o[n][:, (g0 + 1) * TQ:(g0 + 2) * TQ]], axis=0)
        y_sc[:, j * LANES:(j + 1) * LANES] = pair.T
    off = pl.multiple_of(b * out_step, SUBLANES)
    y_ref[...] = y_sc[pl.ds(off, out_rows), :]


def _dsa(uc, k_arr, vt_arr, ki_arr, *, nb, nq, n_sel, adm_lo, adm_cap, end_base, end_step, end_half,
         out_rows, out_step, precise):
    rows = uc.shape[0]
    nkb_max, KB = k_arr.shape[1], k_arr.shape[2]
    assert nkb_max % 2 == 0
    n_plane = -(-nkb_max // PLANE_CHUNK) * PLANE_CHUNK
    G = H_C // HKV_C
    per_b = lambda b, c: (b, 0, 0, 0)
    u_map = (lambda b, c: (b * nq + c, 0)) if out_step == 0 else (lambda b, c: (0, 0))
    return pl.pallas_call(
        functools.partial(_dsa_kernel, KB=KB, nkb_alloc=nkb_max, n_sel=n_sel, adm_lo=adm_lo, adm_cap=adm_cap,
                          end_base=end_base, end_step=end_step, end_half=end_half, out_rows=out_rows,
                          out_step=out_step, precise=precise),
        grid=(nb, nq),
        in_specs=[pl.BlockSpec((LANES, _GROUP_WIDTHS[2]), u_map),
                  pl.BlockSpec((1, nkb_max, KB, HKV_C * DH_C), per_b),
                  pl.BlockSpec((1, nkb_max, HKV_C * DH_C, KB), per_b),
                  pl.BlockSpec((1, nkb_max, KB, D_IDX), per_b)],
        out_specs=pl.BlockSpec((out_rows, W_C), lambda b, c: (b * nq + c, 0)),
        out_shape=jax.ShapeDtypeStruct((rows, W_C), F32),
        scratch_shapes=[pltpu.VMEM((nkb_max, KB, LANES), F32),
                        pltpu.VMEM((32, n_plane, SUBLANES, LANES), jnp.uint32),
                        pltpu.VMEM((n_plane, SUBLANES, LANES), jnp.uint32),
                        pltpu.VMEM((HKV_C * (G // 2), KB, 2 * LANES), F32),
                        pltpu.VMEM((HKV_C * (G // 2), KB, 2 * LANES), F32),
                        pltpu.VMEM((KB, LANES), F32), pltpu.VMEM((KB, LANES), F32),
                        pltpu.VMEM((HKV_C, DH_C, G * LANES), F32), pltpu.VMEM((HKV_C, 1, G * LANES), F32),
                        pltpu.VMEM((HKV_C, 1, G * LANES), F32), pltpu.VMEM((LANES, W_C), F32)],
        compiler_params=_params(("parallel", "arbitrary")),
        name="dsa_attend",
    )(uc, k_arr, vt_arr, ki_arr)


def _outproj_kernel(ya_ref, yb_ref, yc_ref, h_ref, w_ref, g_ref, b_ref, o_ref, *, alpha, precise):
    mix = (_mm(ya_ref[...], w_ref[0:W_A, :], precise) + _mm(yb_ref[...], w_ref[W_A:W_A + W_B, :], precise)
           + _mm(yc_ref[...], w_ref[W_A + W_B:, :], precise))
    o_ref[...] = _ln(alpha * h_ref[...] + mix, g_ref[...], b_ref[...])


def _outproj(ya, yb, yc, h, w_out_l, g, b, alpha, precise):
    n, d = h.shape
    tm = _pick_tile(n, 768)
    row = lambda i: (i, 0)
    fix = lambda i: (0, 0)
    return pl.pallas_call(
        functools.partial(_outproj_kernel, alpha=alpha, precise=precise),
        grid=(n // tm,),
        in_specs=[pl.BlockSpec((tm, W_A), row), pl.BlockSpec((tm, W_B), row), pl.BlockSpec((tm, W_C), row),
                  pl.BlockSpec((tm, d), row), pl.BlockSpec(w_out_l.shape, fix),
                  pl.BlockSpec((1, d), fix), pl.BlockSpec((1, d), fix)],
        out_specs=pl.BlockSpec((tm, d), row),
        out_shape=jax.ShapeDtypeStruct((n, d), F32),
        compiler_params=_params(("parallel",)),
        name="outproj_ln1",
    )(ya, yb, yc, h, w_out_l, g.reshape(1, d), b.reshape(1, d))


def _moe_kernel(h_ref, wr_ref, br_ref, wg_ref, wu_ref, wd_ref, g_ref, b_ref, o_ref, acc_sc, gate_sc, *, alpha, precise):
    grp = pl.program_id(1)
    x = h_ref[...]
    tm = x.shape[0]
    xo = x if precise else x.astype(BF16)
    eg = N_EXPERTS // N_GROUPS

    @pl.when(grp == 0)
    def _():
        s = _sigmoid(_mm(wr_ref[...], xo, precise, _NT))
        sb = s + br_ref[...]
        srow = [s[j:j + 1, :] for j in range(N_EXPERTS)]
        brow = [sb[j:j + 1, :] for j in range(N_EXPERTS)]
        gscore = []
        for g in range(N_GROUPS):
            best = None
            for i in range(eg):
                for j in range(i + 1, eg):
                    pair = brow[g * eg + i] + brow[g * eg + j]
                    best = pair if best is None else jnp.maximum(best, pair)
            gscore.append(best)
        gmax = functools.reduce(jnp.maximum, gscore)
        taken = jnp.zeros_like(gmax) > 1.0
        sel = [None] * N_EXPERTS
        for g in range(N_GROUPS):
            is_g = jnp.logical_and(jnp.logical_not(taken), gscore[g] == gmax)
            taken = jnp.logical_or(taken, is_g)
            for i in range(eg):
                a = brow[g * eg + i]
                rank = jnp.zeros(a.shape, I32)
                for j in range(eg):
                    if j == i:
                        continue
                    o = brow[g * eg + j]
                    ahead = (o >= a) if j < i else (o > a)
                    rank = rank + jnp.where(ahead, 1, 0)
                sel[g * eg + i] = jnp.logical_and(is_g, rank < TOP_K)
        denom = jnp.zeros_like(gmax)
        for j in range(N_EXPERTS):
            denom = denom + jnp.where(sel[j], srow[j], 0.0)
        gates = [jnp.where(sel[j], srow[j] / denom, 0.0) for j in range(N_EXPERTS)]
        gt = jnp.concatenate(gates + [jnp.zeros((LANES - N_EXPERTS, tm), F32)], axis=0)
        gate_sc[...] = gt.T
        acc_sc[...] = jnp.zeros_like(acc_sc)

    lane = lax.broadcasted_iota(I32, (1, LANES), 1)
    gate = gate_sc[...]
    hes = []
    for j in range(eg):
        gcol = jnp.sum(jnp.where(lane == grp * eg + j, gate, 0.0), axis=1, keepdims=True)
        hes.append(gcol * (_silu(_mm(xo, wg_ref[j], precise)) * _mm(xo, wu_ref[j], precise)))
    acc_sc[...] += _mm(jnp.concatenate(hes, axis=1), wd_ref[0], precise)

    @pl.when(grp == N_GROUPS - 1)
    def _():
        o_ref[...] = _ln(alpha * x + acc_sc[...], g_ref[...], b_ref[...])


def _moe(h, wr_t, br_col, w_gate, w_up, w_down, g, b, alpha, precise):
    n, d = h.shape
    tm = _pick_tile(n, MOE_TILE)
    assert tm % LANES == 0, tm
    eg = N_EXPERTS // N_GROUPS
    row = lambda i, e: (i, 0)
    fix = lambda i, e: (0, 0)
    per_g = lambda i, e: (e, 0, 0)
    return pl.pallas_call(
        functools.partial(_moe_kernel, alpha=alpha, precise=precise),
        grid=(n // tm, N_GROUPS),
        in_specs=[pl.BlockSpec((tm, d), row), pl.BlockSpec((LANES, d), fix), pl.BlockSpec((LANES, 1), fix),
                  pl.BlockSpec((eg, d, D_EXPERT), per_g), pl.BlockSpec((eg, d, D_EXPERT), per_g),
                  pl.BlockSpec((1, eg * D_EXPERT, d), per_g), pl.BlockSpec((1, d), fix), pl.BlockSpec((1, d), fix)],
        out_specs=pl.BlockSpec((tm, d), row),
        out_shape=jax.ShapeDtypeStruct((n, d), F32),
        scratch_shapes=[pltpu.VMEM((tm, d), F32), pltpu.VMEM((tm, LANES), F32)],
        compiler_params=_params(("parallel", "arbitrary")),
        name="moe_ln2",
    )(h, wr_t, br_col, w_gate, w_up, w_down.reshape(N_GROUPS, eg * D_EXPERT, d), g.reshape(1, d), b.reshape(1, d))


def _pack_in_weights(w_in_l, b_in_l):
    offs = {}
    off = 0
    for name, n in _SPLITS:
        offs[name] = (off, n)
        off += n
    wcols, bcols = [], []
    for grp in (_GROUP_A, _GROUP_B, _GROUP_C):
        for name, n in grp:
            if name is None:
                wcols.append(jnp.zeros((w_in_l.shape[0], n), w_in_l.dtype))
                bcols.append(jnp.zeros((n,), b_in_l.dtype))
            else:
                o, m = offs[name]
                assert m == n
                wcols.append(w_in_l[:, o:o + n])
                bcols.append(b_in_l[o:o + n])
    return jnp.concatenate(wcols, axis=1), jnp.concatenate(bcols).reshape(1, -1).astype(F32)


def _block_diag_embed(x):
    B, H, r, c = x.shape
    eye = jnp.eye(H, dtype=x.dtype)
    return (x[:, :, :, None, :] * eye[None, :, None, :, None]).reshape(B, H * r, H * c)


def _block_diag_extract(x, H):
    B, R, C = x.shape
    r, c = R // H, C // H
    x5 = x.reshape(B, H, r, H, c)
    return jnp.stack([x5[:, h, :, h, :] for h in range(H)], axis=1)


def _mixer_states_in(s_gla, s_c, s_n, s_m, s_conv):
    nb = s_gla.shape[0]
    s0t = _block_diag_embed(jnp.swapaxes(s_gla.astype(F32), -1, -2))
    c0 = _block_diag_embed(jnp.swapaxes(s_c.astype(F32), -1, -2))
    n0 = s_n.astype(F32).reshape(nb, 1, W_B)
    m0 = jnp.pad(s_m.astype(F32), ((0, 0), (0, LANES - H_B))).reshape(nb, 1, LANES)
    conv0 = jnp.pad(s_conv.astype(F32), ((0, 0), (SUBLANES - (CONV_W - 1), 0), (0, 0)))
    return s0t, c0, n0, m0, conv0


def _key_blocks(k, v, ki, dtype):
    nb, s, _ = k.shape
    s_even = -(-s // (2 * KEY_BLOCK)) * 2 * KEY_BLOCK
    k, v, ki = (jnp.pad(a, ((0, 0), (0, s_even - s), (0, 0))) for a in (k, v, ki))
    nkb = s_even // KEY_BLOCK
    kk = k.astype(dtype).reshape(nb, nkb, KEY_BLOCK, -1)
    vt = jnp.swapaxes(v.astype(dtype).reshape(nb, nkb, KEY_BLOCK, -1), -1, -2)
    kib = ki.astype(dtype).reshape(nb, nkb, KEY_BLOCK, -1)
    return kk, vt, kib


def _layer(h, stream, l, p):
    precise = stream['precise']
    wdt = F32 if precise else BF16
    (w_in_p, b_in_p, wg, bg, gn_a, cw, cb, bf, gn_b, w_out_l, ln1_g, ln1_b, wr_t, br_col,
     w_gate, w_up, w_down, ln2_g, ln2_b, ln_in_g, ln_in_b, alpha) = p
    h, ua, ub, uc = _inproj(h, ln_in_g, ln_in_b, w_in_p.astype(wdt), b_in_p, apply_ln=(l == 0), precise=precise)
    nb, nc, L, lo, hi = stream['nb'], stream['nc'], stream['L'], stream['lo'], stream['hi']
    s0t, c0, n0, m0, conv0 = stream['states'](l)
    ya, yb, st_out, c_out, n_out, m_out = _scan(ua, ub, wg.astype(wdt), bg, gn_a, cw, cb, bf, gn_b, s0t, c0, n0, m0,
                                                conv0, nb=nb, nc=nc, L=L, lo=lo, hi=hi, precise=precise)
    k_arr, vt_arr, ki_arr = stream['keys'](l, uc, wdt)
    yc = _dsa(uc, k_arr, vt_arr, ki_arr, nb=nb, precise=precise, **stream['dsa'])
    h1 = _outproj(ya, yb, yc, h, w_out_l.astype(wdt), ln1_g, ln1_b, alpha, precise)
    h2 = _moe(h1, wr_t.astype(wdt), br_col, w_gate.astype(wdt), w_up.astype(wdt), w_down.astype(wdt), ln2_g, ln2_b,
              alpha, precise)
    S = jnp.swapaxes(_block_diag_extract(st_out, H_A), -1, -2)
    C = jnp.swapaxes(_block_diag_extract(c_out, H_B), -1, -2)
    return h2, (ub, uc, S, C, n_out.reshape(nb, H_B, DH_B), m_out[:, 0, :H_B])


def kernel(x_prompt, x_sample, cache_k, cache_v, cache_kidx, state_gla, state_mlstm_c, state_mlstm_n, state_mlstm_m, state_conv, meta_tokens, ln_in_g, ln_in_b, w_in, b_in, w_gla_gate, b_gla_gate, g_gla_norm, conv_w, conv_b, b_forget, g_mlstm_norm, w_out, ln1_g, ln1_b, w_router, b_router, w_gate, w_up, w_down, ln2_g, ln2_b):
    depth = w_in.shape[0]
    alpha = float((2 * depth) ** 0.25)
    B, S_p, D = x_prompt.shape
    DB, Ts, _ = x_sample.shape
    P = cache_k.shape[2]
    assert S_p % CHUNK == 0 and CHUNK % Ts == 0 and P % CHUNK == 0 and Ts >= CONV_W - 1
    assert DB * Ts == LANES and Ts % SUBLANES == 0
    KB = KEY_BLOCK

    wr_t = jnp.pad(w_router.T, ((0, LANES - N_EXPERTS), (0, 0)))
    br_col = jnp.pad(b_router.astype(F32), (0, LANES - N_EXPERTS)).reshape(LANES, 1)

    def layer_params(l):
        w_in_p, b_in_p = _pack_in_weights(w_in[l], b_in[l])
        wg = jnp.pad(w_gla_gate[l], ((0, LANES - GATE_RANK), (0, 0)))
        bf = jnp.pad(b_forget[l].astype(F32), (H_B, LANES - 2 * H_B)).reshape(1, LANES)
        return (w_in_p, b_in_p, wg, b_gla_gate[l].reshape(1, -1), g_gla_norm[l].reshape(1, -1),
                conv_w[l], conv_b[l].reshape(1, -1), bf, g_mlstm_norm[l].reshape(1, -1), w_out[l],
                ln1_g[l], ln1_b[l], wr_t, br_col, w_gate[l], w_up[l], w_down[l], ln2_g[l], ln2_b[l],
                ln_in_g, ln_in_b, alpha)

    params = [layer_params(l) for l in range(depth)]

    front = CHUNK - N_META
    t_real = N_META + S_p
    t_pad = -(-(front + t_real) // KB) * KB
    hp = jnp.concatenate([jnp.zeros((B, front, D), x_prompt.dtype),
                          jnp.broadcast_to(meta_tokens.astype(x_prompt.dtype)[None], (B, N_META, D)), x_prompt,
                          jnp.zeros((B, t_pad - front - t_real, D), x_prompt.dtype)], axis=1).reshape(B * t_pad, D)

    def prompt_keys(l, uc, dtype):
        u3 = uc.reshape(B, t_pad, -1)
        return _key_blocks(u3[:, :, 512:640], u3[:, :, 640:768], u3[:, :, 896:896 + D_IDX], dtype)

    zero_states = _mixer_states_in(jnp.zeros((B, H_A, DK_A, DV_A), F32), jnp.zeros((B, H_B, DH_B, DH_B), F32),
                                   jnp.zeros((B, H_B, DH_B), F32), jnp.zeros((B, H_B), F32),
                                   jnp.zeros((B, CONV_W - 1, 2 * W_B), F32))
    prompt_stream = dict(nb=B, nc=t_pad // SCAN_CHUNK, L=SCAN_CHUNK, lo=front, hi=front + t_real, precise=False,
                         states=lambda l: zero_states, keys=prompt_keys,
                         dsa=dict(nq=t_pad // LANES, n_sel=min(TOPK_KEYS, S_p // 4), adm_lo=front,
                                  adm_cap=front + t_real, end_base=CHUNK, end_step=2 * CHUNK, end_half=CHUNK,
                                  out_rows=LANES, out_step=0))
    p_out = [[] for _ in range(8)]
    for l in range(depth):
        hp, (ub, uc, S, C, n_o, m_o) = _layer(hp, prompt_stream, l, params[l])
        ub3 = ub.reshape(B, t_pad, -1)[:, front:front + t_real]
        uc3 = uc.reshape(B, t_pad, -1)[:, front:front + t_real]
        vals = (uc3[:, :, 512:640].reshape(B, t_real, HKV_C, DH_C), uc3[:, :, 640:768].reshape(B, t_real, HKV_C, DH_C),
                uc3[:, :, 896:896 + D_IDX], S, C, n_o, m_o, ub3[:, t_real - (CONV_W - 1):, 0:2 * W_B])
        for acc, a in zip(p_out, vals):
            acc.append(a)
    y_prompt = hp.reshape(B, t_pad, D)[:, front + N_META:front + t_real]

    s_tot = P + Ts
    s_pad = -(-s_tot // KB) * KB

    def sample_keys(l, uc, dtype):
        u3 = uc.reshape(DB, Ts, -1)

        def cat(cache, new):
            a = jnp.concatenate([cache.astype(F32).reshape(DB, P, -1), new], axis=1)
            return jnp.pad(a, ((0, 0), (0, s_pad - s_tot), (0, 0)))

        return _key_blocks(cat(cache_k[l], u3[:, :, 512:640]), cat(cache_v[l], u3[:, :, 640:768]),
                           cat(cache_kidx[l], u3[:, :, 896:896 + D_IDX]), dtype)

    sample_stream = dict(nb=DB, nc=1, L=Ts, lo=0, hi=Ts, precise=True,
                         states=lambda l: _mixer_states_in(state_gla[l], state_mlstm_c[l], state_mlstm_n[l],
                                                           state_mlstm_m[l], state_conv[l]),
                         keys=sample_keys,
                         dsa=dict(nq=1, n_sel=min(TOPK_KEYS, s_tot // 4), adm_lo=0, adm_cap=s_tot, end_base=s_tot,
                                  end_step=0, end_half=0, out_rows=Ts, out_step=Ts))
    hs = x_sample.reshape(DB * Ts, D)
    s_out = [[] for _ in range(8)]
    for l in range(depth):
        hs, (ub, uc, S, C, n_o, m_o) = _layer(hs, sample_stream, l, params[l])
        ub3 = ub.reshape(DB, Ts, -1)
        uc3 = uc.reshape(DB, Ts, -1)
        vals = (uc3[:, :, 512:640].reshape(DB, Ts, HKV_C, DH_C), uc3[:, :, 640:768].reshape(DB, Ts, HKV_C, DH_C),
                uc3[:, :, 896:896 + D_IDX], S, C, n_o, m_o, ub3[:, Ts - (CONV_W - 1):, 0:2 * W_B])
        for acc, a in zip(s_out, vals):
            acc.append(a)
    y_sample = hs.reshape(DB, Ts, D)

    return (y_prompt, y_sample, *[jnp.stack(a) for a in p_out], *[jnp.stack(a) for a in s_out])
```

```python
import functools

import jax
import jax.numpy as jnp
from jax import lax
from jax.experimental import pallas as pl
from jax.experimental.pallas import tpu as pltpu

F32 = jnp.float32
BF16 = jnp.bfloat16
I32 = jnp.int32

CHUNK = 64
N_META = 16
H_A, DK_A, DV_A = 4, 32, 64
GATE_RANK, GATE_TAU = 16, 16.0
H_B, DH_B, CONV_W = 4, 64, 4
H_C, HKV_C, DH_C = 8, 2, 64
H_IDX, D_IDX = 4, 32
TOPK_KEYS = 256
N_EXPERTS, N_GROUPS, TOP_K, D_EXPERT = 16, 4, 2, 256
W_A, W_B, W_C = H_A * DV_A, H_B * DH_B, H_C * DH_C
LN_EPS = 1e-5

LANES = 128
SUBLANES = 8
VMEM_LIMIT = 56 * 1024 * 1024
KEY_BLOCK = 256
MOE_TILE = 768
PLANE_CHUNK = 8
VT_ROWS = DH_C + 16
SCAN_CHUNK = 128
NEG_BIG = -1e30
M_INIT = -1e29
INT_MIN = -2 ** 31
LOG2_E = 1.4426950408889634
KEY_NEG_INF = INT_MIN + 0x7FFFFF

_SPLITS = (
    ('a_q', H_A * DK_A), ('a_k', H_A * DK_A), ('a_v', W_A), ('a_g', GATE_RANK), ('a_r', W_A),
    ('b_qk', 2 * W_B), ('b_v', W_B), ('b_i', H_B), ('b_f', H_B), ('b_o', W_B),
    ('c_q', W_C), ('c_k', HKV_C * DH_C), ('c_v', HKV_C * DH_C),
    ('c_qi', H_IDX * D_IDX), ('c_w', H_IDX), ('c_ki', D_IDX),
)
_GROUP_A = (('a_q', 128), ('a_k', 128), ('a_v', 256), ('a_r', 256), ('a_g', 16), (None, 112))
_GROUP_B = (('b_qk', 512), ('b_v', 256), ('b_o', 256), ('b_i', 4), ('b_f', 4), (None, 120))
_GROUP_C = (('c_q', 512), ('c_k', 128), ('c_v', 128), ('c_qi', 128), ('c_ki', 32), ('c_w', 4), (None, 92))
_GROUP_WIDTHS = tuple(sum(n for _, n in g) for g in (_GROUP_A, _GROUP_B, _GROUP_C))

_NN = (((1,), (0,)), ((), ()))
_NT = (((1,), (1,)), ((), ()))
_TN = (((0,), (0,)), ((), ()))


def _mm(a, b, precise, dims=_NN):
    if precise:
        return lax.dot_general(a.astype(F32), b.astype(F32), dims, precision=lax.Precision.HIGHEST,
                               preferred_element_type=F32)
    return lax.dot_general(a.astype(BF16), b.astype(BF16), dims, preferred_element_type=F32)


def _cumsum_rows(x, L):
    ti = lax.broadcasted_iota(I32, (L, L), 0)
    si = lax.broadcasted_iota(I32, (L, L), 1)
    return lax.dot_general((ti >= si).astype(F32), x, _NN, precision=lax.Precision.HIGHEST, preferred_element_type=F32)


def _log_sigmoid(x):
    return jnp.minimum(x, 0.0) - jnp.log(1.0 + jnp.exp(-jnp.abs(x)))


def _sigmoid(x):
    return 1.0 / (1.0 + jnp.exp(-x))


def _silu(x):
    return x * _sigmoid(x)


def _ln(x, g, b):
    mu = jnp.mean(x, axis=-1, keepdims=True)
    xc = x - mu
    var = jnp.mean(xc * xc, axis=-1, keepdims=True)
    return xc * lax.rsqrt(var + LN_EPS) * g + b


def _pick_tile(n, cap):
    best = None
    for t in range(SUBLANES, min(n, cap) + 1, SUBLANES):
        if n % t == 0:
            best = t
    assert best is not None, n
    return best


def _stack_heads(x, n_heads, width):
    lane_head = lax.broadcasted_iota(I32, (1, n_heads * width), 1) // width
    return jnp.concatenate([jnp.where(lane_head == h, x, 0.0) for h in range(n_heads)], axis=0)


def _head_norm_stacked(o_st, mask_st, n_heads, L, width):
    o_st = jnp.where(mask_st, o_st, 0.0)
    mu = jnp.sum(o_st, axis=-1, keepdims=True) * (1.0 / width)
    xc = jnp.where(mask_st, o_st - mu, 0.0)
    var = jnp.sum(xc * xc, axis=-1, keepdims=True) * (1.0 / width)
    yn = xc * lax.rsqrt(var + LN_EPS)
    y = yn[0:L]
    for h in range(1, n_heads):
        y = y + yn[h * L:(h + 1) * L]
    return y


def _params(sem):
    return pltpu.CompilerParams(dimension_semantics=sem, vmem_limit_bytes=VMEM_LIMIT)


def _inproj_kernel(x_ref, g_ref, b_ref, w_ref, bias_ref, *out_refs, apply_ln, precise):
    x = x_ref[...]
    if apply_ln:
        h_ref, *u_refs = out_refs
        x = _ln(x, g_ref[...], b_ref[...])
        h_ref[...] = x
    else:
        u_refs = out_refs
    xo = x if precise else x.astype(BF16)
    off = 0
    for r, n in zip(u_refs, _GROUP_WIDTHS):
        r[...] = _mm(xo, w_ref[:, off:off + n], precise) + bias_ref[:, off:off + n]
        off += n


def _inproj(x, ln_g, ln_b, w_packed, b_packed, apply_ln, precise):
    n, d = x.shape
    tm = _pick_tile(n, 384)
    dp = w_packed.shape[1]
    row = lambda i: (i, 0)
    fix = lambda i: (0, 0)
    out_shape = [jax.ShapeDtypeStruct((n, w), F32) for w in _GROUP_WIDTHS]
    out_specs = [pl.BlockSpec((tm, w), row) for w in _GROUP_WIDTHS]
    if apply_ln:
        out_shape = [jax.ShapeDtypeStruct((n, d), F32)] + out_shape
        out_specs = [pl.BlockSpec((tm, d), row)] + out_specs
    outs = pl.pallas_call(
        functools.partial(_inproj_kernel, apply_ln=apply_ln, precise=precise),
        grid=(n // tm,),
        in_specs=[pl.BlockSpec((tm, d), row), pl.BlockSpec((1, d), fix), pl.BlockSpec((1, d), fix),
                  pl.BlockSpec((d, dp), fix), pl.BlockSpec((1, dp), fix)],
        out_specs=out_specs, out_shape=out_shape,
        compiler_params=_params(("parallel",)),
        name="inproj_ln" if apply_ln else "inproj",
    )(x, ln_g.reshape(1, d), ln_b.reshape(1, d), w_packed, b_packed)
    if apply_ln:
        return outs[0], outs[1], outs[2], outs[3]
    return x, outs[0], outs[1], outs[2]


def _gla_step(u, wg, bg, gn, st, c, *, L, lo, hi, precise):
    q = u[:, 0:128] * (DK_A ** -0.5)
    k = u[:, 128:256]
    v = u[:, 256:512]
    r = u[:, 512:768]
    g = u[:, 768:896]
    la = _log_sigmoid(_mm(g, wg, precise) + bg) * (1.0 / GATE_TAU)
    pos = c * L + lax.broadcasted_iota(I32, (L, 1), 0)
    real = (pos >= lo) & (pos < hi)
    la = jnp.where(real, la, 0.0)
    k = jnp.where(real, k, 0.0)
    b = _cumsum_rows(la, L)
    bmid = b[L // 2 - 1:L // 2, :]
    bl = b[L - 1:L, :]
    qs = _stack_heads(q * jnp.exp(b - bmid), H_A, DK_A)
    qbs = _stack_heads(q * jnp.exp(b), H_A, DK_A)
    kt = k * jnp.exp(bmid - b)
    kh = k * jnp.exp(bl - b)
    att = _mm(qs, kt, precise, _NT)
    tr = lax.broadcasted_iota(I32, (H_A * L, L), 0) % L
    sc = lax.broadcasted_iota(I32, (H_A * L, L), 1)
    att = jnp.where(sc <= tr, att, 0.0)
    row_head = lax.broadcasted_iota(I32, (H_A * L, W_A), 0) // L
    lane_head = lax.broadcasted_iota(I32, (H_A * L, W_A), 1) // DV_A
    mask_st = row_head == lane_head
    o_st = jnp.where(mask_st, _mm(att, v, precise), 0.0) + _mm(qbs, st, precise, _NT)
    y = _head_norm_stacked(o_st, mask_st, H_A, L, DV_A) * gn * _silu(r)
    bd = (lax.broadcasted_iota(I32, (W_A, H_A * DK_A), 0) // DV_A) == (lax.broadcasted_iota(I32, (W_A, H_A * DK_A), 1) // DK_A)
    return y, st * jnp.exp(bl) + jnp.where(bd, _mm(v, kh, precise, _TN), 0.0)


def _mlstm_step(u, cw, cb, bf, gn, cbd, nrow, m_prev, buf, c, *, L, lo, hi, precise):
    H, D = H_B, DH_B
    pos = c * L + lax.broadcasted_iota(I32, (L, 1), 0)
    real = (pos >= lo) & (pos < hi)
    buf[SUBLANES:SUBLANES + L, :] = jnp.where(real, u[:, 0:2 * W_B], 0.0)
    hist = SUBLANES - (CONV_W - 1)
    conv = buf[hist:hist + L, :] * cw[0:1, :]
    for j in range(1, CONV_W):
        conv = conv + buf[hist + j:hist + j + L, :] * cw[j:j + 1, :]
    conv = conv + cb
    buf[0:SUBLANES, :] = buf[L:L + SUBLANES, :]
    act = _silu(conv)
    q = act[:, 0:W_B]
    k = act[:, W_B:2 * W_B] * (D ** -0.5)
    v = u[:, 512:768]
    og = u[:, 768:1024]
    gts = u[:, 1024:1152]
    ig_col = jnp.where(real, gts, NEG_BIG)
    lf_col = jnp.where(real, _log_sigmoid(gts + bf), 0.0)
    b_col = _cumsum_rows(lf_col, L)
    lane = lax.broadcasted_iota(I32, (1, LANES), 1)
    z = jnp.where(lane < H, ig_col, b_col)
    zp = z if L == LANES else jnp.concatenate([z, jnp.zeros((LANES - L, LANES), F32)], axis=0)
    zt = zp.T
    b_st = jnp.concatenate([b_col[:, H + h:H + h + 1] for h in range(H)], axis=0)
    m0_st = jnp.concatenate([jnp.broadcast_to(m_prev[:, h:h + 1], (L, 1)) for h in range(H)], axis=0)
    rowv_st = jnp.concatenate(
        [jnp.broadcast_to(zt[h:h + 1, 0:L] - zt[H + h:H + h + 1, 0:L], (L, L)) for h in range(H)], axis=0)
    tr = lax.broadcasted_iota(I32, (H * L, L), 0) % L
    sc = lax.broadcasted_iota(I32, (H * L, L), 1)
    logw = jnp.where(sc <= tr, b_st + rowv_st, NEG_BIG)
    lp = b_st + m0_st
    m_st = jnp.maximum(lp, jnp.max(logw, axis=1, keepdims=True))
    w = jnp.exp(logw - m_st)
    sp = jnp.exp(lp - m_st)
    qs = _stack_heads(q, H, D)
    qk = _mm(qs, k, precise, _NT) * w
    row_head = lax.broadcasted_iota(I32, (H * L, W_B), 0) // L
    lane_head = lax.broadcasted_iota(I32, (H * L, W_B), 1) // D
    mask_st = row_head == lane_head
    num = jnp.where(mask_st, _mm(qk, v, precise), 0.0) + sp * _mm(qs, cbd, precise)
    den = jnp.sum(qk, axis=1, keepdims=True) + sp * jnp.sum(qs * nrow, axis=1, keepdims=True)
    hh = num / jnp.maximum(jnp.abs(den), jnp.exp(-m_st))
    y = _head_norm_stacked(hh, mask_st, H, L, D) * gn * _sigmoid(og)
    lh = lax.broadcasted_iota(I32, (1, W_B), 1) // D
    wlk = jnp.zeros((L, W_B), F32)
    sl_lanes = jnp.zeros((1, W_B), F32)
    m_new = jnp.zeros((1, LANES), F32)
    for h in range(H):
        ml = m_st[h * L + L - 1:h * L + L, :]
        bl = b_st[h * L + L - 1:h * L + L, :]
        wl = jnp.exp(bl - b_col[:, H + h:H + h + 1] + ig_col[:, h:h + 1] - ml)
        sl = jnp.exp(bl + m_prev[:, h:h + 1] - ml)
        wlk = jnp.where(lh == h, wl, wlk)
        sl_lanes = jnp.where(lh == h, sl, sl_lanes)
        m_new = jnp.where(lane == h, ml, m_new)
    kw = k * wlk
    bd = (lax.broadcasted_iota(I32, (W_B, W_B), 0) // D) == (lax.broadcasted_iota(I32, (W_B, W_B), 1) // D)
    c_new = sl_lanes * cbd + jnp.where(bd, _mm(kw, v, precise, _TN), 0.0)
    n_new = sl_lanes * nrow + jnp.sum(kw, axis=0, keepdims=True)
    return y, c_new, n_new, m_new


def _scan_kernel(ua_ref, ub_ref, wg_ref, bg_ref, gna_ref, cw_ref, cb_ref, bf_ref, gnb_ref,
                 s0_ref, c0_ref, n0_ref, m0_ref, conv0_ref,
                 ya_ref, yb_ref, sout_ref, cout_ref, nout_ref, mout_ref,
                 st_sc, c_sc, n_sc, m_sc, buf_sc, *, L, lo, hi, precise, bpg):
    c = pl.program_id(1)

    @pl.when(c == 0)
    def _():
        st_sc[...] = s0_ref[...]
        c_sc[...] = c0_ref[...]
        n_sc[...] = n0_ref[...]
        m_sc[...] = m0_ref[...]
        buf_sc[:, 0:SUBLANES, :] = conv0_ref[...]

    for b in range(bpg):
        y, st_new = _gla_step(ua_ref[b], wg_ref[...], bg_ref[...], gna_ref[...], st_sc[b], c,
                              L=L, lo=lo, hi=hi, precise=precise)
        ya_ref[b] = y
        st_sc[b] = st_new
        sout_ref[b] = st_new
    for b in range(bpg):
        y, c_new, n_new, m_new = _mlstm_step(ub_ref[b], cw_ref[...], cb_ref[...], bf_ref[...], gnb_ref[...],
                                             c_sc[b], n_sc[b], m_sc[b], buf_sc.at[b], c,
                                             L=L, lo=lo, hi=hi, precise=precise)
        yb_ref[b] = y
        c_sc[b] = c_new
        n_sc[b] = n_new
        m_sc[b] = m_new
        cout_ref[b] = c_new
        nout_ref[b] = n_new
        mout_ref[b] = m_new


def _scan(ua, ub, wg, bg, gna, cw, cb, bf, gnb, s0t, c0, n0, m0, conv0, *, nb, nc, L, lo, hi, precise):
    bpg = 2
    assert nb % bpg == 0
    rows = ua.shape[0]
    t = rows // nb
    fix2 = lambda g, c: (0, 0)
    tok = lambda g, c: (g, c, 0)
    per_g = lambda g, c: (g, 0, 0)
    kd = H_A * DK_A
    ya, yb, st_out, c_out, n_out, m_out = pl.pallas_call(
        functools.partial(_scan_kernel, L=L, lo=lo, hi=hi, precise=precise, bpg=bpg),
        grid=(nb // bpg, nc),
        in_specs=[pl.BlockSpec((bpg, L, _GROUP_WIDTHS[0]), tok), pl.BlockSpec((bpg, L, _GROUP_WIDTHS[1]), tok),
                  pl.BlockSpec((LANES, LANES), fix2), pl.BlockSpec((1, LANES), fix2), pl.BlockSpec((1, W_A), fix2),
                  pl.BlockSpec((CONV_W, 2 * W_B), fix2), pl.BlockSpec((1, 2 * W_B), fix2),
                  pl.BlockSpec((1, LANES), fix2), pl.BlockSpec((1, W_B), fix2),
                  pl.BlockSpec((bpg, W_A, kd), per_g), pl.BlockSpec((bpg, W_B, W_B), per_g),
                  pl.BlockSpec((bpg, 1, W_B), per_g), pl.BlockSpec((bpg, 1, LANES), per_g),
                  pl.BlockSpec((bpg, SUBLANES, 2 * W_B), per_g)],
        out_specs=[pl.BlockSpec((bpg, L, W_A), tok), pl.BlockSpec((bpg, L, W_B), tok),
                   pl.BlockSpec((bpg, W_A, kd), per_g), pl.BlockSpec((bpg, W_B, W_B), per_g),
                   pl.BlockSpec((bpg, 1, W_B), per_g), pl.BlockSpec((bpg, 1, LANES), per_g)],
        out_shape=[jax.ShapeDtypeStruct((nb, t, W_A), F32), jax.ShapeDtypeStruct((nb, t, W_B), F32),
                   jax.ShapeDtypeStruct((nb, W_A, kd), F32), jax.ShapeDtypeStruct((nb, W_B, W_B), F32),
                   jax.ShapeDtypeStruct((nb, 1, W_B), F32), jax.ShapeDtypeStruct((nb, 1, LANES), F32)],
        scratch_shapes=[pltpu.VMEM((bpg, W_A, kd), F32), pltpu.VMEM((bpg, W_B, W_B), F32),
                        pltpu.VMEM((bpg, 1, W_B), F32), pltpu.VMEM((bpg, 1, LANES), F32),
                        pltpu.VMEM((bpg, L + SUBLANES, 2 * W_B), F32)],
        compiler_params=_params(("parallel", "arbitrary")),
        name="mixer_scans",
    )(ua.reshape(nb, t, -1), ub.reshape(nb, t, -1), wg, bg, gna, cw, cb, bf, gnb, s0t, c0, n0, m0, conv0)
    return ya.reshape(rows, W_A), yb.reshape(rows, W_B), st_out, c_out, n_out, m_out


def _dsa_kernel(u_ref, k_ref, vt_ref, ki_ref, y_ref, sc_sc, pl_sc, act_sc, sa_sc, sb_sc, ba_sc, bb_sc, acc_sc, m_sc, l_sc, y_sc,
                *, KB, nkb_alloc, n_sel, adm_lo, adm_cap, end_base, end_step, end_half, out_rows, out_step, precise):
    b = pl.program_id(0)
    c = pl.program_id(1)
    TQ = LANES
    G = H_C // HKV_C
    lane_q = lax.broadcasted_iota(I32, (1, TQ), 1)
    end_lo = end_base + end_step * c
    end_q = jnp.minimum(jnp.where(lane_q < CHUNK, end_lo, end_lo + end_half), adm_cap)
    end_max = jnp.minimum(end_lo + end_half, adm_cap)
    npair = (end_max + 2 * KB - 1) // (2 * KB)
    u = u_ref[...]
    qi_t = (u[:, 768:896] * (D_IDX ** -0.5)).T
    kw_t = u[:, 896:1024].T
    qi_st = jnp.concatenate([qi_t[h * D_IDX:(h + 1) * D_IDX, :] for h in range(H_IDX)], axis=1)
    if not precise:
        qi_st = qi_st.astype(BF16)
    wi = [kw_t[D_IDX + h:D_IDX + h + 1, :] * (H_IDX ** -0.5) for h in range(H_IDX)]
    key_idx = lax.broadcasted_iota(I32, (KB, TQ), 0)

    def score_block(kb):
        raw = jnp.maximum(_mm(ki_ref[0, kb], qi_st, precise), 0.0)
        isc = wi[0] * raw[:, 0:TQ]
        for h in range(1, H_IDX):
            isc = isc + wi[h] * raw[:, h * TQ:(h + 1) * TQ]
        idx = kb * KB + key_idx
        sc_sc[kb] = jnp.where((idx >= adm_lo) & (idx < end_q), isc, -jnp.inf)

    def score_body(i, carry):
        score_block(2 * i)
        score_block(2 * i + 1)
        return carry

    lax.fori_loop(0, npair, score_body, 0)

    def count(pred_fn):
        def body(i, acc):
            for kb in (2 * i, 2 * i + 1):
                hit = jnp.where(pred_fn(sc_sc[kb]), 1, 0)
                acc = acc + jnp.sum(hit.reshape(KB // SUBLANES, SUBLANES, TQ), axis=0)
            return acc
        acc = lax.fori_loop(0, npair, body, jnp.zeros((SUBLANES, TQ), I32))
        return jnp.sum(acc, axis=0, keepdims=True)

    def key_to_float(key):
        return lax.bitcast_convert_type(jnp.where(key >= 0, key, key ^ 0x7FFFFFFF), F32)

    U32 = jnp.uint32
    nch = (2 * npair + PLANE_CHUNK - 1) // PLANE_CHUNK

    def plane_body(kb, carry):
        bits = lax.bitcast_convert_type(sc_sc[kb], U32)
        flip = jnp.where(bits >= U32(0x80000000), U32(0xFFFFFFFF), U32(0x80000000))
        uk = bits ^ flip
        uk = jnp.where(uk == U32(0x7FFFFFFF), U32(0x80000000), uk)
        a = [uk[SUBLANES * j:SUBLANES * (j + 1), :] for j in range(32)]
        j, m = 16, 0x0000FFFF
        while j:
            k = 0
            while k < 32:
                t = (a[k] ^ (a[k + j] >> U32(j))) & U32(m)
                a[k] = a[k] ^ t
                a[k + j] = a[k + j] ^ (t << U32(j))
                k = (k + j + 1) & ~j
            j >>= 1
            m = (m ^ (m << j)) & 0xFFFFFFFF
        for p in range(32):
            pl_sc[p, kb] = a[p]
        return carry

    lax.fori_loop(0, 2 * npair, plane_body, 0)

    def plane_pad_body(kb, carry):
        for p in range(32):
            pl_sc[p, kb] = jnp.zeros((SUBLANES, TQ), U32)
        return carry

    lax.fori_loop(2 * npair, nch * PLANE_CHUNK, plane_pad_body, 0)

    def chunk(i):
        return pl.ds(pl.multiple_of(i * PLANE_CHUNK, PLANE_CHUNK), PLANE_CHUNK)

    def popsum(words):
        return jnp.sum(lax.population_count(words).astype(I32), axis=0)

    def first_body(i, acc):
        act_sc[chunk(i)] = jnp.full((PLANE_CHUNK, SUBLANES, TQ), 0xFFFFFFFF, U32)
        return acc + popsum(pl_sc[0, chunk(i)])

    cnt1 = jnp.sum(lax.fori_loop(0, nch, first_body, jnp.zeros((SUBLANES, TQ), I32)), axis=0, keepdims=True)

    def radix_body(p, carry):
        rem, ukey, c1 = carry
        take = c1 >= rem

        def body(i, acc):
            act = act_sc[chunk(i)]
            prev = pl_sc[p - 1, chunk(i)]
            act = jnp.where(take, act & prev, act & ~prev)
            act_sc[chunk(i)] = act
            return acc + popsum(act & pl_sc[p, chunk(i)])

        acc = lax.fori_loop(0, nch, body, jnp.zeros((SUBLANES, TQ), I32))
        bit = lax.shift_left(U32(1), jnp.asarray(32 - p, U32))
        return (jnp.where(take, rem, rem - c1), jnp.where(take, ukey | bit, ukey),
                jnp.sum(acc, axis=0, keepdims=True))

    rem, ukey, c1 = lax.fori_loop(1, 32, radix_body,
                                  (jnp.full((1, TQ), n_sel, I32), jnp.zeros((1, TQ), U32), cnt1))
    ukey = jnp.where(c1 >= rem, ukey | U32(1), ukey)
    thr_fast = key_to_float(lax.bitcast_convert_type(ukey ^ U32(0x80000000), I32))
    def check_body(i, accs):
        a_ge, a_gt = accs
        for kb in (2 * i, 2 * i + 1):
            s = sc_sc[kb]
            a_ge = a_ge + jnp.sum(jnp.where(s >= thr_fast, 1, 0).reshape(KB // SUBLANES, SUBLANES, TQ), axis=0)
            a_gt = a_gt + jnp.sum(jnp.where(s > thr_fast, 1, 0).reshape(KB // SUBLANES, SUBLANES, TQ), axis=0)
        return a_ge, a_gt

    zero_cnt = jnp.zeros((SUBLANES, TQ), I32)
    a_ge, a_gt = lax.fori_loop(0, npair, check_body, (zero_cnt, zero_cnt))
    c_ge = jnp.sum(a_ge, axis=0, keepdims=True)
    c_gt = jnp.sum(a_gt, axis=0, keepdims=True)
    ok = jnp.min(jnp.where((c_gt < n_sel) & (c_ge >= n_sel), 1, 0)) == 1

    def float_search():
        def thr_body(i, p):
            cand = p + lax.shift_left(jnp.int32(1), 31 - i)
            cand_f = key_to_float(cand)
            cnt = count(lambda s: s >= cand_f)
            return jnp.where(cnt >= n_sel, cand, p)

        thr_key = lax.fori_loop(0, 32, thr_body, jnp.full((1, TQ), INT_MIN, I32))
        t = key_to_float(jnp.maximum(thr_key, KEY_NEG_INF))
        return t, count(lambda s: s > t)

    thr, n_above = lax.cond(ok, lambda: (thr_fast, c_gt), float_search)
    need = (n_sel - n_above).astype(F32)

    q_t = [(u[:, j * LANES:(j + 1) * LANES] * (DH_C ** -0.5 * LOG2_E)).T for j in range(W_C // LANES)]
    zero_half = jnp.zeros((DH_C, G * TQ), F32)
    qz = []
    for n in range(HKV_C):
        heads = [n * G + g for g in range(G)]
        qn = jnp.concatenate([q_t[hc // 2][(hc % 2) * DH_C:(hc % 2 + 1) * DH_C, :] for hc in heads], axis=1)
        qn = jnp.concatenate([qn, zero_half] if n == 0 else [zero_half, qn], axis=0)
        qz.append(qn if precise else qn.astype(BF16))
    tri = (lax.broadcasted_iota(I32, (KB, KB), 0) >= lax.broadcasted_iota(I32, (KB, KB), 1)).astype(BF16)

    def mask_block(kb, tie_seen, dst):
        s_idx = sc_sc[kb]
        eq = (s_idx == thr) & (s_idx > -jnp.inf)
        rank = tie_seen + jnp.dot(tri, jnp.where(eq, 1.0, 0.0).astype(BF16), preferred_element_type=F32)
        sel = (s_idx > thr) | (eq & (rank <= need))
        dst[...] = jnp.where(sel, 0.0, NEG_BIG)
        return rank[KB - 1:KB, :]

    acc_sc[...] = jnp.zeros_like(acc_sc)
    m_sc[...] = jnp.full_like(m_sc, M_INIT)
    l_sc[...] = jnp.zeros_like(l_sc)
    W2 = 2 * TQ
    units = [(n, slice(gp * W2, (gp + 1) * W2)) for n in range(HKV_C) for gp in range(G // 2)]

    def logits(kb, dst):
        kblk = k_ref[0, kb]
        for ui, (n, cols) in enumerate(units):
            dst[ui] = _mm(kblk, qz[n][:, cols], precise)

    def att_step(kb, tie_seen, cur, nxt, bias_cur, bias_nxt):
        logits(jnp.minimum(kb + 1, nkb_alloc - 1), nxt)
        tie_seen = mask_block(jnp.minimum(kb + 1, 2 * npair - 1), tie_seen, bias_nxt)
        bias = bias_cur[...]
        bias2 = jnp.concatenate([bias, bias], axis=1)
        vtblk = vt_ref[0, kb]
        for ui, (n, cols) in enumerate(units):
            s = cur[ui] + bias2
            m_old = m_sc[n, :, cols]
            m_new = jnp.maximum(m_old, jnp.max(s, axis=0, keepdims=True))
            p = jnp.exp2(s - m_new)
            alpha = jnp.exp2(m_old - m_new)
            pv = _mm(vtblk[n * VT_ROWS:(n + 1) * VT_ROWS, :], p, precise)
            l_sc[n, :, cols] = alpha * l_sc[n, :, cols] + pv[DH_C:DH_C + 1, :]
            acc_sc[n, :, cols] = alpha * acc_sc[n, :, cols] + pv[0:DH_C, :]
            m_sc[n, :, cols] = m_new
        return tie_seen

    logits(0, sa_sc)
    tie0 = mask_block(0, jnp.zeros((1, TQ), F32), ba_sc)

    def att_body(i, tie_seen):
        tie_seen = att_step(2 * i, tie_seen, sa_sc, sb_sc, ba_sc, bb_sc)
        return att_step(2 * i + 1, tie_seen, sb_sc, sa_sc, bb_sc, ba_sc)

    lax.fori_loop(0, npair, att_body, tie0)
    o = [acc_sc[n] / l_sc[n] for n in range(HKV_C)]
    for j in range(W_C // LANES):
        n, g0 = (2 * j) // G, (2 * j) % G
        pair = jnp.concatenate([o[n][:, g0 * TQ:(g0 + 1) * TQ], o[n][:, (g0 + 1) * TQ:(g0 + 2) * TQ]], axis=0)
        y_sc[:, j * LANES:(j + 1) * LANES] = pair.T
    off = pl.multiple_of(b * out_step, SUBLANES)
    y_ref[...] = y_sc[pl.ds(off, out_rows), :]


def _dsa(uc, k_arr, vt_arr, ki_arr, *, nb, nq, n_sel, adm_lo, adm_cap, end_base, end_step, end_half,
         out_rows, out_step, precise):
    rows = uc.shape[0]
    nkb_max, KB = k_arr.shape[1], k_arr.shape[2]
    assert nkb_max % 2 == 0
    n_plane = -(-nkb_max // PLANE_CHUNK) * PLANE_CHUNK
    G = H_C // HKV_C
    per_b = lambda b, c: (b, 0, 0, 0)
    u_map = (lambda b, c: (b * nq + c, 0)) if out_step == 0 else (lambda b, c: (0, 0))
    return pl.pallas_call(
        functools.partial(_dsa_kernel, KB=KB, nkb_alloc=nkb_max, n_sel=n_sel, adm_lo=adm_lo, adm_cap=adm_cap,
                          end_base=end_base, end_step=end_step, end_half=end_half, out_rows=out_rows,
                          out_step=out_step, precise=precise),
        grid=(nb, nq),
        in_specs=[pl.BlockSpec((LANES, _GROUP_WIDTHS[2]), u_map),
                  pl.BlockSpec((1, nkb_max, KB, HKV_C * DH_C), per_b),
                  pl.BlockSpec((1, nkb_max, HKV_C * VT_ROWS, KB), per_b),
                  pl.BlockSpec((1, nkb_max, KB, D_IDX), per_b)],
        out_specs=pl.BlockSpec((out_rows, W_C), lambda b, c: (b * nq + c, 0)),
        out_shape=jax.ShapeDtypeStruct((rows, W_C), F32),
        scratch_shapes=[pltpu.VMEM((nkb_max, KB, LANES), F32),
                        pltpu.VMEM((32, n_plane, SUBLANES, LANES), jnp.uint32),
                        pltpu.VMEM((n_plane, SUBLANES, LANES), jnp.uint32),
                        pltpu.VMEM((HKV_C * (G // 2), KB, 2 * LANES), F32),
                        pltpu.VMEM((HKV_C * (G // 2), KB, 2 * LANES), F32),
                        pltpu.VMEM((KB, LANES), F32), pltpu.VMEM((KB, LANES), F32),
                        pltpu.VMEM((HKV_C, DH_C, G * LANES), F32), pltpu.VMEM((HKV_C, 1, G * LANES), F32),
                        pltpu.VMEM((HKV_C, 1, G * LANES), F32), pltpu.VMEM((LANES, W_C), F32)],
        compiler_params=_params(("parallel", "arbitrary")),
        name="dsa_attend",
    )(uc, k_arr, vt_arr, ki_arr)


def _outproj_kernel(ya_ref, yb_ref, yc_ref, h_ref, w_ref, g_ref, b_ref, o_ref, *, alpha, precise):
    mix = (_mm(ya_ref[...], w_ref[0:W_A, :], precise) + _mm(yb_ref[...], w_ref[W_A:W_A + W_B, :], precise)
           + _mm(yc_ref[...], w_ref[W_A + W_B:, :], precise))
    o_ref[...] = _ln(alpha * h_ref[...] + mix, g_ref[...], b_ref[...])


def _outproj(ya, yb, yc, h, w_out_l, g, b, alpha, precise):
    n, d = h.shape
    tm = _pick_tile(n, 768)
    row = lambda i: (i, 0)
    fix = lambda i: (0, 0)
    return pl.pallas_call(
        functools.partial(_outproj_kernel, alpha=alpha, precise=precise),
        grid=(n // tm,),
        in_specs=[pl.BlockSpec((tm, W_A), row), pl.BlockSpec((tm, W_B), row), pl.BlockSpec((tm, W_C), row),
                  pl.BlockSpec((tm, d), row), pl.BlockSpec(w_out_l.shape, fix),
                  pl.BlockSpec((1, d), fix), pl.BlockSpec((1, d), fix)],
        out_specs=pl.BlockSpec((tm, d), row),
        out_shape=jax.ShapeDtypeStruct((n, d), F32),
        compiler_params=_params(("parallel",)),
        name="outproj_ln1",
    )(ya, yb, yc, h, w_out_l, g.reshape(1, d), b.reshape(1, d))


def _moe_kernel(h_ref, wr_ref, br_ref, wg_ref, wu_ref, wd_ref, g_ref, b_ref, o_ref, acc_sc, gate_sc, *, alpha, precise):
    grp = pl.program_id(1)
    x = h_ref[...]
    tm = x.shape[0]
    xo = x if precise else x.astype(BF16)
    eg = N_EXPERTS // N_GROUPS

    @pl.when(grp == 0)
    def _():
        s = _sigmoid(_mm(wr_ref[...], xo, precise, _NT))
        sb = s + br_ref[...]
        srow = [s[j:j + 1, :] for j in range(N_EXPERTS)]
        brow = [sb[j:j + 1, :] for j in range(N_EXPERTS)]
        gscore = []
        for g in range(N_GROUPS):
            best = None
            for i in range(eg):
                for j in range(i + 1, eg):
                    pair = brow[g * eg + i] + brow[g * eg + j]
                    best = pair if best is None else jnp.maximum(best, pair)
            gscore.append(best)
        gmax = functools.reduce(jnp.maximum, gscore)
        taken = jnp.zeros_like(gmax) > 1.0
        sel = [None] * N_EXPERTS
        for g in range(N_GROUPS):
            is_g = jnp.logical_and(jnp.logical_not(taken), gscore[g] == gmax)
            taken = jnp.logical_or(taken, is_g)
            for i in range(eg):
                a = brow[g * eg + i]
                rank = jnp.zeros(a.shape, I32)
                for j in range(eg):
                    if j == i:
                        continue
                    o = brow[g * eg + j]
                    ahead = (o >= a) if j < i else (o > a)
                    rank = rank + jnp.where(ahead, 1, 0)
                sel[g * eg + i] = jnp.logical_and(is_g, rank < TOP_K)
        denom = jnp.zeros_like(gmax)
        for j in range(N_EXPERTS):
            denom = denom + jnp.where(sel[j], srow[j], 0.0)
        gates = [jnp.where(sel[j], srow[j] / denom, 0.0) for j in range(N_EXPERTS)]
        gt = jnp.concatenate(gates + [jnp.zeros((LANES - N_EXPERTS, tm), F32)], axis=0)
        gate_sc[...] = gt.T
        acc_sc[...] = jnp.zeros_like(acc_sc)

    lane = lax.broadcasted_iota(I32, (1, LANES), 1)
    gate = gate_sc[...]
    hes = []
    for j in range(eg):
        gcol = jnp.sum(jnp.where(lane == grp * eg + j, gate, 0.0), axis=1, keepdims=True)
        hes.append(gcol * (_silu(_mm(xo, wg_ref[j], precise)) * _mm(xo, wu_ref[j], precise)))
    acc_sc[...] += _mm(jnp.concatenate(hes, axis=1), wd_ref[0], precise)

    @pl.when(grp == N_GROUPS - 1)
    def _():
        o_ref[...] = _ln(alpha * x + acc_sc[...], g_ref[...], b_ref[...])


def _moe(h, wr_t, br_col, w_gate, w_up, w_down, g, b, alpha, precise):
    n, d = h.shape
    tm = _pick_tile(n, MOE_TILE)
    assert tm % LANES == 0, tm
    eg = N_EXPERTS // N_GROUPS
    row = lambda i, e: (i, 0)
    fix = lambda i, e: (0, 0)
    per_g = lambda i, e: (e, 0, 0)
    return pl.pallas_call(
        functools.partial(_moe_kernel, alpha=alpha, precise=precise),
        grid=(n // tm, N_GROUPS),
        in_specs=[pl.BlockSpec((tm, d), row), pl.BlockSpec((LANES, d), fix), pl.BlockSpec((LANES, 1), fix),
                  pl.BlockSpec((eg, d, D_EXPERT), per_g), pl.BlockSpec((eg, d, D_EXPERT), per_g),
                  pl.BlockSpec((1, eg * D_EXPERT, d), per_g), pl.BlockSpec((1, d), fix), pl.BlockSpec((1, d), fix)],
        out_specs=pl.BlockSpec((tm, d), row),
        out_shape=jax.ShapeDtypeStruct((n, d), F32),
        scratch_shapes=[pltpu.VMEM((tm, d), F32), pltpu.VMEM((tm, LANES), F32)],
        compiler_params=_params(("parallel", "arbitrary")),
        name="moe_ln2",
    )(h, wr_t, br_col, w_gate, w_up, w_down.reshape(N_GROUPS, eg * D_EXPERT, d), g.reshape(1, d), b.reshape(1, d))


def _pack_in_weights(w_in_l, b_in_l):
    offs = {}
    off = 0
    for name, n in _SPLITS:
        offs[name] = (off, n)
        off += n
    wcols, bcols = [], []
    for grp in (_GROUP_A, _GROUP_B, _GROUP_C):
        for name, n in grp:
            if name is None:
                wcols.append(jnp.zeros((w_in_l.shape[0], n), w_in_l.dtype))
                bcols.append(jnp.zeros((n,), b_in_l.dtype))
            else:
                o, m = offs[name]
                assert m == n
                wcols.append(w_in_l[:, o:o + n])
                bcols.append(b_in_l[o:o + n])
    return jnp.concatenate(wcols, axis=1), jnp.concatenate(bcols).reshape(1, -1).astype(F32)


def _block_diag_embed(x):
    B, H, r, c = x.shape
    eye = jnp.eye(H, dtype=x.dtype)
    return (x[:, :, :, None, :] * eye[None, :, None, :, None]).reshape(B, H * r, H * c)


def _block_diag_extract(x, H):
    B, R, C = x.shape
    r, c = R // H, C // H
    x5 = x.reshape(B, H, r, H, c)
    return jnp.stack([x5[:, h, :, h, :] for h in range(H)], axis=1)


def _mixer_states_in(s_gla, s_c, s_n, s_m, s_conv):
    nb = s_gla.shape[0]
    s0t = _block_diag_embed(jnp.swapaxes(s_gla.astype(F32), -1, -2))
    c0 = _block_diag_embed(jnp.swapaxes(s_c.astype(F32), -1, -2))
    n0 = s_n.astype(F32).reshape(nb, 1, W_B)
    m0 = jnp.pad(s_m.astype(F32), ((0, 0), (0, LANES - H_B))).reshape(nb, 1, LANES)
    conv0 = jnp.pad(s_conv.astype(F32), ((0, 0), (SUBLANES - (CONV_W - 1), 0), (0, 0)))
    return s0t, c0, n0, m0, conv0


def _key_blocks(k, v, ki, dtype):
    nb, s, _ = k.shape
    s_even = -(-s // (2 * KEY_BLOCK)) * 2 * KEY_BLOCK
    k, v, ki = (jnp.pad(a, ((0, 0), (0, s_even - s), (0, 0))) for a in (k, v, ki))
    nkb = s_even // KEY_BLOCK
    kk = k.astype(dtype).reshape(nb, nkb, KEY_BLOCK, -1)
    vt = jnp.swapaxes(v.astype(dtype).reshape(nb, nkb, KEY_BLOCK, HKV_C, DH_C), 2, 4)
    vt = jnp.moveaxis(vt, 3, 2)
    ones = jnp.ones((nb, nkb, HKV_C, VT_ROWS - DH_C, KEY_BLOCK), dtype)
    vt = jnp.concatenate([vt, ones], axis=3).reshape(nb, nkb, HKV_C * VT_ROWS, KEY_BLOCK)
    kib = ki.astype(dtype).reshape(nb, nkb, KEY_BLOCK, -1)
    return kk, vt, kib


def _layer(h, stream, l, p):
    precise = stream['precise']
    wdt = F32 if precise else BF16
    (w_in_p, b_in_p, wg, bg, gn_a, cw, cb, bf, gn_b, w_out_l, ln1_g, ln1_b, wr_t, br_col,
     w_gate, w_up, w_down, ln2_g, ln2_b, ln_in_g, ln_in_b, alpha) = p
    h, ua, ub, uc = _inproj(h, ln_in_g, ln_in_b, w_in_p.astype(wdt), b_in_p, apply_ln=(l == 0), precise=precise)
    nb, nc, L, lo, hi = stream['nb'], stream['nc'], stream['L'], stream['lo'], stream['hi']
    s0t, c0, n0, m0, conv0 = stream['states'](l)
    ya, yb, st_out, c_out, n_out, m_out = _scan(ua, ub, wg.astype(wdt), bg, gn_a, cw, cb, bf, gn_b, s0t, c0, n0, m0,
                                                conv0, nb=nb, nc=nc, L=L, lo=lo, hi=hi, precise=precise)
    k_arr, vt_arr, ki_arr = stream['keys'](l, uc, wdt)
    yc = _dsa(uc, k_arr, vt_arr, ki_arr, nb=nb, precise=precise, **stream['dsa'])
    h1 = _outproj(ya, yb, yc, h, w_out_l.astype(wdt), ln1_g, ln1_b, alpha, precise)
    h2 = _moe(h1, wr_t.astype(wdt), br_col, w_gate.astype(wdt), w_up.astype(wdt), w_down.astype(wdt), ln2_g, ln2_b,
              alpha, precise)
    S = jnp.swapaxes(_block_diag_extract(st_out, H_A), -1, -2)
    C = jnp.swapaxes(_block_diag_extract(c_out, H_B), -1, -2)
    return h2, (ub, uc, S, C, n_out.reshape(nb, H_B, DH_B), m_out[:, 0, :H_B])


def kernel(x_prompt, x_sample, cache_k, cache_v, cache_kidx, state_gla, state_mlstm_c, state_mlstm_n, state_mlstm_m, state_conv, meta_tokens, ln_in_g, ln_in_b, w_in, b_in, w_gla_gate, b_gla_gate, g_gla_norm, conv_w, conv_b, b_forget, g_mlstm_norm, w_out, ln1_g, ln1_b, w_router, b_router, w_gate, w_up, w_down, ln2_g, ln2_b):
    depth = w_in.shape[0]
    alpha = float((2 * depth) ** 0.25)
    B, S_p, D = x_prompt.shape
    DB, Ts, _ = x_sample.shape
    P = cache_k.shape[2]
    assert S_p % CHUNK == 0 and CHUNK % Ts == 0 and P % CHUNK == 0 and Ts >= CONV_W - 1
    assert DB * Ts == LANES and Ts % SUBLANES == 0
    KB = KEY_BLOCK

    wr_t = jnp.pad(w_router.T, ((0, LANES - N_EXPERTS), (0, 0)))
    br_col = jnp.pad(b_router.astype(F32), (0, LANES - N_EXPERTS)).reshape(LANES, 1)

    def layer_params(l):
        w_in_p, b_in_p = _pack_in_weights(w_in[l], b_in[l])
        wg = jnp.pad(w_gla_gate[l], ((0, LANES - GATE_RANK), (0, 0)))
        bf = jnp.pad(b_forget[l].astype(F32), (H_B, LANES - 2 * H_B)).reshape(1, LANES)
        return (w_in_p, b_in_p, wg, b_gla_gate[l].reshape(1, -1), g_gla_norm[l].reshape(1, -1),
                conv_w[l], conv_b[l].reshape(1, -1), bf, g_mlstm_norm[l].reshape(1, -1), w_out[l],
                ln1_g[l], ln1_b[l], wr_t, br_col, w_gate[l], w_up[l], w_down[l], ln2_g[l], ln2_b[l],
                ln_in_g, ln_in_b, alpha)

    params = [layer_params(l) for l in range(depth)]

    front = CHUNK - N_META
    t_real = N_META + S_p
    t_pad = -(-(front + t_real) // KB) * KB
    hp = jnp.concatenate([jnp.zeros((B, front, D), x_prompt.dtype),
                          jnp.broadcast_to(meta_tokens.astype(x_prompt.dtype)[None], (B, N_META, D)), x_prompt,
                          jnp.zeros((B, t_pad - front - t_real, D), x_prompt.dtype)], axis=1).reshape(B * t_pad, D)

    def prompt_keys(l, uc, dtype):
        u3 = uc.reshape(B, t_pad, -1)
        return _key_blocks(u3[:, :, 512:640], u3[:, :, 640:768], u3[:, :, 896:896 + D_IDX], dtype)

    zero_states = _mixer_states_in(jnp.zeros((B, H_A, DK_A, DV_A), F32), jnp.zeros((B, H_B, DH_B, DH_B), F32),
                                   jnp.zeros((B, H_B, DH_B), F32), jnp.zeros((B, H_B), F32),
                                   jnp.zeros((B, CONV_W - 1, 2 * W_B), F32))
    prompt_stream = dict(nb=B, nc=t_pad // SCAN_CHUNK, L=SCAN_CHUNK, lo=front, hi=front + t_real, precise=False,
                         states=lambda l: zero_states, keys=prompt_keys,
                         dsa=dict(nq=t_pad // LANES, n_sel=min(TOPK_KEYS, S_p // 4), adm_lo=front,
                                  adm_cap=front + t_real, end_base=CHUNK, end_step=2 * CHUNK, end_half=CHUNK,
                                  out_rows=LANES, out_step=0))
    p_out = [[] for _ in range(8)]
    for l in range(depth):
        hp, (ub, uc, S, C, n_o, m_o) = _layer(hp, prompt_stream, l, params[l])
        ub3 = ub.reshape(B, t_pad, -1)[:, front:front + t_real]
        uc3 = uc.reshape(B, t_pad, -1)[:, front:front + t_real]
        vals = (uc3[:, :, 512:640].reshape(B, t_real, HKV_C, DH_C), uc3[:, :, 640:768].reshape(B, t_real, HKV_C, DH_C),
                uc3[:, :, 896:896 + D_IDX], S, C, n_o, m_o, ub3[:, t_real - (CONV_W - 1):, 0:2 * W_B])
        for acc, a in zip(p_out, vals):
            acc.append(a)
    y_prompt = hp.reshape(B, t_pad, D)[:, front + N_META:front + t_real]

    s_tot = P + Ts
    s_pad = -(-s_tot // KB) * KB

    def sample_keys(l, uc, dtype):
        u3 = uc.reshape(DB, Ts, -1)

        def cat(cache, new):
            a = jnp.concatenate([cache.astype(F32).reshape(DB, P, -1), new], axis=1)
            return jnp.pad(a, ((0, 0), (0, s_pad - s_tot), (0, 0)))

        return _key_blocks(cat(cache_k[l], u3[:, :, 512:640]), cat(cache_v[l], u3[:, :, 640:768]),
                           cat(cache_kidx[l], u3[:, :, 896:896 + D_IDX]), dtype)

    sample_stream = dict(nb=DB, nc=1, L=Ts, lo=0, hi=Ts, precise=True,
                         states=lambda l: _mixer_states_in(state_gla[l], state_mlstm_c[l], state_mlstm_n[l],
                                                           state_mlstm_m[l], state_conv[l]),
                         keys=sample_keys,
                         dsa=dict(nq=1, n_sel=min(TOPK_KEYS, s_tot // 4), adm_lo=0, adm_cap=s_tot, end_base=s_tot,
                                  end_step=0, end_half=0, out_rows=Ts, out_step=Ts))
    hs = x_sample.reshape(DB * Ts, D)
    s_out = [[] for _ in range(8)]
    for l in range(depth):
        hs, (ub, uc, S, C, n_o, m_o) = _layer(hs, sample_stream, l, params[l])
        ub3 = ub.reshape(DB, Ts, -1)
        uc3 = uc.reshape(DB, Ts, -1)
        vals = (uc3[:, :, 512:640].reshape(DB, Ts, HKV_C, DH_C), uc3[:, :, 640:768].reshape(DB, Ts, HKV_C, DH_C),
                uc3[:, :, 896:896 + D_IDX], S, C, n_o, m_o, ub3[:, Ts - (CONV_W - 1):, 0:2 * W_B])
        for acc, a in zip(s_out, vals):
            acc.append(a)
    y_sample = hs.reshape(DB, Ts, D)

    return (y_prompt, y_sample, *[jnp.stack(a) for a in p_out], *[jnp.stack(a) for a in s_out])
```

```python
import functools

import jax
import jax.numpy as jnp
from jax import lax
from jax.experimental import pallas as pl
from jax.experimental.pallas import tpu as pltpu

F32 = jnp.float32
BF16 = jnp.bfloat16
I32 = jnp.int32

CHUNK = 64
N_META = 16
H_A, DK_A, DV_A = 4, 32, 64
GATE_RANK, GATE_TAU = 16, 16.0
H_B, DH_B, CONV_W = 4, 64, 4
H_C, HKV_C, DH_C = 8, 2, 64
H_IDX, D_IDX = 4, 32
TOPK_KEYS = 256
N_EXPERTS, N_GROUPS, TOP_K, D_EXPERT = 16, 4, 2, 256
W_A, W_B, W_C = H_A * DV_A, H_B * DH_B, H_C * DH_C
LN_EPS = 1e-5

LANES = 128
SUBLANES = 8
VMEM_LIMIT = 56 * 1024 * 1024
KEY_BLOCK = 256
MOE_TILE = 768
PLANE_CHUNK = 8
SCAN_CHUNK = 128
NEG_BIG = -1e30
M_INIT = -1e29
INT_MIN = -2 ** 31
LOG2_E = 1.4426950408889634
KEY_NEG_INF = INT_MIN + 0x7FFFFF

_SPLITS = (
    ('a_q', H_A * DK_A), ('a_k', H_A * DK_A), ('a_v', W_A), ('a_g', GATE_RANK), ('a_r', W_A),
    ('b_qk', 2 * W_B), ('b_v', W_B), ('b_i', H_B), ('b_f', H_B), ('b_o', W_B),
    ('c_q', W_C), ('c_k', HKV_C * DH_C), ('c_v', HKV_C * DH_C),
    ('c_qi', H_IDX * D_IDX), ('c_w', H_IDX), ('c_ki', D_IDX),
)
_GROUP_A = (('a_q', 128), ('a_k', 128), ('a_v', 256), ('a_r', 256), ('a_g', 16), (None, 112))
_GROUP_B = (('b_qk', 512), ('b_v', 256), ('b_o', 256), ('b_i', 4), ('b_f', 4), (None, 120))
_GROUP_C = (('c_q', 512), ('c_k', 128), ('c_v', 128), ('c_qi', 128), ('c_ki', 32), ('c_w', 4), (None, 92))
_GROUP_WIDTHS = tuple(sum(n for _, n in g) for g in (_GROUP_A, _GROUP_B, _GROUP_C))

_NN = (((1,), (0,)), ((), ()))
_NT = (((1,), (1,)), ((), ()))
_TN = (((0,), (0,)), ((), ()))


def _mm(a, b, precise, dims=_NN):
    if precise:
        return lax.dot_general(a.astype(F32), b.astype(F32), dims, precision=lax.Precision.HIGHEST,
                               preferred_element_type=F32)
    return lax.dot_general(a.astype(BF16), b.astype(BF16), dims, preferred_element_type=F32)


def _log_sigmoid(x):
    return jnp.minimum(x, 0.0) - jnp.log(1.0 + jnp.exp(-jnp.abs(x)))


def _sigmoid(x):
    return 1.0 / (1.0 + jnp.exp(-x))


def _silu(x):
    return x * _sigmoid(x)


def _ln(x, g, b):
    mu = jnp.mean(x, axis=-1, keepdims=True)
    xc = x - mu
    var = jnp.mean(xc * xc, axis=-1, keepdims=True)
    return xc * lax.rsqrt(var + LN_EPS) * g + b


def _pick_tile(n, cap):
    best = None
    for t in range(SUBLANES, min(n, cap) + 1, SUBLANES):
        if n % t == 0:
            best = t
    assert best is not None, n
    return best


def _stack_heads(x, n_heads, width):
    lane_head = lax.broadcasted_iota(I32, (1, n_heads * width), 1) // width
    return jnp.concatenate([jnp.where(lane_head == h, x, 0.0) for h in range(n_heads)], axis=0)


def _head_norm_stacked(o_st, mask_st, n_heads, L, width):
    o_st = jnp.where(mask_st, o_st, 0.0)
    mu = jnp.sum(o_st, axis=-1, keepdims=True) * (1.0 / width)
    xc = jnp.where(mask_st, o_st - mu, 0.0)
    var = jnp.sum(xc * xc, axis=-1, keepdims=True) * (1.0 / width)
    yn = xc * lax.rsqrt(var + LN_EPS)
    y = yn[0:L]
    for h in range(1, n_heads):
        y = y + yn[h * L:(h + 1) * L]
    return y


def _params(sem):
    return pltpu.CompilerParams(dimension_semantics=sem, vmem_limit_bytes=VMEM_LIMIT)


def _inproj_kernel(x_ref, g_ref, b_ref, w_ref, bias_ref, *out_refs, apply_ln, precise):
    x = x_ref[...]
    if apply_ln:
        h_ref, *u_refs = out_refs
        x = _ln(x, g_ref[...], b_ref[...])
        h_ref[...] = x
    else:
        u_refs = out_refs
    xo = x if precise else x.astype(BF16)
    off = 0
    for r, n in zip(u_refs, _GROUP_WIDTHS):
        r[...] = _mm(xo, w_ref[:, off:off + n], precise) + bias_ref[:, off:off + n]
        off += n


def _inproj(x, ln_g, ln_b, w_packed, b_packed, apply_ln, precise):
    n, d = x.shape
    tm = _pick_tile(n, 384)
    dp = w_packed.shape[1]
    row = lambda i: (i, 0)
    fix = lambda i: (0, 0)
    out_shape = [jax.ShapeDtypeStruct((n, w), F32) for w in _GROUP_WIDTHS]
    out_specs = [pl.BlockSpec((tm, w), row) for w in _GROUP_WIDTHS]
    if apply_ln:
        out_shape = [jax.ShapeDtypeStruct((n, d), F32)] + out_shape
        out_specs = [pl.BlockSpec((tm, d), row)] + out_specs
    outs = pl.pallas_call(
        functools.partial(_inproj_kernel, apply_ln=apply_ln, precise=precise),
        grid=(n // tm,),
        in_specs=[pl.BlockSpec((tm, d), row), pl.BlockSpec((1, d), fix), pl.BlockSpec((1, d), fix),
                  pl.BlockSpec((d, dp), fix), pl.BlockSpec((1, dp), fix)],
        out_specs=out_specs, out_shape=out_shape,
        compiler_params=_params(("parallel",)),
        name="inproj_ln" if apply_ln else "inproj",
    )(x, ln_g.reshape(1, d), ln_b.reshape(1, d), w_packed, b_packed)
    if apply_ln:
        return outs[0], outs[1], outs[2], outs[3]
    return x, outs[0], outs[1], outs[2]


def _gla_step(u, wg, bg, gn, st, c, *, L, lo, hi, precise):
    q = u[:, 0:128] * (DK_A ** -0.5)
    k = u[:, 128:256]
    v = u[:, 256:512]
    r = u[:, 512:768]
    g = u[:, 768:896]
    la = _log_sigmoid(_mm(g, wg, precise) + bg) * (1.0 / GATE_TAU)
    pos = c * L + lax.broadcasted_iota(I32, (L, 1), 0)
    real = (pos >= lo) & (pos < hi)
    la = jnp.where(real, la, 0.0)
    k = jnp.where(real, k, 0.0)
    ti = lax.broadcasted_iota(I32, (L, L), 0)
    si = lax.broadcasted_iota(I32, (L, L), 1)
    b = _mm((ti >= si).astype(F32), la, True)
    bmid = b[L // 2 - 1:L // 2, :]
    bl = b[L - 1:L, :]
    qs = _stack_heads(q * jnp.exp(b - bmid), H_A, DK_A)
    qbs = _stack_heads(q * jnp.exp(b), H_A, DK_A)
    kt = k * jnp.exp(bmid - b)
    kh = k * jnp.exp(bl - b)
    att = _mm(qs, kt, precise, _NT)
    tr = lax.broadcasted_iota(I32, (H_A * L, L), 0) % L
    sc = lax.broadcasted_iota(I32, (H_A * L, L), 1)
    att = jnp.where(sc <= tr, att, 0.0)
    row_head = lax.broadcasted_iota(I32, (H_A * L, W_A), 0) // L
    lane_head = lax.broadcasted_iota(I32, (H_A * L, W_A), 1) // DV_A
    mask_st = row_head == lane_head
    o_st = jnp.where(mask_st, _mm(att, v, precise), 0.0) + _mm(qbs, st, precise, _NT)
    y = _head_norm_stacked(o_st, mask_st, H_A, L, DV_A) * gn * _silu(r)
    bd = (lax.broadcasted_iota(I32, (W_A, H_A * DK_A), 0) // DV_A) == (lax.broadcasted_iota(I32, (W_A, H_A * DK_A), 1) // DK_A)
    return y, st * jnp.exp(bl) + jnp.where(bd, _mm(v, kh, precise, _TN), 0.0)


def _mlstm_step(u, cw, cb, bf, gn, cbd, nrow, m_prev, buf, c, *, L, lo, hi, precise):
    H, D = H_B, DH_B
    pos = c * L + lax.broadcasted_iota(I32, (L, 1), 0)
    real = (pos >= lo) & (pos < hi)
    buf[SUBLANES:SUBLANES + L, :] = jnp.where(real, u[:, 0:2 * W_B], 0.0)
    hist = SUBLANES - (CONV_W - 1)
    conv = buf[hist:hist + L, :] * cw[0:1, :]
    for j in range(1, CONV_W):
        conv = conv + buf[hist + j:hist + j + L, :] * cw[j:j + 1, :]
    conv = conv + cb
    buf[0:SUBLANES, :] = buf[L:L + SUBLANES, :]
    act = _silu(conv)
    q = act[:, 0:W_B]
    k = act[:, W_B:2 * W_B] * (D ** -0.5)
    v = u[:, 512:768]
    og = u[:, 768:1024]
    gts = u[:, 1024:1152]
    ig_col = jnp.where(real, gts, NEG_BIG)
    lf_col = jnp.where(real, _log_sigmoid(gts + bf), 0.0)
    ti = lax.broadcasted_iota(I32, (L, L), 0)
    si = lax.broadcasted_iota(I32, (L, L), 1)
    b_col = _mm((ti >= si).astype(F32), lf_col, True)
    lane = lax.broadcasted_iota(I32, (1, LANES), 1)
    z = jnp.where(lane < H, ig_col, b_col)
    zp = z if L == LANES else jnp.concatenate([z, jnp.zeros((LANES - L, LANES), F32)], axis=0)
    zt = zp.T
    b_st = jnp.concatenate([b_col[:, H + h:H + h + 1] for h in range(H)], axis=0)
    m0_st = jnp.concatenate([jnp.broadcast_to(m_prev[:, h:h + 1], (L, 1)) for h in range(H)], axis=0)
    rowv_st = jnp.concatenate(
        [jnp.broadcast_to(zt[h:h + 1, 0:L] - zt[H + h:H + h + 1, 0:L], (L, L)) for h in range(H)], axis=0)
    tr = lax.broadcasted_iota(I32, (H * L, L), 0) % L
    sc = lax.broadcasted_iota(I32, (H * L, L), 1)
    logw = jnp.where(sc <= tr, b_st + rowv_st, NEG_BIG)
    lp = b_st + m0_st
    m_st = jnp.maximum(lp, jnp.max(logw, axis=1, keepdims=True))
    w = jnp.exp(logw - m_st)
    sp = jnp.exp(lp - m_st)
    qs = _stack_heads(q, H, D)
    qk = _mm(qs, k, precise, _NT) * w
    row_head = lax.broadcasted_iota(I32, (H * L, W_B), 0) // L
    lane_head = lax.broadcasted_iota(I32, (H * L, W_B), 1) // D
    mask_st = row_head == lane_head
    num = jnp.where(mask_st, _mm(qk, v, precise), 0.0) + sp * _mm(qs, cbd, precise)
    den = jnp.sum(qk, axis=1, keepdims=True) + sp * jnp.sum(qs * nrow, axis=1, keepdims=True)
    hh = num / jnp.maximum(jnp.abs(den), jnp.exp(-m_st))
    y = _head_norm_stacked(hh, mask_st, H, L, D) * gn * _sigmoid(og)
    lh = lax.broadcasted_iota(I32, (1, W_B), 1) // D
    wlk = jnp.zeros((L, W_B), F32)
    sl_lanes = jnp.zeros((1, W_B), F32)
    m_new = jnp.zeros((1, LANES), F32)
    for h in range(H):
        ml = m_st[h * L + L - 1:h * L + L, :]
        bl = b_st[h * L + L - 1:h * L + L, :]
        wl = jnp.exp(bl - b_col[:, H + h:H + h + 1] + ig_col[:, h:h + 1] - ml)
        sl = jnp.exp(bl + m_prev[:, h:h + 1] - ml)
        wlk = jnp.where(lh == h, wl, wlk)
        sl_lanes = jnp.where(lh == h, sl, sl_lanes)
        m_new = jnp.where(lane == h, ml, m_new)
    kw = k * wlk
    bd = (lax.broadcasted_iota(I32, (W_B, W_B), 0) // D) == (lax.broadcasted_iota(I32, (W_B, W_B), 1) // D)
    c_new = sl_lanes * cbd + jnp.where(bd, _mm(kw, v, precise, _TN), 0.0)
    n_new = sl_lanes * nrow + jnp.sum(kw, axis=0, keepdims=True)
    return y, c_new, n_new, m_new


def _scan_kernel(ua_ref, ub_ref, wg_ref, bg_ref, gna_ref, cw_ref, cb_ref, bf_ref, gnb_ref,
                 s0_ref, c0_ref, n0_ref, m0_ref, conv0_ref,
                 ya_ref, yb_ref, sout_ref, cout_ref, nout_ref, mout_ref,
                 st_sc, c_sc, n_sc, m_sc, buf_sc, *, L, lo, hi, precise, bpg):
    c = pl.program_id(1)

    @pl.when(c == 0)
    def _():
        st_sc[...] = s0_ref[...]
        c_sc[...] = c0_ref[...]
        n_sc[...] = n0_ref[...]
        m_sc[...] = m0_ref[...]
        buf_sc[:, 0:SUBLANES, :] = conv0_ref[...]

    for b in range(bpg):
        y, st_new = _gla_step(ua_ref[b], wg_ref[...], bg_ref[...], gna_ref[...], st_sc[b], c,
                              L=L, lo=lo, hi=hi, precise=precise)
        ya_ref[b] = y
        st_sc[b] = st_new
        sout_ref[b] = st_new
    for b in range(bpg):
        y, c_new, n_new, m_new = _mlstm_step(ub_ref[b], cw_ref[...], cb_ref[...], bf_ref[...], gnb_ref[...],
                                             c_sc[b], n_sc[b], m_sc[b], buf_sc.at[b], c,
                                             L=L, lo=lo, hi=hi, precise=precise)
        yb_ref[b] = y
        c_sc[b] = c_new
        n_sc[b] = n_new
        m_sc[b] = m_new
        cout_ref[b] = c_new
        nout_ref[b] = n_new
        mout_ref[b] = m_new


def _scan(ua, ub, wg, bg, gna, cw, cb, bf, gnb, s0t, c0, n0, m0, conv0, *, nb, nc, L, lo, hi, precise):
    bpg = 2
    assert nb % bpg == 0
    rows = ua.shape[0]
    t = rows // nb
    fix2 = lambda g, c: (0, 0)
    tok = lambda g, c: (g, c, 0)
    per_g = lambda g, c: (g, 0, 0)
    kd = H_A * DK_A
    ya, yb, st_out, c_out, n_out, m_out = pl.pallas_call(
        functools.partial(_scan_kernel, L=L, lo=lo, hi=hi, precise=precise, bpg=bpg),
        grid=(nb // bpg, nc),
        in_specs=[pl.BlockSpec((bpg, L, _GROUP_WIDTHS[0]), tok), pl.BlockSpec((bpg, L, _GROUP_WIDTHS[1]), tok),
                  pl.BlockSpec((LANES, LANES), fix2), pl.BlockSpec((1, LANES), fix2), pl.BlockSpec((1, W_A), fix2),
                  pl.BlockSpec((CONV_W, 2 * W_B), fix2), pl.BlockSpec((1, 2 * W_B), fix2),
                  pl.BlockSpec((1, LANES), fix2), pl.BlockSpec((1, W_B), fix2),
                  pl.BlockSpec((bpg, W_A, kd), per_g), pl.BlockSpec((bpg, W_B, W_B), per_g),
                  pl.BlockSpec((bpg, 1, W_B), per_g), pl.BlockSpec((bpg, 1, LANES), per_g),
                  pl.BlockSpec((bpg, SUBLANES, 2 * W_B), per_g)],
        out_specs=[pl.BlockSpec((bpg, L, W_A), tok), pl.BlockSpec((bpg, L, W_B), tok),
                   pl.BlockSpec((bpg, W_A, kd), per_g), pl.BlockSpec((bpg, W_B, W_B), per_g),
                   pl.BlockSpec((bpg, 1, W_B), per_g), pl.BlockSpec((bpg, 1, LANES), per_g)],
        out_shape=[jax.ShapeDtypeStruct((nb, t, W_A), F32), jax.ShapeDtypeStruct((nb, t, W_B), F32),
                   jax.ShapeDtypeStruct((nb, W_A, kd), F32), jax.ShapeDtypeStruct((nb, W_B, W_B), F32),
                   jax.ShapeDtypeStruct((nb, 1, W_B), F32), jax.ShapeDtypeStruct((nb, 1, LANES), F32)],
        scratch_shapes=[pltpu.VMEM((bpg, W_A, kd), F32), pltpu.VMEM((bpg, W_B, W_B), F32),
                        pltpu.VMEM((bpg, 1, W_B), F32), pltpu.VMEM((bpg, 1, LANES), F32),
                        pltpu.VMEM((bpg, L + SUBLANES, 2 * W_B), F32)],
        compiler_params=_params(("parallel", "arbitrary")),
        name="mixer_scans",
    )(ua.reshape(nb, t, -1), ub.reshape(nb, t, -1), wg, bg, gna, cw, cb, bf, gnb, s0t, c0, n0, m0, conv0)
    return ya.reshape(rows, W_A), yb.reshape(rows, W_B), st_out, c_out, n_out, m_out


def _dsa_kernel(u_ref, k_ref, vt_ref, ki_ref, y_ref, sc_sc, pl_sc, act_sc, sa_sc, sb_sc, ba_sc, bb_sc, acc_sc, m_sc, l_sc, y_sc,
                *, KB, nkb_alloc, n_sel, adm_lo, adm_cap, end_base, end_step, end_half, out_rows, out_step, precise):
    b = pl.program_id(0)
    c = pl.program_id(1)
    TQ = LANES
    G = H_C // HKV_C
    lane_q = lax.broadcasted_iota(I32, (1, TQ), 1)
    end_lo = end_base + end_step * c
    end_q = jnp.minimum(jnp.where(lane_q < CHUNK, end_lo, end_lo + end_half), adm_cap)
    end_max = jnp.minimum(end_lo + end_half, adm_cap)
    npair = (end_max + 2 * KB - 1) // (2 * KB)
    u = u_ref[...]
    qi_t = (u[:, 768:896] * (D_IDX ** -0.5)).T
    kw_t = u[:, 896:1024].T
    qi_st = jnp.concatenate([qi_t[h * D_IDX:(h + 1) * D_IDX, :] for h in range(H_IDX)], axis=1)
    if not precise:
        qi_st = qi_st.astype(BF16)
    wi = [kw_t[D_IDX + h:D_IDX + h + 1, :] * (H_IDX ** -0.5) for h in range(H_IDX)]
    key_idx = lax.broadcasted_iota(I32, (KB, TQ), 0)

    def score_block(kb):
        raw = jnp.maximum(_mm(ki_ref[0, kb], qi_st, precise), 0.0)
        isc = wi[0] * raw[:, 0:TQ]
        for h in range(1, H_IDX):
            isc = isc + wi[h] * raw[:, h * TQ:(h + 1) * TQ]
        idx = kb * KB + key_idx
        sc_sc[kb] = jnp.where((idx >= adm_lo) & (idx < end_q), isc, -jnp.inf)

    def score_body(i, carry):
        score_block(2 * i)
        score_block(2 * i + 1)
        return carry

    lax.fori_loop(0, npair, score_body, 0)

    def count(pred_fn):
        def body(i, acc):
            for kb in (2 * i, 2 * i + 1):
                hit = jnp.where(pred_fn(sc_sc[kb]), 1, 0)
                acc = acc + jnp.sum(hit.reshape(KB // SUBLANES, SUBLANES, TQ), axis=0)
            return acc
        acc = lax.fori_loop(0, npair, body, jnp.zeros((SUBLANES, TQ), I32))
        return jnp.sum(acc, axis=0, keepdims=True)

    def key_to_float(key):
        return lax.bitcast_convert_type(jnp.where(key >= 0, key, key ^ 0x7FFFFFFF), F32)

    U32 = jnp.uint32
    nch = (2 * npair + PLANE_CHUNK - 1) // PLANE_CHUNK

    def plane_body(kb, carry):
        bits = lax.bitcast_convert_type(sc_sc[kb], U32)
        flip = jnp.where(bits >= U32(0x80000000), U32(0xFFFFFFFF), U32(0x80000000))
        uk = bits ^ flip
        uk = jnp.where(uk == U32(0x7FFFFFFF), U32(0x80000000), uk)
        a = [uk[SUBLANES * j:SUBLANES * (j + 1), :] for j in range(32)]
        j, m = 16, 0x0000FFFF
        while j:
            k = 0
            while k < 32:
                t = (a[k] ^ (a[k + j] >> U32(j))) & U32(m)
                a[k] = a[k] ^ t
                a[k + j] = a[k + j] ^ (t << U32(j))
                k = (k + j + 1) & ~j
            j >>= 1
            m = (m ^ (m << j)) & 0xFFFFFFFF
        for p in range(32):
            pl_sc[p, kb] = a[p]
        return carry

    lax.fori_loop(0, 2 * npair, plane_body, 0)

    def plane_pad_body(kb, carry):
        for p in range(32):
            pl_sc[p, kb] = jnp.zeros((SUBLANES, TQ), U32)
        return carry

    lax.fori_loop(2 * npair, nch * PLANE_CHUNK, plane_pad_body, 0)

    def chunk(i):
        return pl.ds(pl.multiple_of(i * PLANE_CHUNK, PLANE_CHUNK), PLANE_CHUNK)

    def popsum(words):
        return jnp.sum(lax.population_count(words).astype(I32), axis=0)

    def first_body(i, acc):
        act_sc[chunk(i)] = jnp.full((PLANE_CHUNK, SUBLANES, TQ), 0xFFFFFFFF, U32)
        return acc + popsum(pl_sc[0, chunk(i)])

    cnt1 = jnp.sum(lax.fori_loop(0, nch, first_body, jnp.zeros((SUBLANES, TQ), I32)), axis=0, keepdims=True)

    def radix_body(p, carry):
        rem, ukey, c1 = carry
        take = c1 >= rem

        def body(i, acc):
            act = act_sc[chunk(i)]
            prev = pl_sc[p - 1, chunk(i)]
            act = jnp.where(take, act & prev, act & ~prev)
            act_sc[chunk(i)] = act
            return acc + popsum(act & pl_sc[p, chunk(i)])

        acc = lax.fori_loop(0, nch, body, jnp.zeros((SUBLANES, TQ), I32))
        bit = lax.shift_left(U32(1), jnp.asarray(32 - p, U32))
        return (jnp.where(take, rem, rem - c1), jnp.where(take, ukey | bit, ukey),
                jnp.sum(acc, axis=0, keepdims=True))

    rem, ukey, c1 = lax.fori_loop(1, 32, radix_body,
                                  (jnp.full((1, TQ), n_sel, I32), jnp.zeros((1, TQ), U32), cnt1))
    ukey = jnp.where(c1 >= rem, ukey | U32(1), ukey)
    thr_fast = key_to_float(lax.bitcast_convert_type(ukey ^ U32(0x80000000), I32))
    c_ge = count(lambda s: s >= thr_fast)
    c_gt = count(lambda s: s > thr_fast)
    ok = jnp.min(jnp.where((c_gt < n_sel) & (c_ge >= n_sel), 1, 0)) == 1

    def float_search():
        def thr_body(i, p):
            cand = p + lax.shift_left(jnp.int32(1), 31 - i)
            cand_f = key_to_float(cand)
            cnt = count(lambda s: s >= cand_f)
            return jnp.where(cnt >= n_sel, cand, p)

        thr_key = lax.fori_loop(0, 32, thr_body, jnp.full((1, TQ), INT_MIN, I32))
        t = key_to_float(jnp.maximum(thr_key, KEY_NEG_INF))
        return t, count(lambda s: s > t)

    thr, n_above = lax.cond(ok, lambda: (thr_fast, c_gt), float_search)
    need = (n_sel - n_above).astype(F32)

    q_t = [(u[:, j * LANES:(j + 1) * LANES] * (DH_C ** -0.5 * LOG2_E)).T for j in range(W_C // LANES)]
    zero_half = jnp.zeros((DH_C, G * TQ), F32)
    qz = []
    for n in range(HKV_C):
        heads = [n * G + g for g in range(G)]
        qn = jnp.concatenate([q_t[hc // 2][(hc % 2) * DH_C:(hc % 2 + 1) * DH_C, :] for hc in heads], axis=1)
        qn = jnp.concatenate([qn, zero_half] if n == 0 else [zero_half, qn], axis=0)
        qz.append(qn if precise else qn.astype(BF16))
    tri = (lax.broadcasted_iota(I32, (KB, KB), 0) >= lax.broadcasted_iota(I32, (KB, KB), 1)).astype(BF16)

    def mask_block(kb, tie_seen, dst):
        s_idx = sc_sc[kb]
        eq = (s_idx == thr) & (s_idx > -jnp.inf)
        rank = tie_seen + jnp.dot(tri, jnp.where(eq, 1.0, 0.0).astype(BF16), preferred_element_type=F32)
        sel = (s_idx > thr) | (eq & (rank <= need))
        dst[...] = jnp.where(sel, 0.0, NEG_BIG)
        return rank[KB - 1:KB, :]

    acc_sc[...] = jnp.zeros_like(acc_sc)
    m_sc[...] = jnp.full_like(m_sc, M_INIT)
    l_sc[...] = jnp.zeros_like(l_sc)
    W2 = 2 * TQ
    units = [(n, slice(gp * W2, (gp + 1) * W2)) for n in range(HKV_C) for gp in range(G // 2)]

    def logits(kb, dst):
        kblk = k_ref[0, kb]
        for ui, (n, cols) in enumerate(units):
            dst[ui] = _mm(kblk, qz[n][:, cols], precise)

    def att_step(kb, tie_seen, cur, nxt, bias_cur, bias_nxt):
        logits(jnp.minimum(kb + 1, nkb_alloc - 1), nxt)
        tie_seen = mask_block(jnp.minimum(kb + 1, 2 * npair - 1), tie_seen, bias_nxt)
        bias = bias_cur[...]
        bias2 = jnp.concatenate([bias, bias], axis=1)
        vtblk = vt_ref[0, kb]
        for ui, (n, cols) in enumerate(units):
            s = cur[ui] + bias2
            m_old = m_sc[n, :, cols]
            m_new = jnp.maximum(m_old, jnp.max(s, axis=0, keepdims=True))
            p = jnp.exp2(s - m_new)
            alpha = jnp.exp2(m_old - m_new)
            l_sc[n, :, cols] = alpha * l_sc[n, :, cols] + jnp.sum(p, axis=0, keepdims=True)
            acc_sc[n, :, cols] = alpha * acc_sc[n, :, cols] + _mm(vtblk[n * DH_C:(n + 1) * DH_C, :], p, precise)
            m_sc[n, :, cols] = m_new
        return tie_seen

    logits(0, sa_sc)
    tie0 = mask_block(0, jnp.zeros((1, TQ), F32), ba_sc)

    def att_body(i, tie_seen):
        tie_seen = att_step(2 * i, tie_seen, sa_sc, sb_sc, ba_sc, bb_sc)
        return att_step(2 * i + 1, tie_seen, sb_sc, sa_sc, bb_sc, ba_sc)

    lax.fori_loop(0, npair, att_body, tie0)
    o = [acc_sc[n] / l_sc[n] for n in range(HKV_C)]
    for j in range(W_C // LANES):
        n, g0 = (2 * j) // G, (2 * j) % G
        pair = jnp.concatenate([o[n][:, g0 * TQ:(g0 + 1) * TQ], o[n][:, (g0 + 1) * TQ:(g0 + 2) * TQ]], axis=0)
        y_sc[:, j * LANES:(j + 1) * LANES] = pair.T
    off = pl.multiple_of(b * out_step, SUBLANES)
    y_ref[...] = y_sc[pl.ds(off, out_rows), :]


def _dsa(uc, k_arr, vt_arr, ki_arr, *, nb, nq, n_sel, adm_lo, adm_cap, end_base, end_step, end_half,
         out_rows, out_step, precise):
    rows = uc.shape[0]
    nkb_max, KB = k_arr.shape[1], k_arr.shape[2]
    assert nkb_max % 2 == 0
    n_plane = -(-nkb_max // PLANE_CHUNK) * PLANE_CHUNK
    G = H_C // HKV_C
    per_b = lambda b, c: (b, 0, 0, 0)
    u_map = (lambda b, c: (b * nq + c, 0)) if out_step == 0 else (lambda b, c: (0, 0))
    return pl.pallas_call(
        functools.partial(_dsa_kernel, KB=KB, nkb_alloc=nkb_max, n_sel=n_sel, adm_lo=adm_lo, adm_cap=adm_cap,
                          end_base=end_base, end_step=end_step, end_half=end_half, out_rows=out_rows,
                          out_step=out_step, precise=precise),
        grid=(nb, nq),
        in_specs=[pl.BlockSpec((LANES, _GROUP_WIDTHS[2]), u_map),
                  pl.BlockSpec((1, nkb_max, KB, HKV_C * DH_C), per_b),
                  pl.BlockSpec((1, nkb_max, HKV_C * DH_C, KB), per_b),
                  pl.BlockSpec((1, nkb_max, KB, D_IDX), per_b)],
        out_specs=pl.BlockSpec((out_rows, W_C), lambda b, c: (b * nq + c, 0)),
        out_shape=jax.ShapeDtypeStruct((rows, W_C), F32),
        scratch_shapes=[pltpu.VMEM((nkb_max, KB, LANES), F32),
                        pltpu.VMEM((32, n_plane, SUBLANES, LANES), jnp.uint32),
                        pltpu.VMEM((n_plane, SUBLANES, LANES), jnp.uint32),
                        pltpu.VMEM((HKV_C * (G // 2), KB, 2 * LANES), F32),
                        pltpu.VMEM((HKV_C * (G // 2), KB, 2 * LANES), F32),
                        pltpu.VMEM((KB, LANES), F32), pltpu.VMEM((KB, LANES), F32),
                        pltpu.VMEM((HKV_C, DH_C, G * LANES), F32), pltpu.VMEM((HKV_C, 1, G * LANES), F32),
                        pltpu.VMEM((HKV_C, 1, G * LANES), F32), pltpu.VMEM((LANES, W_C), F32)],
        compiler_params=_params(("parallel", "arbitrary")),
        name="dsa_attend",
    )(uc, k_arr, vt_arr, ki_arr)


def _outproj_kernel(ya_ref, yb_ref, yc_ref, h_ref, w_ref, g_ref, b_ref, o_ref, *, alpha, precise):
    mix = (_mm(ya_ref[...], w_ref[0:W_A, :], precise) + _mm(yb_ref[...], w_ref[W_A:W_A + W_B, :], precise)
           + _mm(yc_ref[...], w_ref[W_A + W_B:, :], precise))
    o_ref[...] = _ln(alpha * h_ref[...] + mix, g_ref[...], b_ref[...])


def _outproj(ya, yb, yc, h, w_out_l, g, b, alpha, precise):
    n, d = h.shape
    tm = _pick_tile(n, 768)
    row = lambda i: (i, 0)
    fix = lambda i: (0, 0)
    return pl.pallas_call(
        functools.partial(_outproj_kernel, alpha=alpha, precise=precise),
        grid=(n // tm,),
        in_specs=[pl.BlockSpec((tm, W_A), row), pl.BlockSpec((tm, W_B), row), pl.BlockSpec((tm, W_C), row),
                  pl.BlockSpec((tm, d), row), pl.BlockSpec(w_out_l.shape, fix),
                  pl.BlockSpec((1, d), fix), pl.BlockSpec((1, d), fix)],
        out_specs=pl.BlockSpec((tm, d), row),
        out_shape=jax.ShapeDtypeStruct((n, d), F32),
        compiler_params=_params(("parallel",)),
        name="outproj_ln1",
    )(ya, yb, yc, h, w_out_l, g.reshape(1, d), b.reshape(1, d))


def _moe_kernel(h_ref, wr_ref, br_ref, wg_ref, wu_ref, wd_ref, g_ref, b_ref, o_ref, acc_sc, gate_sc, *, alpha, precise):
    grp = pl.program_id(1)
    x = h_ref[...]
    tm = x.shape[0]
    xo = x if precise else x.astype(BF16)
    eg = N_EXPERTS // N_GROUPS

    @pl.when(grp == 0)
    def _():
        s = _sigmoid(_mm(wr_ref[...], xo, precise, _NT))
        sb = s + br_ref[...]
        srow = [s[j:j + 1, :] for j in range(N_EXPERTS)]
        brow = [sb[j:j + 1, :] for j in range(N_EXPERTS)]
        gscore = []
        for g in range(N_GROUPS):
            best = None
            for i in range(eg):
                for j in range(i + 1, eg):
                    pair = brow[g * eg + i] + brow[g * eg + j]
                    best = pair if best is None else jnp.maximum(best, pair)
            gscore.append(best)
        gmax = functools.reduce(jnp.maximum, gscore)
        taken = jnp.zeros_like(gmax) > 1.0
        sel = [None] * N_EXPERTS
        for g in range(N_GROUPS):
            is_g = jnp.logical_and(jnp.logical_not(taken), gscore[g] == gmax)
            taken = jnp.logical_or(taken, is_g)
            for i in range(eg):
                a = brow[g * eg + i]
                rank = jnp.zeros(a.shape, I32)
                for j in range(eg):
                    if j == i:
                        continue
                    o = brow[g * eg + j]
                    ahead = (o >= a) if j < i else (o > a)
                    rank = rank + jnp.where(ahead, 1, 0)
                sel[g * eg + i] = jnp.logical_and(is_g, rank < TOP_K)
        denom = jnp.zeros_like(gmax)
        for j in range(N_EXPERTS):
            denom = denom + jnp.where(sel[j], srow[j], 0.0)
        gates = [jnp.where(sel[j], srow[j] / denom, 0.0) for j in range(N_EXPERTS)]
        gt = jnp.concatenate(gates + [jnp.zeros((LANES - N_EXPERTS, tm), F32)], axis=0)
        gate_sc[...] = gt.T
        acc_sc[...] = jnp.zeros_like(acc_sc)

    lane = lax.broadcasted_iota(I32, (1, LANES), 1)
    gate = gate_sc[...]
    hes = []
    for j in range(eg):
        gcol = jnp.sum(jnp.where(lane == grp * eg + j, gate, 0.0), axis=1, keepdims=True)
        hes.append(gcol * (_silu(_mm(xo, wg_ref[j], precise)) * _mm(xo, wu_ref[j], precise)))
    acc_sc[...] += _mm(jnp.concatenate(hes, axis=1), wd_ref[0], precise)

    @pl.when(grp == N_GROUPS - 1)
    def _():
        o_ref[...] = _ln(alpha * x + acc_sc[...], g_ref[...], b_ref[...])


def _moe(h, wr_t, br_col, w_gate, w_up, w_down, g, b, alpha, precise):
    n, d = h.shape
    tm = _pick_tile(n, MOE_TILE)
    assert tm % LANES == 0, tm
    eg = N_EXPERTS // N_GROUPS
    row = lambda i, e: (i, 0)
    fix = lambda i, e: (0, 0)
    per_g = lambda i, e: (e, 0, 0)
    return pl.pallas_call(
        functools.partial(_moe_kernel, alpha=alpha, precise=precise),
        grid=(n // tm, N_GROUPS),
        in_specs=[pl.BlockSpec((tm, d), row), pl.BlockSpec((LANES, d), fix), pl.BlockSpec((LANES, 1), fix),
                  pl.BlockSpec((eg, d, D_EXPERT), per_g), pl.BlockSpec((eg, d, D_EXPERT), per_g),
                  pl.BlockSpec((1, eg * D_EXPERT, d), per_g), pl.BlockSpec((1, d), fix), pl.BlockSpec((1, d), fix)],
        out_specs=pl.BlockSpec((tm, d), row),
        out_shape=jax.ShapeDtypeStruct((n, d), F32),
        scratch_shapes=[pltpu.VMEM((tm, d), F32), pltpu.VMEM((tm, LANES), F32)],
        compiler_params=_params(("parallel", "arbitrary")),
        name="moe_ln2",
    )(h, wr_t, br_col, w_gate, w_up, w_down.reshape(N_GROUPS, eg * D_EXPERT, d), g.reshape(1, d), b.reshape(1, d))


def _pack_in_weights(w_in_l, b_in_l):
    offs = {}
    off = 0
    for name, n in _SPLITS:
        offs[name] = (off, n)
        off += n
    wcols, bcols = [], []
    for grp in (_GROUP_A, _GROUP_B, _GROUP_C):
        for name, n in grp:
            if name is None:
                wcols.append(jnp.zeros((w_in_l.shape[0], n), w_in_l.dtype))
                bcols.append(jnp.zeros((n,), b_in_l.dtype))
            else:
                o, m = offs[name]
                assert m == n
                wcols.append(w_in_l[:, o:o + n])
                bcols.append(b_in_l[o:o + n])
    return jnp.concatenate(wcols, axis=1), jnp.concatenate(bcols).reshape(1, -1).astype(F32)


def _block_diag_embed(x):
    B, H, r, c = x.shape
    eye = jnp.eye(H, dtype=x.dtype)
    return (x[:, :, :, None, :] * eye[None, :, None, :, None]).reshape(B, H * r, H * c)


def _block_diag_extract(x, H):
    B, R, C = x.shape
    r, c = R // H, C // H
    x5 = x.reshape(B, H, r, H, c)
    return jnp.stack([x5[:, h, :, h, :] for h in range(H)], axis=1)


def _mixer_states_in(s_gla, s_c, s_n, s_m, s_conv):
    nb = s_gla.shape[0]
    s0t = _block_diag_embed(jnp.swapaxes(s_gla.astype(F32), -1, -2))
    c0 = _block_diag_embed(jnp.swapaxes(s_c.astype(F32), -1, -2))
    n0 = s_n.astype(F32).reshape(nb, 1, W_B)
    m0 = jnp.pad(s_m.astype(F32), ((0, 0), (0, LANES - H_B))).reshape(nb, 1, LANES)
    conv0 = jnp.pad(s_conv.astype(F32), ((0, 0), (SUBLANES - (CONV_W - 1), 0), (0, 0)))
    return s0t, c0, n0, m0, conv0


def _key_blocks(k, v, ki, dtype):
    nb, s, _ = k.shape
    s_even = -(-s // (2 * KEY_BLOCK)) * 2 * KEY_BLOCK
    k, v, ki = (jnp.pad(a, ((0, 0), (0, s_even - s), (0, 0))) for a in (k, v, ki))
    nkb = s_even // KEY_BLOCK
    kk = k.astype(dtype).reshape(nb, nkb, KEY_BLOCK, -1)
    vt = jnp.swapaxes(v.astype(dtype).reshape(nb, nkb, KEY_BLOCK, -1), -1, -2)
    kib = ki.astype(dtype).reshape(nb, nkb, KEY_BLOCK, -1)
    return kk, vt, kib


def _layer(h, stream, l, p):
    precise = stream['precise']
    wdt = F32 if precise else BF16
    (w_in_p, b_in_p, wg, bg, gn_a, cw, cb, bf, gn_b, w_out_l, ln1_g, ln1_b, wr_t, br_col,
     w_gate, w_up, w_down, ln2_g, ln2_b, ln_in_g, ln_in_b, alpha) = p
    h, ua, ub, uc = _inproj(h, ln_in_g, ln_in_b, w_in_p.astype(wdt), b_in_p, apply_ln=(l == 0), precise=precise)
    nb, nc, L, lo, hi = stream['nb'], stream['nc'], stream['L'], stream['lo'], stream['hi']
    s0t, c0, n0, m0, conv0 = stream['states'](l)
    ya, yb, st_out, c_out, n_out, m_out = _scan(ua, ub, wg.astype(wdt), bg, gn_a, cw, cb, bf, gn_b, s0t, c0, n0, m0,
                                                conv0, nb=nb, nc=nc, L=L, lo=lo, hi=hi, precise=precise)
    k_arr, vt_arr, ki_arr = stream['keys'](l, uc, wdt)
    yc = _dsa(uc, k_arr, vt_arr, ki_arr, nb=nb, precise=precise, **stream['dsa'])
    h1 = _outproj(ya, yb, yc, h, w_out_l.astype(wdt), ln1_g, ln1_b, alpha, precise)
    h2 = _moe(h1, wr_t.astype(wdt), br_col, w_gate, w_up, w_down, ln2_g, ln2_b, alpha, precise)
    S = jnp.swapaxes(_block_diag_extract(st_out, H_A), -1, -2)
    C = jnp.swapaxes(_block_diag_extract(c_out, H_B), -1, -2)
    return h2, (ub, uc, S, C, n_out.reshape(nb, H_B, DH_B), m_out[:, 0, :H_B])


def kernel(x_prompt, x_sample, cache_k, cache_v, cache_kidx, state_gla, state_mlstm_c, state_mlstm_n, state_mlstm_m, state_conv, meta_tokens, ln_in_g, ln_in_b, w_in, b_in, w_gla_gate, b_gla_gate, g_gla_norm, conv_w, conv_b, b_forget, g_mlstm_norm, w_out, ln1_g, ln1_b, w_router, b_router, w_gate, w_up, w_down, ln2_g, ln2_b):
    depth = w_in.shape[0]
    alpha = float((2 * depth) ** 0.25)
    B, S_p, D = x_prompt.shape
    DB, Ts, _ = x_sample.shape
    P = cache_k.shape[2]
    assert S_p % CHUNK == 0 and CHUNK % Ts == 0 and P % CHUNK == 0 and Ts >= CONV_W - 1
    assert DB * Ts == LANES and Ts % SUBLANES == 0
    KB = KEY_BLOCK

    wr_t = jnp.pad(w_router.T, ((0, LANES - N_EXPERTS), (0, 0)))
    br_col = jnp.pad(b_router.astype(F32), (0, LANES - N_EXPERTS)).reshape(LANES, 1)

    def layer_params(l):
        w_in_p, b_in_p = _pack_in_weights(w_in[l], b_in[l])
        wg = jnp.pad(w_gla_gate[l], ((0, LANES - GATE_RANK), (0, 0)))
        bf = jnp.pad(b_forget[l].astype(F32), (H_B, LANES - 2 * H_B)).reshape(1, LANES)
        return (w_in_p, b_in_p, wg, b_gla_gate[l].reshape(1, -1), g_gla_norm[l].reshape(1, -1),
                conv_w[l], conv_b[l].reshape(1, -1), bf, g_mlstm_norm[l].reshape(1, -1), w_out[l],
                ln1_g[l], ln1_b[l], wr_t, br_col, w_gate[l], w_up[l], w_down[l], ln2_g[l], ln2_b[l],
                ln_in_g, ln_in_b, alpha)

    params = [layer_params(l) for l in range(depth)]

    front = CHUNK - N_META
    t_real = N_META + S_p
    t_pad = -(-(front + t_real) // KB) * KB
    hp = jnp.concatenate([jnp.zeros((B, front, D), x_prompt.dtype),
                          jnp.broadcast_to(meta_tokens.astype(x_prompt.dtype)[None], (B, N_META, D)), x_prompt,
                          jnp.zeros((B, t_pad - front - t_real, D), x_prompt.dtype)], axis=1).reshape(B * t_pad, D)

    def prompt_keys(l, uc, dtype):
        u3 = uc.reshape(B, t_pad, -1)
        return _key_blocks(u3[:, :, 512:640], u3[:, :, 640:768], u3[:, :, 896:896 + D_IDX], dtype)

    zero_states = _mixer_states_in(jnp.zeros((B, H_A, DK_A, DV_A), F32), jnp.zeros((B, H_B, DH_B, DH_B), F32),
                                   jnp.zeros((B, H_B, DH_B), F32), jnp.zeros((B, H_B), F32),
                                   jnp.zeros((B, CONV_W - 1, 2 * W_B), F32))
    prompt_stream = dict(nb=B, nc=t_pad // SCAN_CHUNK, L=SCAN_CHUNK, lo=front, hi=front + t_real, precise=False,
                         states=lambda l: zero_states, keys=prompt_keys,
                         dsa=dict(nq=t_pad // LANES, n_sel=min(TOPK_KEYS, S_p // 4), adm_lo=front,
                                  adm_cap=front + t_real, end_base=CHUNK, end_step=2 * CHUNK, end_half=CHUNK,
                                  out_rows=LANES, out_step=0))
    p_out = [[] for _ in range(8)]
    for l in range(depth):
        hp, (ub, uc, S, C, n_o, m_o) = _layer(hp, prompt_stream, l, params[l])
        ub3 = ub.reshape(B, t_pad, -1)[:, front:front + t_real]
        uc3 = uc.reshape(B, t_pad, -1)[:, front:front + t_real]
        vals = (uc3[:, :, 512:640].reshape(B, t_real, HKV_C, DH_C), uc3[:, :, 640:768].reshape(B, t_real, HKV_C, DH_C),
                uc3[:, :, 896:896 + D_IDX], S, C, n_o, m_o, ub3[:, t_real - (CONV_W - 1):, 0:2 * W_B])
        for acc, a in zip(p_out, vals):
            acc.append(a)
    y_prompt = hp.reshape(B, t_pad, D)[:, front + N_META:front + t_real]

    s_tot = P + Ts
    s_pad = -(-s_tot // KB) * KB

    def sample_keys(l, uc, dtype):
        u3 = uc.reshape(DB, Ts, -1)

        def cat(cache, new):
            a = jnp.concatenate([cache.astype(F32).reshape(DB, P, -1), new], axis=1)
            return jnp.pad(a, ((0, 0), (0, s_pad - s_tot), (0, 0)))

        return _key_blocks(cat(cache_k[l], u3[:, :, 512:640]), cat(cache_v[l], u3[:, :, 640:768]),
                           cat(cache_kidx[l], u3[:, :, 896:896 + D_IDX]), dtype)

    sample_stream = dict(nb=DB, nc=1, L=Ts, lo=0, hi=Ts, precise=True,
                         states=lambda l: _mixer_states_in(state_gla[l], state_mlstm_c[l], state_mlstm_n[l],
                                                           state_mlstm_m[l], state_conv[l]),
                         keys=sample_keys,
                         dsa=dict(nq=1, n_sel=min(TOPK_KEYS, s_tot // 4), adm_lo=0, adm_cap=s_tot, end_base=s_tot,
                                  end_step=0, end_half=0, out_rows=Ts, out_step=Ts))
    hs = x_sample.reshape(DB * Ts, D)
    s_out = [[] for _ in range(8)]
    for l in range(depth):
        hs, (ub, uc, S, C, n_o, m_o) = _layer(hs, sample_stream, l, params[l])
        ub3 = ub.reshape(DB, Ts, -1)
        uc3 = uc.reshape(DB, Ts, -1)
        vals = (uc3[:, :, 512:640].reshape(DB, Ts, HKV_C, DH_C), uc3[:, :, 640:768].reshape(DB, Ts, HKV_C, DH_C),
                uc3[:, :, 896:896 + D_IDX], S, C, n_o, m_o, ub3[:, Ts - (CONV_W - 1):, 0:2 * W_B])
        for acc, a in zip(s_out, vals):
            acc.append(a)
    y_sample = hs.reshape(DB, Ts, D)

    return (y_prompt, y_sample, *[jnp.stack(a) for a in p_out], *[jnp.stack(a) for a in s_out])
```

```python
import functools

import jax
import jax.numpy as jnp
from jax import lax
from jax.experimental import pallas as pl
from jax.experimental.pallas import tpu as pltpu

F32 = jnp.float32
BF16 = jnp.bfloat16
I32 = jnp.int32

CHUNK = 64
N_META = 16
H_A, DK_A, DV_A = 4, 32, 64
GATE_RANK, GATE_TAU = 16, 16.0
H_B, DH_B, CONV_W = 4, 64, 4
H_C, HKV_C, DH_C = 8, 2, 64
H_IDX, D_IDX = 4, 32
TOPK_KEYS = 256
N_EXPERTS, N_GROUPS, TOP_K, D_EXPERT = 16, 4, 2, 256
W_A, W_B, W_C = H_A * DV_A, H_B * DH_B, H_C * DH_C
LN_EPS = 1e-5

LANES = 128
SUBLANES = 8
VMEM_LIMIT = 56 * 1024 * 1024
KEY_BLOCK = 256
MOE_TILE = 768
PLANE_CHUNK = 8
SCAN_CHUNK = 128
NEG_BIG = -1e30
M_INIT = -1e29
INT_MIN = -2 ** 31
LOG2_E = 1.4426950408889634
KEY_NEG_INF = INT_MIN + 0x7FFFFF

_SPLITS = (
    ('a_q', H_A * DK_A), ('a_k', H_A * DK_A), ('a_v', W_A), ('a_g', GATE_RANK), ('a_r', W_A),
    ('b_qk', 2 * W_B), ('b_v', W_B), ('b_i', H_B), ('b_f', H_B), ('b_o', W_B),
    ('c_q', W_C), ('c_k', HKV_C * DH_C), ('c_v', HKV_C * DH_C),
    ('c_qi', H_IDX * D_IDX), ('c_w', H_IDX), ('c_ki', D_IDX),
)
_GROUP_A = (('a_q', 128), ('a_k', 128), ('a_v', 256), ('a_r', 256), ('a_g', 16), (None, 112))
_GROUP_B = (('b_qk', 512), ('b_v', 256), ('b_o', 256), ('b_i', 4), ('b_f', 4), (None, 120))
_GROUP_C = (('c_q', 512), ('c_k', 128), ('c_v', 128), ('c_qi', 128), ('c_ki', 32), ('c_w', 4), (None, 92))
_GROUP_WIDTHS = tuple(sum(n for _, n in g) for g in (_GROUP_A, _GROUP_B, _GROUP_C))

_NN = (((1,), (0,)), ((), ()))
_NT = (((1,), (1,)), ((), ()))
_TN = (((0,), (0,)), ((), ()))


def _mm(a, b, precise, dims=_NN):
    if precise:
        return lax.dot_general(a.astype(F32), b.astype(F32), dims, precision=lax.Precision.HIGHEST,
                               preferred_element_type=F32)
    return lax.dot_general(a.astype(BF16), b.astype(BF16), dims, preferred_element_type=F32)


def _log_sigmoid(x):
    return jnp.minimum(x, 0.0) - jnp.log(1.0 + jnp.exp(-jnp.abs(x)))


def _sigmoid(x):
    return 1.0 / (1.0 + jnp.exp(-x))


def _silu(x):
    return x * _sigmoid(x)


def _ln(x, g, b):
    mu = jnp.mean(x, axis=-1, keepdims=True)
    xc = x - mu
    var = jnp.mean(xc * xc, axis=-1, keepdims=True)
    return xc * lax.rsqrt(var + LN_EPS) * g + b


def _pick_tile(n, cap):
    best = None
    for t in range(SUBLANES, min(n, cap) + 1, SUBLANES):
        if n % t == 0:
            best = t
    assert best is not None, n
    return best


def _stack_heads(x, n_heads, width):
    lane_head = lax.broadcasted_iota(I32, (1, n_heads * width), 1) // width
    return jnp.concatenate([jnp.where(lane_head == h, x, 0.0) for h in range(n_heads)], axis=0)


def _head_norm_stacked(o_st, mask_st, n_heads, L, width):
    o_st = jnp.where(mask_st, o_st, 0.0)
    mu = jnp.sum(o_st, axis=-1, keepdims=True) * (1.0 / width)
    xc = jnp.where(mask_st, o_st - mu, 0.0)
    var = jnp.sum(xc * xc, axis=-1, keepdims=True) * (1.0 / width)
    yn = xc * lax.rsqrt(var + LN_EPS)
    y = yn[0:L]
    for h in range(1, n_heads):
        y = y + yn[h * L:(h + 1) * L]
    return y


def _params(sem):
    return pltpu.CompilerParams(dimension_semantics=sem, vmem_limit_bytes=VMEM_LIMIT)


def _inproj_kernel(x_ref, g_ref, b_ref, w_ref, bias_ref, *out_refs, apply_ln, precise):
    x = x_ref[...]
    if apply_ln:
        h_ref, *u_refs = out_refs
        x = _ln(x, g_ref[...], b_ref[...])
        h_ref[...] = x
    else:
        u_refs = out_refs
    xo = x if precise else x.astype(BF16)
    off = 0
    for r, n in zip(u_refs, _GROUP_WIDTHS):
        r[...] = _mm(xo, w_ref[:, off:off + n], precise) + bias_ref[:, off:off + n]
        off += n


def _inproj(x, ln_g, ln_b, w_packed, b_packed, apply_ln, precise):
    n, d = x.shape
    tm = _pick_tile(n, 384)
    dp = w_packed.shape[1]
    row = lambda i: (i, 0)
    fix = lambda i: (0, 0)
    out_shape = [jax.ShapeDtypeStruct((n, w), F32) for w in _GROUP_WIDTHS]
    out_specs = [pl.BlockSpec((tm, w), row) for w in _GROUP_WIDTHS]
    if apply_ln:
        out_shape = [jax.ShapeDtypeStruct((n, d), F32)] + out_shape
        out_specs = [pl.BlockSpec((tm, d), row)] + out_specs
    outs = pl.pallas_call(
        functools.partial(_inproj_kernel, apply_ln=apply_ln, precise=precise),
        grid=(n // tm,),
        in_specs=[pl.BlockSpec((tm, d), row), pl.BlockSpec((1, d), fix), pl.BlockSpec((1, d), fix),
                  pl.BlockSpec((d, dp), fix), pl.BlockSpec((1, dp), fix)],
        out_specs=out_specs, out_shape=out_shape,
        compiler_params=_params(("parallel",)),
        name="inproj_ln" if apply_ln else "inproj",
    )(x, ln_g.reshape(1, d), ln_b.reshape(1, d), w_packed, b_packed)
    if apply_ln:
        return outs[0], outs[1], outs[2], outs[3]
    return x, outs[0], outs[1], outs[2]


def _gla_step(u, wg, bg, gn, st, c, *, L, lo, hi, precise):
    q = u[:, 0:128] * (DK_A ** -0.5)
    k = u[:, 128:256]
    v = u[:, 256:512]
    r = u[:, 512:768]
    g = u[:, 768:896]
    la = _log_sigmoid(_mm(g, wg, precise) + bg) * (1.0 / GATE_TAU)
    pos = c * L + lax.broadcasted_iota(I32, (L, 1), 0)
    real = (pos >= lo) & (pos < hi)
    la = jnp.where(real, la, 0.0)
    k = jnp.where(real, k, 0.0)
    ti = lax.broadcasted_iota(I32, (L, L), 0)
    si = lax.broadcasted_iota(I32, (L, L), 1)
    b = _mm((ti >= si).astype(F32), la, True)
    bmid = b[L // 2 - 1:L // 2, :]
    bl = b[L - 1:L, :]
    qs = _stack_heads(q * jnp.exp(b - bmid), H_A, DK_A)
    qbs = _stack_heads(q * jnp.exp(b), H_A, DK_A)
    kt = k * jnp.exp(bmid - b)
    kh = k * jnp.exp(bl - b)
    att = _mm(qs, kt, precise, _NT)
    tr = lax.broadcasted_iota(I32, (H_A * L, L), 0) % L
    sc = lax.broadcasted_iota(I32, (H_A * L, L), 1)
    att = jnp.where(sc <= tr, att, 0.0)
    row_head = lax.broadcasted_iota(I32, (H_A * L, W_A), 0) // L
    lane_head = lax.broadcasted_iota(I32, (H_A * L, W_A), 1) // DV_A
    mask_st = row_head == lane_head
    o_st = jnp.where(mask_st, _mm(att, v, precise), 0.0) + _mm(qbs, st, precise, _NT)
    y = _head_norm_stacked(o_st, mask_st, H_A, L, DV_A) * gn * _silu(r)
    bd = (lax.broadcasted_iota(I32, (W_A, H_A * DK_A), 0) // DV_A) == (lax.broadcasted_iota(I32, (W_A, H_A * DK_A), 1) // DK_A)
    return y, st * jnp.exp(bl) + jnp.where(bd, _mm(v, kh, precise, _TN), 0.0)


def _mlstm_step(u, cw, cb, bf, gn, cbd, nrow, m_prev, buf, c, *, L, lo, hi, precise):
    H, D = H_B, DH_B
    pos = c * L + lax.broadcasted_iota(I32, (L, 1), 0)
    real = (pos >= lo) & (pos < hi)
    buf[SUBLANES:SUBLANES + L, :] = jnp.where(real, u[:, 0:2 * W_B], 0.0)
    hist = SUBLANES - (CONV_W - 1)
    conv = buf[hist:hist + L, :] * cw[0:1, :]
    for j in range(1, CONV_W):
        conv = conv + buf[hist + j:hist + j + L, :] * cw[j:j + 1, :]
    conv = conv + cb
    buf[0:SUBLANES, :] = buf[L:L + SUBLANES, :]
    act = _silu(conv)
    q = act[:, 0:W_B]
    k = act[:, W_B:2 * W_B] * (D ** -0.5)
    v = u[:, 512:768]
    og = u[:, 768:1024]
    gts = u[:, 1024:1152]
    ig_col = jnp.where(real, gts, NEG_BIG)
    lf_col = jnp.where(real, _log_sigmoid(gts + bf), 0.0)
    ti = lax.broadcasted_iota(I32, (L, L), 0)
    si = lax.broadcasted_iota(I32, (L, L), 1)
    b_col = _mm((ti >= si).astype(F32), lf_col, True)
    lane = lax.broadcasted_iota(I32, (1, LANES), 1)
    z = jnp.where(lane < H, ig_col, b_col)
    zp = z if L == LANES else jnp.concatenate([z, jnp.zeros((LANES - L, LANES), F32)], axis=0)
    zt = zp.T
    b_st = jnp.concatenate([b_col[:, H + h:H + h + 1] for h in range(H)], axis=0)
    m0_st = jnp.concatenate([jnp.broadcast_to(m_prev[:, h:h + 1], (L, 1)) for h in range(H)], axis=0)
    rowv_st = jnp.concatenate(
        [jnp.broadcast_to(zt[h:h + 1, 0:L] - zt[H + h:H + h + 1, 0:L], (L, L)) for h in range(H)], axis=0)
    tr = lax.broadcasted_iota(I32, (H * L, L), 0) % L
    sc = lax.broadcasted_iota(I32, (H * L, L), 1)
    logw = jnp.where(sc <= tr, b_st + rowv_st, NEG_BIG)
    lp = b_st + m0_st
    m_st = jnp.maximum(lp, jnp.max(logw, axis=1, keepdims=True))
    w = jnp.exp(logw - m_st)
    sp = jnp.exp(lp - m_st)
    qs = _stack_heads(q, H, D)
    qk = _mm(qs, k, precise, _NT) * w
    row_head = lax.broadcasted_iota(I32, (H * L, W_B), 0) // L
    lane_head = lax.broadcasted_iota(I32, (H * L, W_B), 1) // D
    mask_st = row_head == lane_head
    num = jnp.where(mask_st, _mm(qk, v, precise), 0.0) + sp * _mm(qs, cbd, precise)
    den = jnp.sum(qk, axis=1, keepdims=True) + sp * jnp.sum(qs * nrow, axis=1, keepdims=True)
    hh = num / jnp.maximum(jnp.abs(den), jnp.exp(-m_st))
    y = _head_norm_stacked(hh, mask_st, H, L, D) * gn * _sigmoid(og)
    lh = lax.broadcasted_iota(I32, (1, W_B), 1) // D
    wlk = jnp.zeros((L, W_B), F32)
    sl_lanes = jnp.zeros((1, W_B), F32)
    m_new = jnp.zeros((1, LANES), F32)
    for h in range(H):
        ml = m_st[h * L + L - 1:h * L + L, :]
        bl = b_st[h * L + L - 1:h * L + L, :]
        wl = jnp.exp(bl - b_col[:, H + h:H + h + 1] + ig_col[:, h:h + 1] - ml)
        sl = jnp.exp(bl + m_prev[:, h:h + 1] - ml)
        wlk = jnp.where(lh == h, wl, wlk)
        sl_lanes = jnp.where(lh == h, sl, sl_lanes)
        m_new = jnp.where(lane == h, ml, m_new)
    kw = k * wlk
    bd = (lax.broadcasted_iota(I32, (W_B, W_B), 0) // D) == (lax.broadcasted_iota(I32, (W_B, W_B), 1) // D)
    c_new = sl_lanes * cbd + jnp.where(bd, _mm(kw, v, precise, _TN), 0.0)
    n_new = sl_lanes * nrow + jnp.sum(kw, axis=0, keepdims=True)
    return y, c_new, n_new, m_new


def _scan_kernel(ua_ref, ub_ref, wg_ref, bg_ref, gna_ref, cw_ref, cb_ref, bf_ref, gnb_ref,
                 s0_ref, c0_ref, n0_ref, m0_ref, conv0_ref,
                 ya_ref, yb_ref, sout_ref, cout_ref, nout_ref, mout_ref,
                 st_sc, c_sc, n_sc, m_sc, buf_sc, *, L, lo, hi, precise, bpg):
    c = pl.program_id(1)

    @pl.when(c == 0)
    def _():
        st_sc[...] = s0_ref[...]
        c_sc[...] = c0_ref[...]
        n_sc[...] = n0_ref[...]
        m_sc[...] = m0_ref[...]
        buf_sc[:, 0:SUBLANES, :] = conv0_ref[...]

    for b in range(bpg):
        y, st_new = _gla_step(ua_ref[b], wg_ref[...], bg_ref[...], gna_ref[...], st_sc[b], c,
                              L=L, lo=lo, hi=hi, precise=precise)
        ya_ref[b] = y
        st_sc[b] = st_new
        sout_ref[b] = st_new
    for b in range(bpg):
        y, c_new, n_new, m_new = _mlstm_step(ub_ref[b], cw_ref[...], cb_ref[...], bf_ref[...], gnb_ref[...],
                                             c_sc[b], n_sc[b], m_sc[b], buf_sc.at[b], c,
                                             L=L, lo=lo, hi=hi, precise=precise)
        yb_ref[b] = y
        c_sc[b] = c_new
        n_sc[b] = n_new
        m_sc[b] = m_new
        cout_ref[b] = c_new
        nout_ref[b] = n_new
        mout_ref[b] = m_new


def _scan(ua, ub, wg, bg, gna, cw, cb, bf, gnb, s0t, c0, n0, m0, conv0, *, nb, nc, L, lo, hi, precise):
    bpg = 2
    assert nb % bpg == 0
    rows = ua.shape[0]
    t = rows // nb
    fix2 = lambda g, c: (0, 0)
    tok = lambda g, c: (g, c, 0)
    per_g = lambda g, c: (g, 0, 0)
    kd = H_A * DK_A
    ya, yb, st_out, c_out, n_out, m_out = pl.pallas_call(
        functools.partial(_scan_kernel, L=L, lo=lo, hi=hi, precise=precise, bpg=bpg),
        grid=(nb // bpg, nc),
        in_specs=[pl.BlockSpec((bpg, L, _GROUP_WIDTHS[0]), tok), pl.BlockSpec((bpg, L, _GROUP_WIDTHS[1]), tok),
                  pl.BlockSpec((LANES, LANES), fix2), pl.BlockSpec((1, LANES), fix2), pl.BlockSpec((1, W_A), fix2),
                  pl.BlockSpec((CONV_W, 2 * W_B), fix2), pl.BlockSpec((1, 2 * W_B), fix2),
                  pl.BlockSpec((1, LANES), fix2), pl.BlockSpec((1, W_B), fix2),
                  pl.BlockSpec((bpg, W_A, kd), per_g), pl.BlockSpec((bpg, W_B, W_B), per_g),
                  pl.BlockSpec((bpg, 1, W_B), per_g), pl.BlockSpec((bpg, 1, LANES), per_g),
                  pl.BlockSpec((bpg, SUBLANES, 2 * W_B), per_g)],
        out_specs=[pl.BlockSpec((bpg, L, W_A), tok), pl.BlockSpec((bpg, L, W_B), tok),
                   pl.BlockSpec((bpg, W_A, kd), per_g), pl.BlockSpec((bpg, W_B, W_B), per_g),
                   pl.BlockSpec((bpg, 1, W_B), per_g), pl.BlockSpec((bpg, 1, LANES), per_g)],
        out_shape=[jax.ShapeDtypeStruct((nb, t, W_A), F32), jax.ShapeDtypeStruct((nb, t, W_B), F32),
                   jax.ShapeDtypeStruct((nb, W_A, kd), F32), jax.ShapeDtypeStruct((nb, W_B, W_B), F32),
                   jax.ShapeDtypeStruct((nb, 1, W_B), F32), jax.ShapeDtypeStruct((nb, 1, LANES), F32)],
        scratch_shapes=[pltpu.VMEM((bpg, W_A, kd), F32), pltpu.VMEM((bpg, W_B, W_B), F32),
                        pltpu.VMEM((bpg, 1, W_B), F32), pltpu.VMEM((bpg, 1, LANES), F32),
                        pltpu.VMEM((bpg, L + SUBLANES, 2 * W_B), F32)],
        compiler_params=_params(("parallel", "arbitrary")),
        name="mixer_scans",
    )(ua.reshape(nb, t, -1), ub.reshape(nb, t, -1), wg, bg, gna, cw, cb, bf, gnb, s0t, c0, n0, m0, conv0)
    return ya.reshape(rows, W_A), yb.reshape(rows, W_B), st_out, c_out, n_out, m_out


def _dsa_kernel(u_ref, k_ref, vt_ref, ki_ref, y_ref, sc_sc, pl_sc, act_sc, sa_sc, sb_sc, ba_sc, bb_sc, acc_sc, m_sc, l_sc, y_sc,
                *, KB, nkb_alloc, n_sel, adm_lo, adm_cap, end_base, end_step, end_half, out_rows, out_step, precise):
    b = pl.program_id(0)
    c = pl.program_id(1)
    TQ = LANES
    G = H_C // HKV_C
    lane_q = lax.broadcasted_iota(I32, (1, TQ), 1)
    end_lo = end_base + end_step * c
    end_q = jnp.minimum(jnp.where(lane_q < CHUNK, end_lo, end_lo + end_half), adm_cap)
    end_max = jnp.minimum(end_lo + end_half, adm_cap)
    npair = (end_max + 2 * KB - 1) // (2 * KB)
    u = u_ref[...]
    qi_t = (u[:, 768:896] * (D_IDX ** -0.5)).T
    kw_t = u[:, 896:1024].T
    qi_st = jnp.concatenate([qi_t[h * D_IDX:(h + 1) * D_IDX, :] for h in range(H_IDX)], axis=1)
    if not precise:
        qi_st = qi_st.astype(BF16)
    wi = [kw_t[D_IDX + h:D_IDX + h + 1, :] * (H_IDX ** -0.5) for h in range(H_IDX)]
    key_idx = lax.broadcasted_iota(I32, (KB, TQ), 0)

    def score_block(kb):
        raw = jnp.maximum(_mm(ki_ref[0, kb], qi_st, precise), 0.0)
        isc = wi[0] * raw[:, 0:TQ]
        for h in range(1, H_IDX):
            isc = isc + wi[h] * raw[:, h * TQ:(h + 1) * TQ]
        idx = kb * KB + key_idx
        sc_sc[kb] = jnp.where((idx >= adm_lo) & (idx < end_q), isc, -jnp.inf)

    def score_body(i, carry):
        score_block(2 * i)
        score_block(2 * i + 1)
        return carry

    lax.fori_loop(0, npair, score_body, 0)

    def count(pred_fn):
        def body(i, acc):
            for kb in (2 * i, 2 * i + 1):
                hit = jnp.where(pred_fn(sc_sc[kb]), 1, 0)
                acc = acc + jnp.sum(hit.reshape(KB // SUBLANES, SUBLANES, TQ), axis=0)
            return acc
        acc = lax.fori_loop(0, npair, body, jnp.zeros((SUBLANES, TQ), I32))
        return jnp.sum(acc, axis=0, keepdims=True)

    def key_to_float(key):
        return lax.bitcast_convert_type(jnp.where(key >= 0, key, key ^ 0x7FFFFFFF), F32)

    U32 = jnp.uint32
    nch = (2 * npair + PLANE_CHUNK - 1) // PLANE_CHUNK

    def plane_body(kb, carry):
        bits = lax.bitcast_convert_type(sc_sc[kb], U32)
        flip = jnp.where(bits >= U32(0x80000000), U32(0xFFFFFFFF), U32(0x80000000))
        uk = bits ^ flip
        uk = jnp.where(uk == U32(0x7FFFFFFF), U32(0x80000000), uk)
        a = [uk[SUBLANES * j:SUBLANES * (j + 1), :] for j in range(32)]
        j, m = 16, 0x0000FFFF
        while j:
            k = 0
            while k < 32:
                t = (a[k] ^ (a[k + j] >> U32(j))) & U32(m)
                a[k] = a[k] ^ t
                a[k + j] = a[k + j] ^ (t << U32(j))
                k = (k + j + 1) & ~j
            j >>= 1
            m = (m ^ (m << j)) & 0xFFFFFFFF
        for p in range(32):
            pl_sc[p, kb] = a[p]
        return carry

    lax.fori_loop(0, 2 * npair, plane_body, 0)

    def plane_pad_body(kb, carry):
        for p in range(32):
            pl_sc[p, kb] = jnp.zeros((SUBLANES, TQ), U32)
        return carry

    lax.fori_loop(2 * npair, nch * PLANE_CHUNK, plane_pad_body, 0)

    def chunk(i):
        return pl.ds(pl.multiple_of(i * PLANE_CHUNK, PLANE_CHUNK), PLANE_CHUNK)

    def popsum(words):
        return jnp.sum(lax.population_count(words).astype(I32), axis=0)

    def first_body(i, acc):
        act_sc[chunk(i)] = jnp.full((PLANE_CHUNK, SUBLANES, TQ), 0xFFFFFFFF, U32)
        return acc + popsum(pl_sc[0, chunk(i)])

    cnt1 = jnp.sum(lax.fori_loop(0, nch, first_body, jnp.zeros((SUBLANES, TQ), I32)), axis=0, keepdims=True)

    def radix_body(p, carry):
        rem, ukey, c1 = carry
        take = c1 >= rem

        def body(i, acc):
            act = act_sc[chunk(i)]
            prev = pl_sc[p - 1, chunk(i)]
            act = jnp.where(take, act & prev, act & ~prev)
            act_sc[chunk(i)] = act
            return acc + popsum(act & pl_sc[p, chunk(i)])

        acc = lax.fori_loop(0, nch, body, jnp.zeros((SUBLANES, TQ), I32))
        bit = lax.shift_left(U32(1), jnp.asarray(32 - p, U32))
        return (jnp.where(take, rem, rem - c1), jnp.where(take, ukey | bit, ukey),
                jnp.sum(acc, axis=0, keepdims=True))

    rem, ukey, c1 = lax.fori_loop(1, 32, radix_body,
                                  (jnp.full((1, TQ), n_sel, I32), jnp.zeros((1, TQ), U32), cnt1))
    ukey = jnp.where(c1 >= rem, ukey | U32(1), ukey)
    thr_fast = key_to_float(lax.bitcast_convert_type(ukey ^ U32(0x80000000), I32))
    c_ge = count(lambda s: s >= thr_fast)
    c_gt = count(lambda s: s > thr_fast)
    ok = jnp.min(jnp.where((c_gt < n_sel) & (c_ge >= n_sel), 1, 0)) == 1

    def float_search():
        def thr_body(i, p):
            cand = p + lax.shift_left(jnp.int32(1), 31 - i)
            cand_f = key_to_float(cand)
            cnt = count(lambda s: s >= cand_f)
            return jnp.where(cnt >= n_sel, cand, p)

        thr_key = lax.fori_loop(0, 32, thr_body, jnp.full((1, TQ), INT_MIN, I32))
        t = key_to_float(jnp.maximum(thr_key, KEY_NEG_INF))
        return t, count(lambda s: s > t)

    thr, n_above = lax.cond(ok, lambda: (thr_fast, c_gt), float_search)
    need = (n_sel - n_above).astype(F32)

    q_t = [(u[:, j * LANES:(j + 1) * LANES] * (DH_C ** -0.5 * LOG2_E)).T for j in range(W_C // LANES)]
    zero_half = jnp.zeros((DH_C, G * TQ), F32)
    qz = []
    for n in range(HKV_C):
        heads = [n * G + g for g in range(G)]
        qn = jnp.concatenate([q_t[hc // 2][(hc % 2) * DH_C:(hc % 2 + 1) * DH_C, :] for hc in heads], axis=1)
        qn = jnp.concatenate([qn, zero_half] if n == 0 else [zero_half, qn], axis=0)
        qz.append(qn if precise else qn.astype(BF16))
    tri = (lax.broadcasted_iota(I32, (KB, KB), 0) >= lax.broadcasted_iota(I32, (KB, KB), 1)).astype(BF16)

    def mask_block(kb, tie_seen, dst):
        s_idx = sc_sc[kb]
        eq = (s_idx == thr) & (s_idx > -jnp.inf)
        rank = tie_seen + jnp.dot(tri, jnp.where(eq, 1.0, 0.0).astype(BF16), preferred_element_type=F32)
        sel = (s_idx > thr) | (eq & (rank <= need))
        dst[...] = jnp.where(sel, 0.0, NEG_BIG)
        return rank[KB - 1:KB, :]

    acc_sc[...] = jnp.zeros_like(acc_sc)
    m_sc[...] = jnp.full_like(m_sc, M_INIT)
    l_sc[...] = jnp.zeros_like(l_sc)
    W2 = 2 * TQ
    units = [(n, slice(gp * W2, (gp + 1) * W2)) for n in range(HKV_C) for gp in range(G // 2)]

    def logits(kb, dst):
        kblk = k_ref[0, kb]
        for ui, (n, cols) in enumerate(units):
            dst[ui] = _mm(kblk, qz[n][:, cols], precise)

    def att_step(kb, tie_seen, cur, nxt, bias_cur, bias_nxt):
        logits(jnp.minimum(kb + 1, nkb_alloc - 1), nxt)
        tie_seen = mask_block(jnp.minimum(kb + 1, 2 * npair - 1), tie_seen, bias_nxt)
        bias = bias_cur[...]
        bias2 = jnp.concatenate([bias, bias], axis=1)
        vtblk = vt_ref[0, kb]
        for ui, (n, cols) in enumerate(units):
            s = cur[ui] + bias2
            m_old = m_sc[n, :, cols]
            m_new = jnp.maximum(m_old, jnp.max(s, axis=0, keepdims=True))
            p = jnp.exp2(s - m_new)
            alpha = jnp.exp2(m_old - m_new)
            l_sc[n, :, cols] = alpha * l_sc[n, :, cols] + jnp.sum(p, axis=0, keepdims=True)
            acc_sc[n, :, cols] = alpha * acc_sc[n, :, cols] + _mm(vtblk[n * DH_C:(n + 1) * DH_C, :], p, precise)
            m_sc[n, :, cols] = m_new
        return tie_seen

    logits(0, sa_sc)
    tie0 = mask_block(0, jnp.zeros((1, TQ), F32), ba_sc)

    def att_body(i, tie_seen):
        tie_seen = att_step(2 * i, tie_seen, sa_sc, sb_sc, ba_sc, bb_sc)
        return att_step(2 * i + 1, tie_seen, sb_sc, sa_sc, bb_sc, ba_sc)

    lax.fori_loop(0, npair, att_body, tie0)
    o = [acc_sc[n] / l_sc[n] for n in range(HKV_C)]
    for j in range(W_C // LANES):
        n, g0 = (2 * j) // G, (2 * j) % G
        pair = jnp.concatenate([o[n][:, g0 * TQ:(g0 + 1) * TQ], o[n][:, (g0 + 1) * TQ:(g0 + 2) * TQ]], axis=0)
        y_sc[:, j * LANES:(j + 1) * LANES] = pair.T
    off = pl.multiple_of(b * out_step, SUBLANES)
    y_ref[...] = y_sc[pl.ds(off, out_rows), :]


def _dsa(uc, k_arr, vt_arr, ki_arr, *, nb, nq, n_sel, adm_lo, adm_cap, end_base, end_step, end_half,
         out_rows, out_step, precise):
    rows = uc.shape[0]
    nkb_max, KB = k_arr.shape[1], k_arr.shape[2]
    assert nkb_max % 2 == 0
    n_plane = -(-nkb_max // PLANE_CHUNK) * PLANE_CHUNK
    G = H_C // HKV_C
    per_b = lambda b, c: (b, 0, 0, 0)
    u_map = (lambda b, c: (b * nq + c, 0)) if out_step == 0 else (lambda b, c: (0, 0))
    return pl.pallas_call(
        functools.partial(_dsa_kernel, KB=KB, nkb_alloc=nkb_max, n_sel=n_sel, adm_lo=adm_lo, adm_cap=adm_cap,
                          end_base=end_base, end_step=end_step, end_half=end_half, out_rows=out_rows,
                          out_step=out_step, precise=precise),
        grid=(nb, nq),
        in_specs=[pl.BlockSpec((LANES, _GROUP_WIDTHS[2]), u_map),
                  pl.BlockSpec((1, nkb_max, KB, HKV_C * DH_C), per_b),
                  pl.BlockSpec((1, nkb_max, HKV_C * DH_C, KB), per_b),
                  pl.BlockSpec((1, nkb_max, KB, D_IDX), per_b)],
        out_specs=pl.BlockSpec((out_rows, W_C), lambda b, c: (b * nq + c, 0)),
        out_shape=jax.ShapeDtypeStruct((rows, W_C), F32),
        scratch_shapes=[pltpu.VMEM((nkb_max, KB, LANES), F32),
                        pltpu.VMEM((32, n_plane, SUBLANES, LANES), jnp.uint32),
                        pltpu.VMEM((n_plane, SUBLANES, LANES), jnp.uint32),
                        pltpu.VMEM((HKV_C * (G // 2), KB, 2 * LANES), F32),
                        pltpu.VMEM((HKV_C * (G // 2), KB, 2 * LANES), F32),
                        pltpu.VMEM((KB, LANES), F32), pltpu.VMEM((KB, LANES), F32),
                        pltpu.VMEM((HKV_C, DH_C, G * LANES), F32), pltpu.VMEM((HKV_C, 1, G * LANES), F32),
                        pltpu.VMEM((HKV_C, 1, G * LANES), F32), pltpu.VMEM((LANES, W_C), F32)],
        compiler_params=_params(("parallel", "arbitrary")),
        name="dsa_attend",
    )(uc, k_arr, vt_arr, ki_arr)


def _outproj_kernel(ya_ref, yb_ref, yc_ref, h_ref, w_ref, g_ref, b_ref, o_ref, *, alpha, precise):
    mix = (_mm(ya_ref[...], w_ref[0:W_A, :], precise) + _mm(yb_ref[...], w_ref[W_A:W_A + W_B, :], precise)
           + _mm(yc_ref[...], w_ref[W_A + W_B:, :], precise))
    o_ref[...] = _ln(alpha * h_ref[...] + mix, g_ref[...], b_ref[...])


def _outproj(ya, yb, yc, h, w_out_l, g, b, alpha, precise):
    n, d = h.shape
    tm = _pick_tile(n, 768)
    row = lambda i: (i, 0)
    fix = lambda i: (0, 0)
    return pl.pallas_call(
        functools.partial(_outproj_kernel, alpha=alpha, precise=precise),
        grid=(n // tm,),
        in_specs=[pl.BlockSpec((tm, W_A), row), pl.BlockSpec((tm, W_B), row), pl.BlockSpec((tm, W_C), row),
                  pl.BlockSpec((tm, d), row), pl.BlockSpec(w_out_l.shape, fix),
                  pl.BlockSpec((1, d), fix), pl.BlockSpec((1, d), fix)],
        out_specs=pl.BlockSpec((tm, d), row),
        out_shape=jax.ShapeDtypeStruct((n, d), F32),
        compiler_params=_params(("parallel",)),
        name="outproj_ln1",
    )(ya, yb, yc, h, w_out_l, g.reshape(1, d), b.reshape(1, d))


def _moe_kernel(ya_ref, yb_ref, yc_ref, h_ref, wo_ref, g1_ref, b1_ref, wr_ref, br_ref, wg_ref, wu_ref, wd_ref,
                g_ref, b_ref, o_ref, x_sc, acc_sc, gate_sc, *, alpha, precise):
    grp = pl.program_id(1)

    @pl.when(grp == 0)
    def _():
        mix = (_mm(ya_ref[...], wo_ref[0:W_A, :], precise) + _mm(yb_ref[...], wo_ref[W_A:W_A + W_B, :], precise)
               + _mm(yc_ref[...], wo_ref[W_A + W_B:, :], precise))
        x_sc[...] = _ln(alpha * h_ref[...] + mix, g1_ref[...], b1_ref[...])

    x = x_sc[...]
    tm = x.shape[0]
    xo = x if precise else x.astype(BF16)
    eg = N_EXPERTS // N_GROUPS

    @pl.when(grp == 0)
    def _():
        s = _sigmoid(_mm(wr_ref[...], xo, precise, _NT))
        sb = s + br_ref[...]
        srow = [s[j:j + 1, :] for j in range(N_EXPERTS)]
        brow = [sb[j:j + 1, :] for j in range(N_EXPERTS)]
        gscore = []
        for g in range(N_GROUPS):
            best = None
            for i in range(eg):
                for j in range(i + 1, eg):
                    pair = brow[g * eg + i] + brow[g * eg + j]
                    best = pair if best is None else jnp.maximum(best, pair)
            gscore.append(best)
        gmax = functools.reduce(jnp.maximum, gscore)
        taken = jnp.zeros_like(gmax) > 1.0
        sel = [None] * N_EXPERTS
        for g in range(N_GROUPS):
            is_g = jnp.logical_and(jnp.logical_not(taken), gscore[g] == gmax)
            taken = jnp.logical_or(taken, is_g)
            for i in range(eg):
                a = brow[g * eg + i]
                rank = jnp.zeros(a.shape, I32)
                for j in range(eg):
                    if j == i:
                        continue
                    o = brow[g * eg + j]
                    ahead = (o >= a) if j < i else (o > a)
                    rank = rank + jnp.where(ahead, 1, 0)
                sel[g * eg + i] = jnp.logical_and(is_g, rank < TOP_K)
        denom = jnp.zeros_like(gmax)
        for j in range(N_EXPERTS):
            denom = denom + jnp.where(sel[j], srow[j], 0.0)
        gates = [jnp.where(sel[j], srow[j] / denom, 0.0) for j in range(N_EXPERTS)]
        gt = jnp.concatenate(gates + [jnp.zeros((LANES - N_EXPERTS, tm), F32)], axis=0)
        gate_sc[...] = gt.T
        acc_sc[...] = jnp.zeros_like(acc_sc)

    lane = lax.broadcasted_iota(I32, (1, LANES), 1)
    gate = gate_sc[...]
    hes = []
    for j in range(eg):
        gcol = jnp.sum(jnp.where(lane == grp * eg + j, gate, 0.0), axis=1, keepdims=True)
        hes.append(gcol * (_silu(_mm(xo, wg_ref[j], precise)) * _mm(xo, wu_ref[j], precise)))
    acc_sc[...] += _mm(jnp.concatenate(hes, axis=1), wd_ref[0], precise)

    @pl.when(grp == N_GROUPS - 1)
    def _():
        o_ref[...] = _ln(alpha * x + acc_sc[...], g_ref[...], b_ref[...])


def _mix_moe(ya, yb, yc, h, w_out_l, g1, b1, wr_t, br_col, w_gate, w_up, w_down, g, b, alpha, precise):
    n, d = h.shape
    tm = _pick_tile(n, MOE_TILE)
    assert tm % LANES == 0, tm
    eg = N_EXPERTS // N_GROUPS
    row = lambda i, e: (i, 0)
    fix = lambda i, e: (0, 0)
    per_g = lambda i, e: (e, 0, 0)
    return pl.pallas_call(
        functools.partial(_moe_kernel, alpha=alpha, precise=precise),
        grid=(n // tm, N_GROUPS),
        in_specs=[pl.BlockSpec((tm, W_A), row), pl.BlockSpec((tm, W_B), row), pl.BlockSpec((tm, W_C), row),
                  pl.BlockSpec((tm, d), row), pl.BlockSpec(w_out_l.shape, fix),
                  pl.BlockSpec((1, d), fix), pl.BlockSpec((1, d), fix),
                  pl.BlockSpec((LANES, d), fix), pl.BlockSpec((LANES, 1), fix),
                  pl.BlockSpec((eg, d, D_EXPERT), per_g), pl.BlockSpec((eg, d, D_EXPERT), per_g),
                  pl.BlockSpec((1, eg * D_EXPERT, d), per_g), pl.BlockSpec((1, d), fix), pl.BlockSpec((1, d), fix)],
        out_specs=pl.BlockSpec((tm, d), row),
        out_shape=jax.ShapeDtypeStruct((n, d), F32),
        scratch_shapes=[pltpu.VMEM((tm, d), F32), pltpu.VMEM((tm, d), F32), pltpu.VMEM((tm, LANES), F32)],
        compiler_params=_params(("parallel", "arbitrary")),
        name="mix_moe",
    )(ya, yb, yc, h, w_out_l, g1.reshape(1, d), b1.reshape(1, d), wr_t, br_col, w_gate, w_up,
      w_down.reshape(N_GROUPS, eg * D_EXPERT, d), g.reshape(1, d), b.reshape(1, d))


def _pack_in_weights(w_in_l, b_in_l):
    offs = {}
    off = 0
    for name, n in _SPLITS:
        offs[name] = (off, n)
        off += n
    wcols, bcols = [], []
    for grp in (_GROUP_A, _GROUP_B, _GROUP_C):
        for name, n in grp:
            if name is None:
                wcols.append(jnp.zeros((w_in_l.shape[0], n), w_in_l.dtype))
                bcols.append(jnp.zeros((n,), b_in_l.dtype))
            else:
                o, m = offs[name]
                assert m == n
                wcols.append(w_in_l[:, o:o + n])
                bcols.append(b_in_l[o:o + n])
    return jnp.concatenate(wcols, axis=1), jnp.concatenate(bcols).reshape(1, -1).astype(F32)


def _block_diag_embed(x):
    B, H, r, c = x.shape
    eye = jnp.eye(H, dtype=x.dtype)
    return (x[:, :, :, None, :] * eye[None, :, None, :, None]).reshape(B, H * r, H * c)


def _block_diag_extract(x, H):
    B, R, C = x.shape
    r, c = R // H, C // H
    x5 = x.reshape(B, H, r, H, c)
    return jnp.stack([x5[:, h, :, h, :] for h in range(H)], axis=1)


def _mixer_states_in(s_gla, s_c, s_n, s_m, s_conv):
    nb = s_gla.shape[0]
    s0t = _block_diag_embed(jnp.swapaxes(s_gla.astype(F32), -1, -2))
    c0 = _block_diag_embed(jnp.swapaxes(s_c.astype(F32), -1, -2))
    n0 = s_n.astype(F32).reshape(nb, 1, W_B)
    m0 = jnp.pad(s_m.astype(F32), ((0, 0), (0, LANES - H_B))).reshape(nb, 1, LANES)
    conv0 = jnp.pad(s_conv.astype(F32), ((0, 0), (SUBLANES - (CONV_W - 1), 0), (0, 0)))
    return s0t, c0, n0, m0, conv0


def _key_blocks(k, v, ki, dtype):
    nb, s, _ = k.shape
    s_even = -(-s // (2 * KEY_BLOCK)) * 2 * KEY_BLOCK
    k, v, ki = (jnp.pad(a, ((0, 0), (0, s_even - s), (0, 0))) for a in (k, v, ki))
    nkb = s_even // KEY_BLOCK
    kk = k.astype(dtype).reshape(nb, nkb, KEY_BLOCK, -1)
    vt = jnp.swapaxes(v.astype(dtype).reshape(nb, nkb, KEY_BLOCK, -1), -1, -2)
    kib = ki.astype(dtype).reshape(nb, nkb, KEY_BLOCK, -1)
    return kk, vt, kib


def _layer(h, stream, l, p):
    precise = stream['precise']
    wdt = F32 if precise else BF16
    (w_in_p, b_in_p, wg, bg, gn_a, cw, cb, bf, gn_b, w_out_l, ln1_g, ln1_b, wr_t, br_col,
     w_gate, w_up, w_down, ln2_g, ln2_b, ln_in_g, ln_in_b, alpha) = p
    h, ua, ub, uc = _inproj(h, ln_in_g, ln_in_b, w_in_p.astype(wdt), b_in_p, apply_ln=(l == 0), precise=precise)
    nb, nc, L, lo, hi = stream['nb'], stream['nc'], stream['L'], stream['lo'], stream['hi']
    s0t, c0, n0, m0, conv0 = stream['states'](l)
    ya, yb, st_out, c_out, n_out, m_out = _scan(ua, ub, wg.astype(wdt), bg, gn_a, cw, cb, bf, gn_b, s0t, c0, n0, m0,
                                                conv0, nb=nb, nc=nc, L=L, lo=lo, hi=hi, precise=precise)
    k_arr, vt_arr, ki_arr = stream['keys'](l, uc, wdt)
    yc = _dsa(uc, k_arr, vt_arr, ki_arr, nb=nb, precise=precise, **stream['dsa'])
    h2 = _mix_moe(ya, yb, yc, h, w_out_l.astype(wdt), ln1_g, ln1_b, wr_t.astype(wdt), br_col, w_gate.astype(wdt),
                  w_up.astype(wdt), w_down.astype(wdt), ln2_g, ln2_b, alpha, precise)
    S = jnp.swapaxes(_block_diag_extract(st_out, H_A), -1, -2)
    C = jnp.swapaxes(_block_diag_extract(c_out, H_B), -1, -2)
    return h2, (ub, uc, S, C, n_out.reshape(nb, H_B, DH_B), m_out[:, 0, :H_B])


def kernel(x_prompt, x_sample, cache_k, cache_v, cache_kidx, state_gla, state_mlstm_c, state_mlstm_n, state_mlstm_m, state_conv, meta_tokens, ln_in_g, ln_in_b, w_in, b_in, w_gla_gate, b_gla_gate, g_gla_norm, conv_w, conv_b, b_forget, g_mlstm_norm, w_out, ln1_g, ln1_b, w_router, b_router, w_gate, w_up, w_down, ln2_g, ln2_b):
    depth = w_in.shape[0]
    alpha = float((2 * depth) ** 0.25)
    B, S_p, D = x_prompt.shape
    DB, Ts, _ = x_sample.shape
    P = cache_k.shape[2]
    assert S_p % CHUNK == 0 and CHUNK % Ts == 0 and P % CHUNK == 0 and Ts >= CONV_W - 1
    assert DB * Ts == LANES and Ts % SUBLANES == 0
    KB = KEY_BLOCK

    wr_t = jnp.pad(w_router.T, ((0, LANES - N_EXPERTS), (0, 0)))
    br_col = jnp.pad(b_router.astype(F32), (0, LANES - N_EXPERTS)).reshape(LANES, 1)

    def layer_params(l):
        w_in_p, b_in_p = _pack_in_weights(w_in[l], b_in[l])
        wg = jnp.pad(w_gla_gate[l], ((0, LANES - GATE_RANK), (0, 0)))
        bf = jnp.pad(b_forget[l].astype(F32), (H_B, LANES - 2 * H_B)).reshape(1, LANES)
        return (w_in_p, b_in_p, wg, b_gla_gate[l].reshape(1, -1), g_gla_norm[l].reshape(1, -1),
                conv_w[l], conv_b[l].reshape(1, -1), bf, g_mlstm_norm[l].reshape(1, -1), w_out[l],
                ln1_g[l], ln1_b[l], wr_t, br_col, w_gate[l], w_up[l], w_down[l], ln2_g[l], ln2_b[l],
                ln_in_g, ln_in_b, alpha)

    params = [layer_params(l) for l in range(depth)]

    front = CHUNK - N_META
    t_real = N_META + S_p
    t_pad = -(-(front + t_real) // KB) * KB
    hp = jnp.concatenate([jnp.zeros((B, front, D), x_prompt.dtype),
                          jnp.broadcast_to(meta_tokens.astype(x_prompt.dtype)[None], (B, N_META, D)), x_prompt,
                          jnp.zeros((B, t_pad - front - t_real, D), x_prompt.dtype)], axis=1).reshape(B * t_pad, D)

    def prompt_keys(l, uc, dtype):
        u3 = uc.reshape(B, t_pad, -1)
        return _key_blocks(u3[:, :, 512:640], u3[:, :, 640:768], u3[:, :, 896:896 + D_IDX], dtype)

    zero_states = _mixer_states_in(jnp.zeros((B, H_A, DK_A, DV_A), F32), jnp.zeros((B, H_B, DH_B, DH_B), F32),
                                   jnp.zeros((B, H_B, DH_B), F32), jnp.zeros((B, H_B), F32),
                                   jnp.zeros((B, CONV_W - 1, 2 * W_B), F32))
    prompt_stream = dict(nb=B, nc=t_pad // SCAN_CHUNK, L=SCAN_CHUNK, lo=front, hi=front + t_real, precise=False,
                         states=lambda l: zero_states, keys=prompt_keys,
                         dsa=dict(nq=t_pad // LANES, n_sel=min(TOPK_KEYS, S_p // 4), adm_lo=front,
                                  adm_cap=front + t_real, end_base=CHUNK, end_step=2 * CHUNK, end_half=CHUNK,
                                  out_rows=LANES, out_step=0))
    p_out = [[] for _ in range(8)]
    for l in range(depth):
        hp, (ub, uc, S, C, n_o, m_o) = _layer(hp, prompt_stream, l, params[l])
        ub3 = ub.reshape(B, t_pad, -1)[:, front:front + t_real]
        uc3 = uc.reshape(B, t_pad, -1)[:, front:front + t_real]
        vals = (uc3[:, :, 512:640].reshape(B, t_real, HKV_C, DH_C), uc3[:, :, 640:768].reshape(B, t_real, HKV_C, DH_C),
                uc3[:, :, 896:896 + D_IDX], S, C, n_o, m_o, ub3[:, t_real - (CONV_W - 1):, 0:2 * W_B])
        for acc, a in zip(p_out, vals):
            acc.append(a)
    y_prompt = hp.reshape(B, t_pad, D)[:, front + N_META:front + t_real]

    s_tot = P + Ts
    s_pad = -(-s_tot // KB) * KB

    def sample_keys(l, uc, dtype):
        u3 = uc.reshape(DB, Ts, -1)

        def cat(cache, new):
            a = jnp.concatenate([cache.astype(F32).reshape(DB, P, -1), new], axis=1)
            return jnp.pad(a, ((0, 0), (0, s_pad - s_tot), (0, 0)))

        return _key_blocks(cat(cache_k[l], u3[:, :, 512:640]), cat(cache_v[l], u3[:, :, 640:768]),
                           cat(cache_kidx[l], u3[:, :, 896:896 + D_IDX]), dtype)

    sample_stream = dict(nb=DB, nc=1, L=Ts, lo=0, hi=Ts, precise=True,
                         states=lambda l: _mixer_states_in(state_gla[l], state_mlstm_c[l], state_mlstm_n[l],
                                                           state_mlstm_m[l], state_conv[l]),
                         keys=sample_keys,
                         dsa=dict(nq=1, n_sel=min(TOPK_KEYS, s_tot // 4), adm_lo=0, adm_cap=s_tot, end_base=s_tot,
                                  end_step=0, end_half=0, out_rows=Ts, out_step=Ts))
    hs = x_sample.reshape(DB * Ts, D)
    s_out = [[] for _ in range(8)]
    for l in range(depth):
        hs, (ub, uc, S, C, n_o, m_o) = _layer(hs, sample_stream, l, params[l])
        ub3 = ub.reshape(DB, Ts, -1)
        uc3 = uc.reshape(DB, Ts, -1)
        vals = (uc3[:, :, 512:640].reshape(DB, Ts, HKV_C, DH_C), uc3[:, :, 640:768].reshape(DB, Ts, HKV_C, DH_C),
                uc3[:, :, 896:896 + D_IDX], S, C, n_o, m_o, ub3[:, Ts - (CONV_W - 1):, 0:2 * W_B])
        for acc, a in zip(s_out, vals):
            acc.append(a)
    y_sample = hs.reshape(DB, Ts, D)

    return (y_prompt, y_sample, *[jnp.stack(a) for a in p_out], *[jnp.stack(a) for a in s_out])
```
